```python
import jax, jax.numpy as jnp
from jax import lax
import numpy as np

D_MODEL = 2048
BATCH = 8
SEQ = 2048
DEPTH = 2

POOL_WIDTH = D_MODEL // 2
POOL_WINDOWS = (2, 4, 8, 16)
N_POOL_GROUPS = len(POOL_WINDOWS)
POOL_GROUP = POOL_WIDTH // N_POOL_GROUPS
HEAD_DIM = 128
N_Q_HEADS = (D_MODEL - POOL_WIDTH) // HEAD_DIM
N_KV_HEADS = 2
Q_PER_KV = N_Q_HEADS // N_KV_HEADS
ATTN_WIDTH = N_Q_HEADS * HEAD_DIM
KV_WIDTH = N_KV_HEADS * HEAD_DIM
MIX_WIDTH = POOL_WIDTH + ATTN_WIDTH
IN_WIDTH = POOL_WIDTH + ATTN_WIDTH + 2 * KV_WIDTH
WINDOW = 128
BLOCK = 128
ROPE_THETA = 500000.0
ROT_DIM = HEAD_DIM // 4
N_GROUPS = 4
EXPERTS_PER_GROUP = 8
N_EXPERTS = N_GROUPS * EXPERTS_PER_GROUP
TOP_K = 2
D_EXPERT = D_MODEL // 4
EPS = 1e-6

kernel_name = "hymba_pool_swa_hmoe_encoder"


def rms_norm(x, g):
    xf = x.astype(jnp.float32)
    y = xf * lax.rsqrt(jnp.mean(xf * xf, axis=-1, keepdims=True) + EPS)
    return (y * g.astype(jnp.float32)).astype(x.dtype)


def multiscale_pool(u, w_pool, pool_scale):
    B, S, _ = u.shape
    uf = u.astype(jnp.float32)
    cs = jnp.concatenate([jnp.zeros((B, 1, POOL_WIDTH), jnp.float32), jnp.cumsum(uf, axis=1)], axis=1)
    t = jnp.arange(S)
    outs = []
    for gi, w in enumerate(POOL_WINDOWS):
        lo = jnp.clip(t - w // 2, 0, S)
        hi = jnp.clip(t + w // 2, 0, S)
        sl = slice(gi * POOL_GROUP, (gi + 1) * POOL_GROUP)
        c = cs[:, :, sl]
        mean = (c[:, hi] - c[:, lo]) / (hi - lo).astype(jnp.float32)[None, :, None]
        outs.append(mean - uf[:, :, sl])
    y = jnp.stack(outs, axis=2).astype(u.dtype)
    y = jnp.einsum('bsgc,gcd->bsgd', y, w_pool).reshape(B, S, POOL_WIDTH)
    return y * pool_scale


def partial_rope(x, positions):
    inv_freq = ROPE_THETA ** (-(jnp.arange(0, ROT_DIM, 2, dtype=jnp.float32) / ROT_DIM))
    ang = positions.astype(jnp.float32)[..., None] * inv_freq
    cos = jnp.cos(ang)[:, :, None, :]
    sin = jnp.sin(ang)[:, :, None, :]
    xf = x.astype(jnp.float32)
    x1 = xf[..., :ROT_DIM // 2]
    x2 = xf[..., ROT_DIM // 2:ROT_DIM]
    out = jnp.concatenate([x1 * cos - x2 * sin, x2 * cos + x1 * sin, xf[..., ROT_DIM:]], axis=-1)
    return out.astype(x.dtype)


def band_attention(q, k, v, sink):
    B, S, _, _ = q.shape
    nb = S // BLOCK
    qb = q.reshape(B, nb, BLOCK, N_KV_HEADS, Q_PER_KV, HEAD_DIM)
    pad = ((0, 0), (BLOCK, BLOCK), (0, 0), (0, 0))

    def neighbour_blocks(t):
        tb = jnp.pad(t, pad).reshape(B, nb + 2, BLOCK, N_KV_HEADS, HEAD_DIM)
        return jnp.concatenate([tb[:, :-2], tb[:, 1:-1], tb[:, 2:]], axis=2)

    kb = neighbour_blocks(k)
    vb = neighbour_blocks(v)
    s = jnp.einsum('bnqkgd,bnckd->bnkgqc', qb, kb,
                   preferred_element_type=jnp.float32) * (HEAD_DIM ** -0.5)
    qi = jnp.arange(nb)[:, None, None] * BLOCK + jnp.arange(BLOCK)[None, :, None]
    kj = jnp.arange(nb)[:, None, None] * BLOCK - BLOCK + jnp.arange(3 * BLOCK)[None, None, :]
    mask = (jnp.abs(kj - qi) <= WINDOW) & (kj >= 0) & (kj < S)
    s = jnp.where(mask[None, :, None, None], s, -jnp.inf)
    sk = sink.astype(jnp.float32).reshape(1, 1, N_KV_HEADS, Q_PER_KV, 1, 1)
    m = jnp.maximum(jnp.max(s, axis=-1, keepdims=True), sk)
    p = jnp.exp(s - m)
    p = (p / (jnp.sum(p, axis=-1, keepdims=True) + jnp.exp(sk - m))).astype(v.dtype)
    o = jnp.einsum('bnkgqc,bnckd->bnqkgd', p, vb)
    return o.reshape(B, S, ATTN_WIDTH)


def hierarchical_moe(h, w_rg, b_rg, w_re, b_re, w_gate, w_up, w_down):
    B, S, D = h.shape
    t = h.reshape(-1, D)
    T = t.shape[0]
    g_prob = jax.nn.softmax((t @ w_rg).astype(jnp.float32) + b_rg, axis=-1)
    g_w, g_idx = lax.top_k(g_prob, 1)
    e_logits = ((t @ w_re).astype(jnp.float32) + b_re).reshape(T, N_GROUPS, EXPERTS_PER_GROUP)
    sel = jnp.broadcast_to(g_idx[:, :, None], (T, 1, EXPERTS_PER_GROUP))
    e_logits = jnp.take_along_axis(e_logits, sel, axis=1)[:, 0]
    e_w, e_idx = lax.top_k(jax.nn.softmax(e_logits, axis=-1), TOP_K)
    weights = g_w * (e_w / jnp.sum(e_w, axis=-1, keepdims=True))
    expert_id = g_idx * EXPERTS_PER_GROUP + e_idx
    combine = jnp.sum(jax.nn.one_hot(expert_id, N_EXPERTS, dtype=jnp.float32)
                      * weights[..., None], axis=1).astype(h.dtype)
    out = jnp.zeros_like(t)
    for e in range(N_EXPERTS):
        a = jax.nn.silu(t @ w_gate[e]) * (t @ w_up[e])
        out = out + combine[:, e:e + 1] * (a @ w_down[e])
    return out.reshape(B, S, D)


def setup_inputs(seed: int = 0) -> dict:
    key = jax.random.key(seed)
    ks = jax.random.split(key, 20)
    f32 = jnp.float32
    nrm = lambda k, shape, scale: jax.random.normal(k, shape, f32) * scale
    gain = lambda k, shape: 1.0 + 0.02 * jax.random.normal(k, shape, f32)
    x = jax.random.normal(ks[0], (BATCH, SEQ, D_MODEL), f32)
    offsets = jax.random.randint(ks[1], (BATCH, 1), 0, SEQ, dtype=jnp.int32)
    positions = (jnp.arange(SEQ, dtype=jnp.int32)[None, :] + offsets).astype(jnp.int32)
    return {
        "x": x,
        "positions": positions,
        "norm_mix": gain(ks[2], (DEPTH, D_MODEL)),
        "w_in": nrm(ks[3], (DEPTH, D_MODEL, IN_WIDTH), D_MODEL ** -0.5),
        "w_pool": nrm(ks[4], (DEPTH, N_POOL_GROUPS, POOL_GROUP, POOL_GROUP), POOL_GROUP ** -0.5),
        "pool_scale": 1.0 + 0.1 * jax.random.normal(ks[5], (DEPTH, POOL_WIDTH), f32),
        "q_norm": gain(ks[6], (DEPTH, HEAD_DIM)),
        "k_norm": gain(ks[7], (DEPTH, HEAD_DIM)),
        "sink": nrm(ks[8], (DEPTH, N_Q_HEADS), 0.5),
        "branch_gain_pool": gain(ks[9], (DEPTH, POOL_WIDTH)),
        "branch_gain_attn": gain(ks[10], (DEPTH, ATTN_WIDTH)),
        "w_out": nrm(ks[11], (DEPTH, MIX_WIDTH, D_MODEL), (2.0 * MIX_WIDTH) ** -0.5),
        "norm_ffn": gain(ks[12], (DEPTH, D_MODEL)),
        "w_router_group": nrm(ks[13], (DEPTH, D_MODEL, N_GROUPS), D_MODEL ** -0.5),
        "b_router_group": nrm(ks[14], (DEPTH, N_GROUPS), 0.01),
        "w_router_expert": nrm(ks[15], (DEPTH, D_MODEL, N_EXPERTS), D_MODEL ** -0.5),
        "b_router_expert": nrm(ks[16], (DEPTH, N_EXPERTS), 0.01),
        "w_gate": nrm(ks[17], (DEPTH, N_EXPERTS, D_MODEL, D_EXPERT), D_MODEL ** -0.5),
        "w_up": nrm(ks[18], (DEPTH, N_EXPERTS, D_MODEL, D_EXPERT), D_MODEL ** -0.5),
        "w_down": nrm(ks[19], (DEPTH, N_EXPERTS, D_EXPERT, D_MODEL), D_EXPERT ** -0.5),
    }


def reference(x, positions, norm_mix, w_in, w_pool, pool_scale, q_norm, k_norm, sink,
              branch_gain_pool, branch_gain_attn, w_out, norm_ffn, w_router_group,
              b_router_group, w_router_expert, b_router_expert, w_gate, w_up, w_down):
    B, S, _ = x.shape
    splits = [POOL_WIDTH, POOL_WIDTH + ATTN_WIDTH, POOL_WIDTH + ATTN_WIDTH + KV_WIDTH]
    for l in range(DEPTH):
        h = rms_norm(x, norm_mix[l])
        z = h @ w_in[l]
        u, q, k, v = jnp.split(z, splits, axis=-1)
        pool_out = multiscale_pool(u, w_pool[l], pool_scale[l])
        q = q.reshape(B, S, N_Q_HEADS, HEAD_DIM)
        k = k.reshape(B, S, N_KV_HEADS, HEAD_DIM)
        v = v.reshape(B, S, N_KV_HEADS, HEAD_DIM)
        q = partial_rope(rms_norm(q, q_norm[l]), positions)
        k = partial_rope(rms_norm(k, k_norm[l]), positions)
        attn_out = band_attention(q, k, v, sink[l])
        mixed = jnp.concatenate([rms_norm(pool_out, branch_gain_pool[l]),
                                 rms_norm(attn_out, branch_gain_attn[l])], axis=-1)
        x = x + mixed @ w_out[l]
        h = rms_norm(x, norm_ffn[l])
        x = x + hierarchical_moe(h, w_router_group[l], b_router_group[l], w_router_expert[l],
                                 b_router_expert[l], w_gate[l], w_up[l], w_down[l])
    return x
```

```python
import functools

import jax
import jax.numpy as jnp
from jax import lax
from jax.experimental import pallas as pl
from jax.experimental.pallas import tpu as pltpu

F32 = jnp.float32
BF16 = jnp.bfloat16
I32 = jnp.int32

D_MODEL = 2048
POOL_WIDTH = 1024
POOL_WINDOWS = (2, 4, 8, 16)
POOL_GROUP = 256
HEAD_DIM = 128
N_Q_HEADS = 8
N_KV_HEADS = 2
Q_PER_KV = 4
ATTN_WIDTH = 1024
KV_WIDTH = 256
QKV_WIDTH = ATTN_WIDTH + 2 * KV_WIDTH
IN_WIDTH = POOL_WIDTH + QKV_WIDTH
WINDOW = 128
BLOCK = 128
ROPE_THETA = 500000.0
ROT_DIM = 32
ROT_HALF = ROT_DIM // 2
N_GROUPS = 4
EXPERTS_PER_GROUP = 8
N_EXPERTS = 32
D_EXPERT = 512
EPS = 1e-6

LANES = 128
SUBLANES = 8
POOL_HALO = 8
ROUTER_LANES = 128
TM_PROJ = 256
TM_MOE = 256
TM_COMBINE = 256
VMEM_LIMIT = 56 * 1024 * 1024


def _rms(x, gain):
    return x * lax.rsqrt(jnp.mean(x * x, axis=-1, keepdims=True) + EPS) * gain


def _in_proj_kernel(x_ref, pos_ref, g_ref, w_ref, qn_ref, kn_ref, freq_ref, u_ref, qkv_ref):
    h = _rms(x_ref[...], g_ref[...]).astype(BF16)
    z = jnp.dot(h, w_ref[...], preferred_element_type=F32)
    u_ref[...] = z[:, :POOL_WIDTH]

    ang = pos_ref[...].astype(F32) * freq_ref[...]
    cos = jnp.cos(ang)
    sin = jnp.sin(ang)
    lane = lax.broadcasted_iota(I32, ang.shape, 1)
    sin_hi = jnp.where(lane >= ROT_HALF, sin, 0.0)
    sin_lo = jnp.where(lane < ROT_HALF, -sin, 0.0)

    def norm_rope(t, gain):
        y = _rms(t, gain)
        return (y * cos + pltpu.roll(y, ROT_HALF, 1) * sin_hi
                + pltpu.roll(y, HEAD_DIM - ROT_HALF, 1) * sin_lo)

    for hd in range(N_Q_HEADS + N_KV_HEADS):
        gain = qn_ref[...] if hd < N_Q_HEADS else kn_ref[...]
        src = POOL_WIDTH + hd * HEAD_DIM
        dst = hd * HEAD_DIM
        qkv_ref[:, dst:dst + HEAD_DIM] = norm_rope(z[:, src:src + HEAD_DIM], gain).astype(BF16)
    v0 = POOL_WIDTH + ATTN_WIDTH + KV_WIDTH
    qkv_ref[:, ATTN_WIDTH + KV_WIDTH:] = z[:, v0:].astype(BF16)


def _in_proj(x2, pos2, gain, w_bf, qn, kn, freq):
    t = x2.shape[0]
    row = lambda i: (i, 0)
    fixed = lambda i: (0, 0)
    return pl.pallas_call(
        _in_proj_kernel,
        grid=(t // TM_PROJ,),
        in_specs=[
            pl.BlockSpec((TM_PROJ, D_MODEL), row),
            pl.BlockSpec((TM_PROJ, 1), row),
            pl.BlockSpec((1, D_MODEL), fixed),
            pl.BlockSpec((D_MODEL, IN_WIDTH), fixed),
            pl.BlockSpec((1, HEAD_DIM), fixed),
            pl.BlockSpec((1, HEAD_DIM), fixed),
            pl.BlockSpec((1, HEAD_DIM), fixed),
        ],
        out_specs=[
            pl.BlockSpec((TM_PROJ, POOL_WIDTH), row),
            pl.BlockSpec((TM_PROJ, QKV_WIDTH), row),
        ],
        out_shape=[
            jax.ShapeDtypeStruct((t, POOL_WIDTH), F32),
            jax.ShapeDtypeStruct((t, QKV_WIDTH), BF16),
        ],
        compiler_params=pltpu.CompilerParams(
            dimension_semantics=("arbitrary",), vmem_limit_bytes=VMEM_LIMIT),
        name="in_proj",
    )(x2, pos2, gain, w_bf, qn, kn, freq)


def _mixer_kernel(sink_ref, u_ref, up_ref, un_ref, q_ref, kp_ref, kc_ref, kn_ref,
                  vp_ref, vc_ref, vn_ref, wp_ref, ps_ref, gp_ref, ga_ref, o_ref, *, seq):
    n = pl.program_id(1)
    nb = pl.num_programs(1)
    t0 = n * BLOCK

    prev = jnp.where(n > 0, up_ref[0], 0.0)
    nxt = jnp.where(n < nb - 1, un_ref[0], 0.0)
    cur = u_ref[0]
    ext = jnp.concatenate([prev, cur, nxt], axis=0)
    rows = ext.shape[0]
    tpos = t0 + lax.broadcasted_iota(I32, (BLOCK, POOL_GROUP), 0)
    pooled = []
    for gi, w in enumerate(POOL_WINDOWS):
        sl = slice(gi * POOL_GROUP, (gi + 1) * POOL_GROUP)
        acc = ext[:, sl]
        span = 1
        while span < w:
            acc = acc + pltpu.roll(acc, span, 0)
            span *= 2
        lead = w // 2 - 1
        if lead:
            acc = pltpu.roll(acc, rows - lead, 0)
        wsum = acc[POOL_HALO:POOL_HALO + BLOCK]
        lo = jnp.clip(tpos - w // 2, 0, seq)
        hi = jnp.clip(tpos + w // 2, 0, seq)
        y = wsum / (hi - lo).astype(F32) - cur[:, sl]
        y = jnp.dot(y.astype(BF16), wp_ref[gi], preferred_element_type=F32)
        pooled.append(y)
    pool = jnp.concatenate(pooled, axis=1) * ps_ref[...]
    o_ref[0, :, :POOL_WIDTH] = _rms(pool, gp_ref[...]).astype(BF16)

    r = lax.broadcasted_iota(I32, (Q_PER_KV * BLOCK, 3 * BLOCK), 0) % BLOCK
    c = lax.broadcasted_iota(I32, (Q_PER_KV * BLOCK, 3 * BLOCK), 1)
    kj = t0 - BLOCK + c
    mask = (c >= r) & (c <= r + 2 * WINDOW) & (kj >= 0) & (kj < seq)
    heads = []
    for kh in range(N_KV_HEADS):
        ks = slice(kh * HEAD_DIM, (kh + 1) * HEAD_DIM)
        q = jnp.concatenate(
            [q_ref[0, :, (kh * Q_PER_KV + g) * HEAD_DIM:(kh * Q_PER_KV + g + 1) * HEAD_DIM]
             for g in range(Q_PER_KV)], axis=0)
        k = jnp.concatenate([kp_ref[0, :, ks], kc_ref[0, :, ks], kn_ref[0, :, ks]], axis=0)
        v = jnp.concatenate([vp_ref[0, :, ks], vc_ref[0, :, ks], vn_ref[0, :, ks]], axis=0)
        s = lax.dot_general(q, k, (((1,), (1,)), ((), ())), preferred_element_type=F32)
        s = jnp.where(mask, s * (HEAD_DIM ** -0.5), -jnp.inf)
        sk = jnp.concatenate(
            [jnp.full((BLOCK, 1), sink_ref[kh * Q_PER_KV + g], F32) for g in range(Q_PER_KV)], axis=0)
        m = jnp.maximum(jnp.max(s, axis=-1, keepdims=True), sk)
        p = jnp.exp(s - m)
        den = jnp.sum(p, axis=-1, keepdims=True) + jnp.exp(sk - m)
        o = jnp.dot(p.astype(BF16), v, preferred_element_type=F32) / den
        heads.extend(o[g * BLOCK:(g + 1) * BLOCK] for g in range(Q_PER_KV))
    attn = jnp.concatenate(heads, axis=1)
    o_ref[0, :, POOL_WIDTH:] = _rms(attn, ga_ref[...]).astype(BF16)


def _mixer(u3, qkv3, sink, wp_bf, pscale, gpool, gattn):
    b, s, _ = u3.shape
    nb = s // BLOCK
    halo_per_block = BLOCK // POOL_HALO
    n_halo = s // POOL_HALO
    kcol = ATTN_WIDTH // KV_WIDTH
    vcol = kcol + 1
    fixed2 = lambda bi, n: (0, 0)
    prev_blk = lambda n: jnp.maximum(n - 1, 0)
    next_blk = lambda n: jnp.minimum(n + 1, nb - 1)
    kv_spec = lambda blk, col: pl.BlockSpec((1, BLOCK, KV_WIDTH), lambda bi, n: (bi, blk(n), col))
    same = lambda n: n
    return pl.pallas_call(
        functools.partial(_mixer_kernel, seq=s),
        grid=(b, nb),
        in_specs=[
            pl.BlockSpec(memory_space=pltpu.SMEM),
            pl.BlockSpec((1, BLOCK, POOL_WIDTH), lambda bi, n: (bi, n, 0)),
            pl.BlockSpec((1, POOL_HALO, POOL_WIDTH),
                         lambda bi, n: (bi, jnp.maximum(n * halo_per_block - 1, 0), 0)),
            pl.BlockSpec((1, POOL_HALO, POOL_WIDTH),
                         lambda bi, n: (bi, jnp.minimum((n + 1) * halo_per_block, n_halo - 1), 0)),
            pl.BlockSpec((1, BLOCK, ATTN_WIDTH), lambda bi, n: (bi, n, 0)),
            kv_spec(prev_blk, kcol), kv_spec(same, kcol), kv_spec(next_blk, kcol),
            kv_spec(prev_blk, vcol), kv_spec(same, vcol), kv_spec(next_blk, vcol),
            pl.BlockSpec((len(POOL_WINDOWS), POOL_GROUP, POOL_GROUP), lambda bi, n: (0, 0, 0)),
            pl.BlockSpec((1, POOL_WIDTH), fixed2),
            pl.BlockSpec((1, POOL_WIDTH), fixed2),
            pl.BlockSpec((1, ATTN_WIDTH), fixed2),
        ],
        out_specs=pl.BlockSpec((1, BLOCK, POOL_WIDTH + ATTN_WIDTH), lambda bi, n: (bi, n, 0)),
        out_shape=jax.ShapeDtypeStruct((b, s, POOL_WIDTH + ATTN_WIDTH), BF16),
        compiler_params=pltpu.CompilerParams(
            dimension_semantics=("arbitrary", "arbitrary"), vmem_limit_bytes=VMEM_LIMIT),
        name="mixer",
    )(sink, u3, u3, u3, qkv3, qkv3, qkv3, qkv3, qkv3, qkv3, qkv3, wp_bf, pscale, gpool, gattn)


def _split_dot(a, b_hi, b_lo):
    a_hi = a.astype(BF16)
    a_lo = (a - a_hi.astype(F32)).astype(BF16)
    return (jnp.dot(a_hi, b_hi, preferred_element_type=F32)
            + jnp.dot(a_lo, b_hi, preferred_element_type=F32)
            + jnp.dot(a_hi, b_lo, preferred_element_type=F32))


def _out_proj_kernel(m_ref, x_ref, w_ref, g_ref, rh_ref, rl_ref, rb_ref,
                     x1_ref, h_ref, ri_ref, rw_ref):
    x1 = x_ref[...] + jnp.dot(m_ref[...], w_ref[...], preferred_element_type=F32)
    x1_ref[...] = x1
    h = _rms(x1, g_ref[...])
    h_ref[...] = h
    logits = _split_dot(h, rh_ref[...], rl_ref[...]) + rb_ref[...]
    lane = lax.broadcasted_iota(I32, logits.shape, 1)
    far = jnp.int32(ROUTER_LANES)
    neg = -jnp.inf

    def top(vals):
        best = jnp.max(vals, axis=-1, keepdims=True)
        idx = jnp.min(jnp.where(vals == best, lane, far), axis=-1, keepdims=True)
        return best, idx

    gl = jnp.where((lane >= N_EXPERTS) & (lane < N_EXPERTS + N_GROUPS), logits, neg)
    gmax, gidx = top(gl)
    g_w = 1.0 / jnp.sum(jnp.exp(gl - gmax), axis=-1, keepdims=True)
    in_group = (lane < N_EXPERTS) & ((lane >> 3) == gidx - N_EXPERTS)
    el = jnp.where(in_group, logits, neg)
    m1, i1 = top(el)
    m2, i2 = top(jnp.where(lane == i1, neg, el))
    esum = jnp.sum(jnp.exp(el - m1), axis=-1, keepdims=True)
    p1 = 1.0 / esum
    p2 = jnp.exp(m2 - m1) / esum
    w1 = g_w * (p1 / (p1 + p2))
    w2 = g_w * (p2 / (p1 + p2))
    ri_ref[...] = jnp.where(lane == 0, i1, jnp.where(lane == 1, i2, 0))
    rw_ref[...] = jnp.where(lane == 0, w1, jnp.where(lane == 1, w2, 0.0))


def _out_proj(mixed2, x2, w_bf, gain, r_hi, r_lo, r_b):
    t = x2.shape[0]
    row = lambda i: (i, 0)
    fixed = lambda i: (0, 0)
    return pl.pallas_call(
        _out_proj_kernel,
        grid=(t // TM_PROJ,),
        in_specs=[
            pl.BlockSpec((TM_PROJ, D_MODEL), row),
            pl.BlockSpec((TM_PROJ, D_MODEL), row),
            pl.BlockSpec((D_MODEL, D_MODEL), fixed),
            pl.BlockSpec((1, D_MODEL), fixed),
            pl.BlockSpec((D_MODEL, ROUTER_LANES), fixed),
            pl.BlockSpec((D_MODEL, ROUTER_LANES), fixed),
            pl.BlockSpec((1, ROUTER_LANES), fixed),
        ],
        out_specs=[
            pl.BlockSpec((TM_PROJ, D_MODEL), row),
            pl.BlockSpec((TM_PROJ, D_MODEL), row),
            pl.BlockSpec((TM_PROJ, ROUTER_LANES), row),
            pl.BlockSpec((TM_PROJ, ROUTER_LANES), row),
        ],
        out_shape=[
            jax.ShapeDtypeStruct((t, D_MODEL), F32),
            jax.ShapeDtypeStruct((t, D_MODEL), F32),
            jax.ShapeDtypeStruct((t, ROUTER_LANES), I32),
            jax.ShapeDtypeStruct((t, ROUTER_LANES), F32),
        ],
        compiler_params=pltpu.CompilerParams(
            dimension_semantics=("arbitrary",), vmem_limit_bytes=VMEM_LIMIT),
        name="out_proj",
    )(mixed2, x2, w_bf, gain, r_hi, r_lo, r_b)


def _start_rows(src_hbm, idx_ref, dst, sem, n_rows):
    def body(r, carry):
        pltpu.make_async_copy(src_hbm.at[pl.ds(idx_ref[0, 0, r], 1)], dst.at[pl.ds(r, 1)], sem).start()
        return carry
    lax.fori_loop(0, n_rows, body, 0)


def _wait_rows(src_hbm, dst, sem, n_rows):
    pltpu.make_async_copy(src_hbm.at[pl.ds(0, n_rows)], dst, sem).wait()


def _moe_kernel(te_ref, nu_ref, first_ref, ahead_ref, h_hbm, wg_ref, wu_ref, wd_ref, y_ref,
                buf, sem, wg_bf, wu_bf, wd_bf):
    i = pl.program_id(0)
    n_used = nu_ref[0]

    @pl.when(i == 0)
    def _():
        _start_rows(h_hbm, first_ref, buf.at[0], sem.at[0], TM_MOE)

    @pl.when(i + 1 < n_used)
    def _():
        nxt = (i + 1) % 2
        _start_rows(h_hbm, ahead_ref, buf.at[nxt], sem.at[nxt], TM_MOE)

    @pl.when(i < n_used)
    def _():
        slot = i % 2
        _wait_rows(h_hbm, buf.at[slot], sem.at[slot], TM_MOE)

        @pl.when((i == 0) | (te_ref[i] != te_ref[jnp.maximum(i - 1, 0)]))
        def _():
            wg_bf[...] = wg_ref[0].astype(BF16)
            wu_bf[...] = wu_ref[0].astype(BF16)
            wd_bf[...] = wd_ref[0].astype(BF16)

        xb = buf[slot].astype(BF16)
        g = jnp.dot(xb, wg_bf[...], preferred_element_type=F32)
        u = jnp.dot(xb, wu_bf[...], preferred_element_type=F32)
        a = (g * jax.nn.sigmoid(g) * u).astype(BF16)
        y_ref[...] = jnp.dot(a, wd_bf[...], preferred_element_type=F32)

    @pl.when(i >= n_used)
    def _():
        y_ref[...] = jnp.zeros_like(y_ref)


def _moe_ffn(h2, w_gate, w_up, w_down, tile_expert, n_used, src3):
    n_tiles = src3.shape[0]
    wspec = lambda shape: pl.BlockSpec((1,) + shape, lambda i, te, nu: (te[i], 0, 0))
    smem_tile = lambda index_map: pl.BlockSpec((1, 1, TM_MOE), index_map, memory_space=pltpu.SMEM)
    return pl.pallas_call(
        _moe_kernel,
        grid_spec=pltpu.PrefetchScalarGridSpec(
            num_scalar_prefetch=2,
            grid=(n_tiles,),
            in_specs=[
                smem_tile(lambda i, te, nu: (0, 0, 0)),
                smem_tile(lambda i, te, nu: (jnp.minimum(i + 1, n_tiles - 1), 0, 0)),
                pl.BlockSpec(memory_space=pl.ANY),
                wspec((D_MODEL, D_EXPERT)),
                wspec((D_MODEL, D_EXPERT)),
                wspec((D_EXPERT, D_MODEL)),
            ],
            out_specs=pl.BlockSpec((TM_MOE, D_MODEL), lambda i, te, nu: (i, 0)),
            scratch_shapes=[
                pltpu.VMEM((2, TM_MOE, D_MODEL), F32),
                pltpu.SemaphoreType.DMA((2,)),
                pltpu.VMEM((D_MODEL, D_EXPERT), BF16),
                pltpu.VMEM((D_MODEL, D_EXPERT), BF16),
                pltpu.VMEM((D_EXPERT, D_MODEL), BF16),
            ],
        ),
        out_shape=jax.ShapeDtypeStruct((n_tiles * TM_MOE, D_MODEL), F32),
        compiler_params=pltpu.CompilerParams(
            dimension_semantics=("arbitrary",), vmem_limit_bytes=VMEM_LIMIT),
        name="moe_ffn",
    )(tile_expert, n_used, src3, src3, h2, w_gate, w_up, w_down)


def _combine_kernel(first_ref, ahead_ref, x_ref, w_ref, y_hbm, o_ref, buf, sem):
    i = pl.program_id(0)
    n = pl.num_programs(0)
    rows = 2 * TM_COMBINE

    @pl.when(i == 0)
    def _():
        _start_rows(y_hbm, first_ref, buf.at[0], sem.at[0], rows)

    @pl.when(i + 1 < n)
    def _():
        nxt = (i + 1) % 2
        _start_rows(y_hbm, ahead_ref, buf.at[nxt], sem.at[nxt], rows)

    slot = i % 2
    _wait_rows(y_hbm, buf.at[slot], sem.at[slot], rows)
    w = w_ref[...]
    o_ref[...] = (x_ref[...] + w[:, 0:1] * buf[slot, :TM_COMBINE]
                  + w[:, 1:2] * buf[slot, TM_COMBINE:])


def _combine(x1, rw, y, slots3):
    t = x1.shape[0]
    n = t // TM_COMBINE
    row = lambda i: (i, 0)
    smem_tile = lambda index_map: pl.BlockSpec((1, 1, 2 * TM_COMBINE), index_map,
                                               memory_space=pltpu.SMEM)
    return pl.pallas_call(
        _combine_kernel,
        grid=(n,),
        in_specs=[
            smem_tile(lambda i: (0, 0, 0)),
            smem_tile(lambda i: (jnp.minimum(i + 1, n - 1), 0, 0)),
            pl.BlockSpec((TM_COMBINE, D_MODEL), row),
            pl.BlockSpec((TM_COMBINE, ROUTER_LANES), row),
            pl.BlockSpec(memory_space=pl.ANY),
        ],
        out_specs=pl.BlockSpec((TM_COMBINE, D_MODEL), row),
        out_shape=jax.ShapeDtypeStruct((t, D_MODEL), F32),
        scratch_shapes=[
            pltpu.VMEM((2, 2 * TM_COMBINE, D_MODEL), F32),
            pltpu.SemaphoreType.DMA((2,)),
        ],
        compiler_params=pltpu.CompilerParams(
            dimension_semantics=("arbitrary",), vmem_limit_bytes=VMEM_LIMIT),
        name="combine",
    )(slots3, slots3, x1, rw, y)


def _plan_slots(expert_ids, n_tiles):
    t = expert_ids.shape[0]
    flat = expert_ids.reshape(-1)
    onehot = (flat[:, None] == jnp.arange(N_EXPERTS, dtype=I32)[None, :]).astype(I32)
    before = jnp.cumsum(onehot, axis=0) - onehot
    rank = jnp.sum(before * onehot, axis=1)
    counts = jnp.sum(onehot, axis=0)
    tiles_per = (counts + TM_MOE - 1) // TM_MOE
    tile_end = jnp.cumsum(tiles_per)
    start = (tile_end - tiles_per) * TM_MOE
    slot = start[flat] + rank
    n_used = tile_end[-1:].astype(I32)
    tile_expert = jnp.minimum(
        jnp.searchsorted(tile_end, jnp.arange(n_tiles, dtype=I32), side="right"),
        N_EXPERTS - 1).astype(I32)
    token = jnp.arange(2 * t, dtype=I32) // 2
    src = jnp.zeros((n_tiles * TM_MOE,), I32).at[slot].set(token)
    src3 = src.reshape(n_tiles, 1, TM_MOE)
    slot2 = slot.reshape(t // TM_COMBINE, TM_COMBINE, 2)
    slots3 = jnp.swapaxes(slot2, 1, 2).reshape(t // TM_COMBINE, 1, 2 * TM_COMBINE)
    return tile_expert, n_used, src3, slots3


def _rope_freq_row():
    inv_freq = ROPE_THETA ** (-(jnp.arange(0, ROT_DIM, 2, dtype=F32) / ROT_DIM))
    return jnp.concatenate(
        [inv_freq, inv_freq, jnp.zeros((HEAD_DIM - ROT_DIM,), F32)]).reshape(1, HEAD_DIM)


def _router_operands(w_rg, b_rg, w_re, b_re):
    pad = ROUTER_LANES - N_EXPERTS - N_GROUPS
    w = jnp.concatenate([w_re, w_rg, jnp.zeros((D_MODEL, pad), F32)], axis=1)
    b = jnp.concatenate([b_re, b_rg, jnp.zeros((pad,), F32)]).reshape(1, ROUTER_LANES)
    hi = w.astype(BF16)
    lo = (w - hi.astype(F32)).astype(BF16)
    return hi, lo, b


def _layer(x2, pos2, freq, b, s, norm_mix, w_in, w_pool, pool_scale, q_norm, k_norm, sink,
           gain_pool, gain_attn, w_out, norm_ffn, w_rg, b_rg, w_re, b_re, w_gate, w_up, w_down):
    t = b * s
    row = lambda v: v.reshape(1, -1)
    u, qkv = _in_proj(x2, pos2, row(norm_mix), w_in.astype(BF16), row(q_norm), row(k_norm), freq)
    mixed = _mixer(u.reshape(b, s, POOL_WIDTH), qkv.reshape(b, s, QKV_WIDTH), sink,
                   w_pool.astype(BF16), row(pool_scale), row(gain_pool), row(gain_attn))
    r_hi, r_lo, r_b = _router_operands(w_rg, b_rg, w_re, b_re)
    x1, h, ri, rw = _out_proj(mixed.reshape(t, D_MODEL), x2, w_out.astype(BF16), row(norm_ffn),
                              r_hi, r_lo, r_b)
    n_tiles = (2 * t) // TM_MOE + N_EXPERTS
    tile_expert, n_used, src3, slots3 = _plan_slots(ri[:, :2], n_tiles)
    y = _moe_ffn(h, w_gate, w_up, w_down, tile_expert, n_used, src3)
    return _combine(x1, rw, y, slots3)


def kernel(x, positions, norm_mix, w_in, w_pool, pool_scale, q_norm, k_norm, sink, branch_gain_pool,
           branch_gain_attn, w_out, norm_ffn, w_router_group, b_router_group, w_router_expert,
           b_router_expert, w_gate, w_up, w_down):
    b, s, d = x.shape
    x2 = x.reshape(b * s, d)
    pos2 = positions.reshape(b * s, 1)
    freq = _rope_freq_row()
    for l in range(norm_mix.shape[0]):
        x2 = _layer(x2, pos2, freq, b, s, norm_mix[l], w_in[l], w_pool[l], pool_scale[l], q_norm[l],
                    k_norm[l], sink[l], branch_gain_pool[l], branch_gain_attn[l], w_out[l],
                    norm_ffn[l], w_router_group[l], b_router_group[l], w_router_expert[l],
                    b_router_expert[l], w_gate[l], w_up[l], w_down[l])
    return x2.reshape(b, s, d)
```

```python
import functools

import jax
import jax.numpy as jnp
from jax import lax
from jax.experimental import pallas as pl
from jax.experimental.pallas import tpu as pltpu

F32 = jnp.float32
BF16 = jnp.bfloat16
I32 = jnp.int32

D_MODEL = 2048
POOL_WIDTH = 1024
POOL_WINDOWS = (2, 4, 8, 16)
POOL_GROUP = 256
HEAD_DIM = 128
N_Q_HEADS = 8
N_KV_HEADS = 2
Q_PER_KV = 4
ATTN_WIDTH = 1024
KV_WIDTH = 256
QKV_WIDTH = ATTN_WIDTH + 2 * KV_WIDTH
IN_WIDTH = POOL_WIDTH + QKV_WIDTH
WINDOW = 128
BLOCK = 128
ROPE_THETA = 500000.0
ROT_DIM = 32
ROT_HALF = ROT_DIM // 2
N_GROUPS = 4
EXPERTS_PER_GROUP = 8
N_EXPERTS = 32
D_EXPERT = 512
EPS = 1e-6

LANES = 128
POOL_HALO = 8
ROUTER_LANES = LANES
TM_PROJ = 256
TM_MOE = 256
TM_COMBINE = 256
PLAN_CHUNK = 2048
ISSUE_UNROLL = 16
VMEM_LIMIT = 56 * 1024 * 1024


def _rms(x, gain):
    return x * lax.rsqrt(jnp.mean(x * x, axis=-1, keepdims=True) + EPS) * gain


def _in_proj_kernel(x_ref, pos_ref, g_ref, w_ref, qn_ref, kn_ref, freq_ref, u_ref, qkv_ref):
    h = _rms(x_ref[...], g_ref[0]).astype(BF16)
    z = jnp.dot(h, w_ref[0], preferred_element_type=F32)
    u_ref[...] = z[:, :POOL_WIDTH]

    ang = pos_ref[...].astype(F32) * freq_ref[...]
    cos = jnp.cos(ang)
    sin = jnp.sin(ang)
    lane = lax.broadcasted_iota(I32, ang.shape, 1)
    sin_hi = jnp.where(lane >= ROT_HALF, sin, 0.0)
    sin_lo = jnp.where(lane < ROT_HALF, -sin, 0.0)

    def norm_rope(t, gain):
        y = _rms(t, gain)
        return (y * cos + pltpu.roll(y, ROT_HALF, 1) * sin_hi
                + pltpu.roll(y, HEAD_DIM - ROT_HALF, 1) * sin_lo)

    for hd in range(N_Q_HEADS + N_KV_HEADS):
        gain = qn_ref[0] if hd < N_Q_HEADS else kn_ref[0]
        src = POOL_WIDTH + hd * HEAD_DIM
        dst = hd * HEAD_DIM
        qkv_ref[:, dst:dst + HEAD_DIM] = norm_rope(z[:, src:src + HEAD_DIM], gain).astype(BF16)
    v0 = POOL_WIDTH + ATTN_WIDTH + KV_WIDTH
    qkv_ref[:, ATTN_WIDTH + KV_WIDTH:] = z[:, v0:].astype(BF16)


def _in_proj(l, x2, pos2, gain, w_bf, qn, kn, freq):
    t = x2.shape[0]
    row = lambda i: (i, 0)
    layer = lambda i: (l, 0, 0)
    return pl.pallas_call(
        _in_proj_kernel,
        grid=(t // TM_PROJ,),
        in_specs=[
            pl.BlockSpec((TM_PROJ, D_MODEL), row),
            pl.BlockSpec((TM_PROJ, 1), row),
            pl.BlockSpec((1, 1, D_MODEL), layer),
            pl.BlockSpec((1, D_MODEL, IN_WIDTH), layer),
            pl.BlockSpec((1, 1, HEAD_DIM), layer),
            pl.BlockSpec((1, 1, HEAD_DIM), layer),
            pl.BlockSpec((1, HEAD_DIM), lambda i: (0, 0)),
        ],
        out_specs=[
            pl.BlockSpec((TM_PROJ, POOL_WIDTH), row),
            pl.BlockSpec((TM_PROJ, QKV_WIDTH), row),
        ],
        out_shape=[
            jax.ShapeDtypeStruct((t, POOL_WIDTH), F32),
            jax.ShapeDtypeStruct((t, QKV_WIDTH), BF16),
        ],
        compiler_params=pltpu.CompilerParams(
            dimension_semantics=("arbitrary",), vmem_limit_bytes=VMEM_LIMIT),
        name="in_proj",
    )(x2, pos2, gain, w_bf, qn, kn, freq)


def _mixer_kernel(sink_ref, u_ref, up_ref, un_ref, q_ref, kp_ref, kc_ref, kn_ref,
                  vp_ref, vc_ref, vn_ref, wp_ref, ps_ref, gp_ref, ga_ref, o_ref, *, seq, layer):
    n = pl.program_id(1)
    nb = pl.num_programs(1)
    t0 = n * BLOCK

    prev = jnp.where(n > 0, up_ref[0], 0.0)
    nxt = jnp.where(n < nb - 1, un_ref[0], 0.0)
    cur = u_ref[0]
    ext = jnp.concatenate([prev, cur, nxt], axis=0)
    rows = ext.shape[0]
    tpos = t0 + lax.broadcasted_iota(I32, (BLOCK, POOL_GROUP), 0)
    pooled = []
    for gi, w in enumerate(POOL_WINDOWS):
        sl = slice(gi * POOL_GROUP, (gi + 1) * POOL_GROUP)
        acc = ext[:, sl]
        span = 1
        while span < w:
            acc = acc + pltpu.roll(acc, span, 0)
            span *= 2
        lead = w // 2 - 1
        if lead:
            acc = pltpu.roll(acc, rows - lead, 0)
        wsum = acc[POOL_HALO:POOL_HALO + BLOCK]
        lo = jnp.clip(tpos - w // 2, 0, seq)
        hi = jnp.clip(tpos + w // 2, 0, seq)
        y = wsum / (hi - lo).astype(F32) - cur[:, sl]
        y = jnp.dot(y.astype(BF16), wp_ref[0, gi], preferred_element_type=F32)
        pooled.append(y)
    pool = jnp.concatenate(pooled, axis=1) * ps_ref[0]
    o_ref[0, :, :POOL_WIDTH] = _rms(pool, gp_ref[0]).astype(BF16)

    r = lax.broadcasted_iota(I32, (Q_PER_KV * BLOCK, 3 * BLOCK), 0) % BLOCK
    c = lax.broadcasted_iota(I32, (Q_PER_KV * BLOCK, 3 * BLOCK), 1)
    kj = t0 - BLOCK + c
    mask = (c >= r) & (c <= r + 2 * WINDOW) & (kj >= 0) & (kj < seq)
    heads = []
    for kh in range(N_KV_HEADS):
        ks = slice(kh * HEAD_DIM, (kh + 1) * HEAD_DIM)
        q = jnp.concatenate(
            [q_ref[0, :, (kh * Q_PER_KV + g) * HEAD_DIM:(kh * Q_PER_KV + g + 1) * HEAD_DIM]
             for g in range(Q_PER_KV)], axis=0)
        k = jnp.concatenate([kp_ref[0, :, ks], kc_ref[0, :, ks], kn_ref[0, :, ks]], axis=0)
        v = jnp.concatenate([vp_ref[0, :, ks], vc_ref[0, :, ks], vn_ref[0, :, ks]], axis=0)
        s = lax.dot_general(q, k, (((1,), (1,)), ((), ())), preferred_element_type=F32)
        s = jnp.where(mask, s * (HEAD_DIM ** -0.5), -jnp.inf)
        sk = jnp.concatenate(
            [jnp.full((BLOCK, 1), sink_ref[layer, kh * Q_PER_KV + g], F32) for g in range(Q_PER_KV)],
            axis=0)
        m = jnp.maximum(jnp.max(s, axis=-1, keepdims=True), sk)
        p = jnp.exp(s - m)
        den = jnp.sum(p, axis=-1, keepdims=True) + jnp.exp(sk - m)
        o = jnp.dot(p.astype(BF16), v, preferred_element_type=F32) / den
        heads.extend(o[g * BLOCK:(g + 1) * BLOCK] for g in range(Q_PER_KV))
    attn = jnp.concatenate(heads, axis=1)
    o_ref[0, :, POOL_WIDTH:] = _rms(attn, ga_ref[0]).astype(BF16)


def _mixer(l, u3, qkv3, sink, wp_bf, pscale, gpool, gattn):
    b, s, _ = u3.shape
    nb = s // BLOCK
    halo_per_block = BLOCK // POOL_HALO
    n_halo = s // POOL_HALO
    kcol = ATTN_WIDTH // KV_WIDTH
    vcol = kcol + 1
    layer = lambda bi, n: (l, 0, 0)
    prev_blk = lambda n: jnp.maximum(n - 1, 0)
    next_blk = lambda n: jnp.minimum(n + 1, nb - 1)
    kv_spec = lambda blk, col: pl.BlockSpec((1, BLOCK, KV_WIDTH), lambda bi, n: (bi, blk(n), col))
    same = lambda n: n
    return pl.pallas_call(
        functools.partial(_mixer_kernel, seq=s, layer=l),
        grid=(b, nb),
        in_specs=[
            pl.BlockSpec(memory_space=pltpu.SMEM),
            pl.BlockSpec((1, BLOCK, POOL_WIDTH), lambda bi, n: (bi, n, 0)),
            pl.BlockSpec((1, POOL_HALO, POOL_WIDTH),
                         lambda bi, n: (bi, jnp.maximum(n * halo_per_block - 1, 0), 0)),
            pl.BlockSpec((1, POOL_HALO, POOL_WIDTH),
                         lambda bi, n: (bi, jnp.minimum((n + 1) * halo_per_block, n_halo - 1), 0)),
            pl.BlockSpec((1, BLOCK, ATTN_WIDTH), lambda bi, n: (bi, n, 0)),
            kv_spec(prev_blk, kcol), kv_spec(same, kcol), kv_spec(next_blk, kcol),
            kv_spec(prev_blk, vcol), kv_spec(same, vcol), kv_spec(next_blk, vcol),
            pl.BlockSpec((1, len(POOL_WINDOWS), POOL_GROUP, POOL_GROUP), lambda bi, n: (l, 0, 0, 0)),
            pl.BlockSpec((1, 1, POOL_WIDTH), layer),
            pl.BlockSpec((1, 1, POOL_WIDTH), layer),
            pl.BlockSpec((1, 1, ATTN_WIDTH), layer),
        ],
        out_specs=pl.BlockSpec((1, BLOCK, POOL_WIDTH + ATTN_WIDTH), lambda bi, n: (bi, n, 0)),
        out_shape=jax.ShapeDtypeStruct((b, s, POOL_WIDTH + ATTN_WIDTH), BF16),
        compiler_params=pltpu.CompilerParams(
            dimension_semantics=("arbitrary", "arbitrary"), vmem_limit_bytes=VMEM_LIMIT),
        name="mixer",
    )(sink, u3, u3, u3, qkv3, qkv3, qkv3, qkv3, qkv3, qkv3, qkv3, wp_bf, pscale, gpool, gattn)


def _split_dot(a, b_hi, b_lo):
    a_hi = a.astype(BF16)
    a_lo = (a - a_hi.astype(F32)).astype(BF16)
    return (jnp.dot(a_hi, b_hi, preferred_element_type=F32)
            + jnp.dot(a_lo, b_hi, preferred_element_type=F32)
            + jnp.dot(a_hi, b_lo, preferred_element_type=F32))


def _out_proj_kernel(m_ref, x_ref, w_ref, g_ref, rh_ref, rl_ref, rb_ref,
                     x1_ref, h_ref, ri_ref, rw_ref, cnt_ref, run_ref):
    @pl.when(pl.program_id(0) == 0)
    def _():
        run_ref[...] = jnp.zeros_like(run_ref)

    x1 = x_ref[...] + jnp.dot(m_ref[...], w_ref[0], preferred_element_type=F32)
    x1_ref[...] = x1
    h = _rms(x1, g_ref[0])
    h_ref[...] = h
    logits = _split_dot(h, rh_ref[0], rl_ref[0]) + rb_ref[0]
    lane = lax.broadcasted_iota(I32, logits.shape, 1)
    far = jnp.int32(ROUTER_LANES)
    neg = -jnp.inf

    def top(vals):
        best = jnp.max(vals, axis=-1, keepdims=True)
        idx = jnp.min(jnp.where(vals == best, lane, far), axis=-1, keepdims=True)
        return best, idx

    gl = jnp.where((lane >= N_EXPERTS) & (lane < N_EXPERTS + N_GROUPS), logits, neg)
    gmax, gidx = top(gl)
    g_w = 1.0 / jnp.sum(jnp.exp(gl - gmax), axis=-1, keepdims=True)
    in_group = (lane < N_EXPERTS) & ((lane >> 3) == gidx - N_EXPERTS)
    el = jnp.where(in_group, logits, neg)
    m1, i1 = top(el)
    m2, i2 = top(jnp.where(lane == i1, neg, el))
    esum = jnp.sum(jnp.exp(el - m1), axis=-1, keepdims=True)
    p1 = 1.0 / esum
    p2 = jnp.exp(m2 - m1) / esum
    w1 = g_w * (p1 / (p1 + p2))
    w2 = g_w * (p2 / (p1 + p2))

    pick1 = lane == i1
    pick2 = lane == i2
    picks = (pick1 | pick2).astype(BF16)
    tm = logits.shape[0]
    lower = (lax.broadcasted_iota(I32, (tm, tm), 1) < lax.broadcasted_iota(I32, (tm, tm), 0)).astype(BF16)
    before = jnp.dot(lower, picks, preferred_element_type=F32) + run_ref[...]
    r1 = jnp.sum(jnp.where(pick1, before, 0.0), axis=-1, keepdims=True).astype(I32)
    r2 = jnp.sum(jnp.where(pick2, before, 0.0), axis=-1, keepdims=True).astype(I32)
    total = run_ref[...] + jnp.sum(picks.astype(F32), axis=0, keepdims=True)
    run_ref[...] = total
    cnt_ref[...] = total.astype(I32)

    ri_ref[...] = jnp.where(lane == 0, i1, jnp.where(lane == 1, i2,
                            jnp.where(lane == 2, r1, jnp.where(lane == 3, r2, 0))))
    rw_ref[...] = jnp.where(lane == 0, w1, jnp.where(lane == 1, w2, 0.0))


def _out_proj(l, mixed2, x2, w_bf, gain, r_hi, r_lo, r_b):
    t = x2.shape[0]
    row = lambda i: (i, 0)
    layer = lambda i: (l, 0, 0)
    return pl.pallas_call(
        _out_proj_kernel,
        grid=(t // TM_PROJ,),
        in_specs=[
            pl.BlockSpec((TM_PROJ, D_MODEL), row),
            pl.BlockSpec((TM_PROJ, D_MODEL), row),
            pl.BlockSpec((1, D_MODEL, D_MODEL), layer),
            pl.BlockSpec((1, 1, D_MODEL), layer),
            pl.BlockSpec((1, D_MODEL, ROUTER_LANES), layer),
            pl.BlockSpec((1, D_MODEL, ROUTER_LANES), layer),
            pl.BlockSpec((1, 1, ROUTER_LANES), layer),
        ],
        out_specs=[
            pl.BlockSpec((TM_PROJ, D_MODEL), row),
            pl.BlockSpec((TM_PROJ, D_MODEL), row),
            pl.BlockSpec((TM_PROJ, ROUTER_LANES), row),
            pl.BlockSpec((TM_PROJ, ROUTER_LANES), row),
            pl.BlockSpec((1, ROUTER_LANES), lambda i: (0, 0)),
        ],
        out_shape=[
            jax.ShapeDtypeStruct((t, D_MODEL), F32),
            jax.ShapeDtypeStruct((t, D_MODEL), F32),
            jax.ShapeDtypeStruct((t, ROUTER_LANES), I32),
            jax.ShapeDtypeStruct((t, ROUTER_LANES), F32),
            jax.ShapeDtypeStruct((1, ROUTER_LANES), I32),
        ],
        scratch_shapes=[pltpu.VMEM((1, ROUTER_LANES), F32)],
        compiler_params=pltpu.CompilerParams(
            dimension_semantics=("arbitrary",), vmem_limit_bytes=VMEM_LIMIT),
        name="out_proj",
    )(mixed2, x2, w_bf, gain, r_hi, r_lo, r_b)


def _plan_kernel(cnt_ref, start_ref, slot_ref, src_ref):
    i = pl.program_id(0)
    base = i * PLAN_CHUNK

    def body(j, carry):
        for k in range(ISSUE_UNROLL):
            a = j * ISSUE_UNROLL + k
            src_ref[slot_ref[0, 0, a]] = (base + a) >> 1
        return carry
    lax.fori_loop(0, PLAN_CHUNK // ISSUE_UNROLL, body, 0)

    @pl.when(i == pl.num_programs(0) - 1)
    def _():
        n_slots = src_ref.shape[0]

        def per_expert(e, carry):
            lo = start_ref[e] + cnt_ref[e]
            hi = jnp.where(e == N_EXPERTS - 1, n_slots, start_ref[jnp.minimum(e + 1, N_EXPERTS - 1)])

            def fill(s, c2):
                src_ref[s] = 0
                return c2
            lax.fori_loop(lo, hi, fill, 0)
            return carry
        lax.fori_loop(0, N_EXPERTS, per_expert, 0)


def _plan(counts, start, slot_flat3, n_slots):
    n_chunks = slot_flat3.shape[0]
    return pl.pallas_call(
        _plan_kernel,
        grid_spec=pltpu.PrefetchScalarGridSpec(
            num_scalar_prefetch=2,
            grid=(n_chunks,),
            in_specs=[pl.BlockSpec((1, 1, PLAN_CHUNK), lambda i, c, s: (i, 0, 0),
                                   memory_space=pltpu.SMEM)],
            out_specs=pl.BlockSpec(memory_space=pltpu.SMEM),
        ),
        out_shape=jax.ShapeDtypeStruct((n_slots,), I32),
        compiler_params=pltpu.CompilerParams(dimension_semantics=("arbitrary",)),
        name="plan",
    )(counts, start, slot_flat3)


def _row_copy(src_hbm, idx_ref, dst, sem, r):
    return pltpu.make_async_copy(src_hbm.at[pl.ds(idx_ref[0, 0, r], 1)], dst.at[pl.ds(r, 1)], sem)


def _start_rows(src_hbm, idx_ref, dst, sem, n_rows):
    def body(j, carry):
        for k in range(ISSUE_UNROLL):
            _row_copy(src_hbm, idx_ref, dst, sem, j * ISSUE_UNROLL + k).start()
        return carry
    lax.fori_loop(0, n_rows // ISSUE_UNROLL, body, 0)


def _wait_rows(src_hbm, dst, sem, n_rows):
    pltpu.make_async_copy(src_hbm.at[pl.ds(0, n_rows)], dst, sem).wait()


def _moe_kernel(te_ref, nu_ref, first_ref, ahead_ref, h_hbm, wg_ref, wu_ref, wd_ref, y_ref,
                buf, sem, wg_bf, wu_bf, wd_bf):
    i = pl.program_id(0)
    n_used = nu_ref[0]

    @pl.when(i == 0)
    def _():
        _start_rows(h_hbm, first_ref, buf.at[0], sem.at[0], TM_MOE)

    @pl.when(i < n_used)
    def _():
        slot = i % 2
        other = 1 - slot
        _wait_rows(h_hbm, buf.at[slot], sem.at[slot], TM_MOE)

        @pl.when((i == 0) | (te_ref[i] != te_ref[jnp.maximum(i - 1, 0)]))
        def _():
            wg_bf[...] = wg_ref[0].astype(BF16)
            wu_bf[...] = wu_ref[0].astype(BF16)
            wd_bf[...] = wd_ref[0].astype(BF16)

        for r in range(TM_MOE):
            _row_copy(h_hbm, ahead_ref, buf.at[other], sem.at[other], r).start()
        xb = buf[slot].astype(BF16)
        g = jnp.dot(xb, wg_bf[...], preferred_element_type=F32)
        u = jnp.dot(xb, wu_bf[...], preferred_element_type=F32)
        a = (g * jax.nn.sigmoid(g) * u).astype(BF16)
        y_ref[...] = jnp.dot(a, wd_bf[...], preferred_element_type=F32)

        @pl.when(i == n_used - 1)
        def _():
            _wait_rows(h_hbm, buf.at[other], sem.at[other], TM_MOE)

    @pl.when(i >= n_used)
    def _():
        y_ref[...] = jnp.zeros_like(y_ref)


def _moe_ffn(h2, w_gate, w_up, w_down, tile_expert, n_used, src3):
    n_tiles = src3.shape[0]
    wspec = lambda shape: pl.BlockSpec((1,) + shape, lambda i, te, nu: (te[i], 0, 0))
    smem_tile = lambda index_map: pl.BlockSpec((1, 1, TM_MOE), index_map, memory_space=pltpu.SMEM)
    return pl.pallas_call(
        _moe_kernel,
        grid_spec=pltpu.PrefetchScalarGridSpec(
            num_scalar_prefetch=2,
            grid=(n_tiles,),
            in_specs=[
                smem_tile(lambda i, te, nu: (0, 0, 0)),
                smem_tile(lambda i, te, nu: (jnp.maximum(jnp.minimum(i + 1, nu[0] - 1), 0), 0, 0)),
                pl.BlockSpec(memory_space=pl.ANY),
                wspec((D_MODEL, D_EXPERT)),
                wspec((D_MODEL, D_EXPERT)),
                wspec((D_EXPERT, D_MODEL)),
            ],
            out_specs=pl.BlockSpec((TM_MOE, D_MODEL), lambda i, te, nu: (i, 0)),
            scratch_shapes=[
                pltpu.VMEM((2, TM_MOE, D_MODEL), F32),
                pltpu.SemaphoreType.DMA((2,)),
                pltpu.VMEM((D_MODEL, D_EXPERT), BF16),
                pltpu.VMEM((D_MODEL, D_EXPERT), BF16),
                pltpu.VMEM((D_EXPERT, D_MODEL), BF16),
            ],
        ),
        out_shape=jax.ShapeDtypeStruct((n_tiles * TM_MOE, D_MODEL), F32),
        compiler_params=pltpu.CompilerParams(
            dimension_semantics=("arbitrary",), vmem_limit_bytes=VMEM_LIMIT),
        name="moe_ffn",
    )(tile_expert, n_used, src3, src3, h2, w_gate, w_up, w_down)


def _combine_kernel(first_ref, ahead_ref, x_ref, w_ref, y_hbm, o_ref, buf, sem):
    i = pl.program_id(0)
    n = pl.num_programs(0)
    rows = 2 * TM_COMBINE

    @pl.when(i == 0)
    def _():
        _start_rows(y_hbm, first_ref, buf.at[0], sem.at[0], rows)

    @pl.when(i + 1 < n)
    def _():
        nxt = (i + 1) % 2
        for r in range(rows):
            _row_copy(y_hbm, ahead_ref, buf.at[nxt], sem.at[nxt], r).start()

    slot = i % 2
    _wait_rows(y_hbm, buf.at[slot], sem.at[slot], rows)
    w = w_ref[...]
    o_ref[...] = (x_ref[...] + w[:, 0:1] * buf[slot, :TM_COMBINE]
                  + w[:, 1:2] * buf[slot, TM_COMBINE:])


def _combine(x1, rw, y, slots3):
    t = x1.shape[0]
    n = t // TM_COMBINE
    row = lambda i: (i, 0)
    smem_tile = lambda index_map: pl.BlockSpec((1, 1, 2 * TM_COMBINE), index_map,
                                               memory_space=pltpu.SMEM)
    return pl.pallas_call(
        _combine_kernel,
        grid=(n,),
        in_specs=[
            smem_tile(lambda i: (0, 0, 0)),
            smem_tile(lambda i: (jnp.minimum(i + 1, n - 1), 0, 0)),
            pl.BlockSpec((TM_COMBINE, D_MODEL), row),
            pl.BlockSpec((TM_COMBINE, ROUTER_LANES), row),
            pl.BlockSpec(memory_space=pl.ANY),
        ],
        out_specs=pl.BlockSpec((TM_COMBINE, D_MODEL), row),
        out_shape=jax.ShapeDtypeStruct((t, D_MODEL), F32),
        scratch_shapes=[
            pltpu.VMEM((2, 2 * TM_COMBINE, D_MODEL), F32),
            pltpu.SemaphoreType.DMA((2,)),
        ],
        compiler_params=pltpu.CompilerParams(
            dimension_semantics=("arbitrary",), vmem_limit_bytes=VMEM_LIMIT),
        name="combine",
    )(slots3, slots3, x1, rw, y)


def _slot_tables(l, ri, counts_row, n_tiles):
    t = ri.shape[0]
    experts = jnp.arange(N_EXPERTS, dtype=I32)
    counts = counts_row[0, :N_EXPERTS]
    tiles_per = (counts + TM_MOE - 1) // TM_MOE
    tile_end = jnp.cumsum(tiles_per)
    start = (tile_end - tiles_per) * TM_MOE
    n_used = tile_end[-1:].astype(I32)
    tile_idx = jnp.arange(n_tiles, dtype=I32)
    tile_expert = jnp.minimum(jnp.sum((tile_end[None, :] <= tile_idx[:, None]).astype(I32), axis=1),
                              N_EXPERTS - 1) + l * N_EXPERTS
    eid = ri[:, 0:2]
    slot = ri[:, 2:4] + jnp.sum(jnp.where(eid[..., None] == experts, start, 0), axis=-1)
    slot_flat3 = slot.reshape((2 * t) // PLAN_CHUNK, 1, PLAN_CHUNK)
    slot2 = slot.reshape(t // TM_COMBINE, TM_COMBINE, 2)
    slots3 = jnp.swapaxes(slot2, 1, 2).reshape(t // TM_COMBINE, 1, 2 * TM_COMBINE)
    return counts, start.astype(I32), tile_expert.astype(I32), n_used, slot_flat3, slots3


def _rope_freq_row():
    inv_freq = ROPE_THETA ** (-(jnp.arange(0, ROT_DIM, 2, dtype=F32) / ROT_DIM))
    return jnp.concatenate(
        [inv_freq, inv_freq, jnp.zeros((HEAD_DIM - ROT_DIM,), F32)]).reshape(1, HEAD_DIM)


def _router_operands(w_rg, b_rg, w_re, b_re):
    depth = w_rg.shape[0]
    pad = ROUTER_LANES - N_EXPERTS - N_GROUPS
    w = jnp.concatenate([w_re, w_rg, jnp.zeros((depth, D_MODEL, pad), F32)], axis=2)
    b = jnp.concatenate([b_re, b_rg, jnp.zeros((depth, pad), F32)], axis=1)
    hi = w.astype(BF16)
    lo = (w - hi.astype(F32)).astype(BF16)
    return hi, lo, b


def kernel(x, positions, norm_mix, w_in, w_pool, pool_scale, q_norm, k_norm, sink, branch_gain_pool,
           branch_gain_attn, w_out, norm_ffn, w_router_group, b_router_group, w_router_expert,
           b_router_expert, w_gate, w_up, w_down):
    b, s, d = x.shape
    t = b * s
    depth = norm_mix.shape[0]
    x2 = x.reshape(t, d)
    pos2 = positions.reshape(t, 1)
    freq = _rope_freq_row()
    w_in_bf = w_in.astype(BF16)
    w_out_bf = w_out.astype(BF16)
    w_pool_bf = w_pool.astype(BF16)
    r_hi, r_lo, r_b = _router_operands(w_router_group, b_router_group, w_router_expert,
                                       b_router_expert)
    wg = w_gate.reshape(depth * N_EXPERTS, D_MODEL, D_EXPERT)
    wu = w_up.reshape(depth * N_EXPERTS, D_MODEL, D_EXPERT)
    wd = w_down.reshape(depth * N_EXPERTS, D_EXPERT, D_MODEL)
    n_tiles = (2 * t) // TM_MOE + N_EXPERTS
    rows = lambda v: v.reshape(depth, 1, v.shape[-1])
    for l in range(depth):
        u, qkv = _in_proj(l, x2, pos2, rows(norm_mix), w_in_bf, rows(q_norm), rows(k_norm), freq)
        mixed = _mixer(l, u.reshape(b, s, POOL_WIDTH), qkv.reshape(b, s, QKV_WIDTH), sink, w_pool_bf,
                       rows(pool_scale), rows(branch_gain_pool), rows(branch_gain_attn))
        x1, h, ri, rw, counts_row = _out_proj(l, mixed.reshape(t, D_MODEL), x2, w_out_bf,
                                              rows(norm_ffn), r_hi, r_lo, rows(r_b))
        counts, start, tile_expert, n_used, slot_flat3, slots3 = _slot_tables(l, ri, counts_row, n_tiles)
        src = _plan(counts, start, slot_flat3, n_tiles * TM_MOE)
        y = _moe_ffn(h, wg, wu, wd, tile_expert, n_used, src.reshape(n_tiles, 1, TM_MOE))
        x2 = _combine(x1, rw, y, slots3)
    return x2.reshape(b, s, d)
```

```python
import functools

import jax
import jax.numpy as jnp
from jax import lax
from jax.experimental import pallas as pl
from jax.experimental.pallas import tpu as pltpu

F32 = jnp.float32
BF16 = jnp.bfloat16
I32 = jnp.int32

D_MODEL = 2048
POOL_WIDTH = 1024
POOL_WINDOWS = (2, 4, 8, 16)
POOL_GROUP = 256
HEAD_DIM = 128
N_Q_HEADS = 8
N_KV_HEADS = 2
Q_PER_KV = 4
ATTN_WIDTH = 1024
KV_WIDTH = 256
QKV_WIDTH = ATTN_WIDTH + 2 * KV_WIDTH
IN_WIDTH = POOL_WIDTH + QKV_WIDTH
WINDOW = 128
BLOCK = 128
ROPE_THETA = 500000.0
ROT_DIM = 32
ROT_HALF = ROT_DIM // 2
N_GROUPS = 4
EXPERTS_PER_GROUP = 8
N_EXPERTS = 32
D_EXPERT = 512
EPS = 1e-6

LANES = 128
POOL_HALO = 8
ROUTER_LANES = LANES
TM_PROJ = 256
MIX_BLOCKS = 2
TM_MOE = 256
TM_COMBINE = 256
PLAN_CHUNK = 8192
ISSUE_UNROLL = 16
VMEM_LIMIT = 56 * 1024 * 1024


def _rms(x, gain):
    return x * lax.rsqrt(jnp.mean(x * x, axis=-1, keepdims=True) + EPS) * gain


def _in_proj_kernel(x_ref, pos_ref, g_ref, w_ref, qn_ref, kn_ref, freq_ref, u_ref, qkv_ref):
    h = _rms(x_ref[...], g_ref[0]).astype(BF16)
    z = jnp.dot(h, w_ref[0], preferred_element_type=F32)
    u_ref[...] = z[:, :POOL_WIDTH]

    ang = pos_ref[...].astype(F32) * freq_ref[...]
    cos = jnp.cos(ang)
    sin = jnp.sin(ang)
    lane = lax.broadcasted_iota(I32, ang.shape, 1)
    sin_hi = jnp.where(lane >= ROT_HALF, sin, 0.0)
    sin_lo = jnp.where(lane < ROT_HALF, -sin, 0.0)

    def norm_rope(t, gain):
        y = _rms(t, gain)
        return (y * cos + pltpu.roll(y, ROT_HALF, 1) * sin_hi
                + pltpu.roll(y, HEAD_DIM - ROT_HALF, 1) * sin_lo)

    for hd in range(N_Q_HEADS + N_KV_HEADS):
        gain = qn_ref[0] if hd < N_Q_HEADS else kn_ref[0]
        src = POOL_WIDTH + hd * HEAD_DIM
        dst = hd * HEAD_DIM
        qkv_ref[:, dst:dst + HEAD_DIM] = norm_rope(z[:, src:src + HEAD_DIM], gain).astype(BF16)
    v0 = POOL_WIDTH + ATTN_WIDTH + KV_WIDTH
    qkv_ref[:, ATTN_WIDTH + KV_WIDTH:] = z[:, v0:].astype(BF16)


def _in_proj(l, x2, pos2, gain, w_bf, qn, kn, freq):
    t = x2.shape[0]
    row = lambda i: (i, 0)
    layer = lambda i: (l, 0, 0)
    return pl.pallas_call(
        _in_proj_kernel,
        grid=(t // TM_PROJ,),
        in_specs=[
            pl.BlockSpec((TM_PROJ, D_MODEL), row),
            pl.BlockSpec((TM_PROJ, 1), row),
            pl.BlockSpec((1, 1, D_MODEL), layer),
            pl.BlockSpec((1, D_MODEL, IN_WIDTH), layer),
            pl.BlockSpec((1, 1, HEAD_DIM), layer),
            pl.BlockSpec((1, 1, HEAD_DIM), layer),
            pl.BlockSpec((1, HEAD_DIM), lambda i: (0, 0)),
        ],
        out_specs=[
            pl.BlockSpec((TM_PROJ, POOL_WIDTH), row),
            pl.BlockSpec((TM_PROJ, QKV_WIDTH), row),
        ],
        out_shape=[
            jax.ShapeDtypeStruct((t, POOL_WIDTH), F32),
            jax.ShapeDtypeStruct((t, QKV_WIDTH), BF16),
        ],
        compiler_params=pltpu.CompilerParams(
            dimension_semantics=("arbitrary",), vmem_limit_bytes=VMEM_LIMIT),
        name="in_proj",
    )(x2, pos2, gain, w_bf, qn, kn, freq)


def _mixer_kernel(sink_ref, u_ref, up_ref, un_ref, q_ref, kp_ref, kc_ref, kn_ref,
                  vp_ref, vc_ref, vn_ref, wp_ref, ps_ref, gp_ref, ga_ref, o_ref, *, seq, layer):
    n = pl.program_id(1)
    n_steps = pl.num_programs(1)
    step_rows = MIX_BLOCKS * BLOCK
    t0 = n * step_rows

    prev = jnp.where(n > 0, up_ref[0], 0.0)
    nxt = jnp.where(n < n_steps - 1, un_ref[0], 0.0)
    cur = u_ref[0]
    ext = jnp.concatenate([prev, cur, nxt], axis=0)
    rows = ext.shape[0]
    tpos = t0 + lax.broadcasted_iota(I32, (step_rows, POOL_GROUP), 0)
    pooled = []
    for gi, w in enumerate(POOL_WINDOWS):
        sl = slice(gi * POOL_GROUP, (gi + 1) * POOL_GROUP)
        acc = ext[:, sl]
        span = 1
        while span < w:
            acc = acc + pltpu.roll(acc, span, 0)
            span *= 2
        lead = w // 2 - 1
        if lead:
            acc = pltpu.roll(acc, rows - lead, 0)
        wsum = acc[POOL_HALO:POOL_HALO + step_rows]
        lo = jnp.clip(tpos - w // 2, 0, seq)
        hi = jnp.clip(tpos + w // 2, 0, seq)
        y = wsum / (hi - lo).astype(F32) - cur[:, sl]
        y = jnp.dot(y.astype(BF16), wp_ref[0, gi], preferred_element_type=F32)
        pooled.append(y)
    pool = jnp.concatenate(pooled, axis=1) * ps_ref[0]
    o_ref[0, :, :POOL_WIDTH] = _rms(pool, gp_ref[0]).astype(BF16)

    r = lax.broadcasted_iota(I32, (Q_PER_KV * BLOCK, 3 * BLOCK), 0) % BLOCK
    c = lax.broadcasted_iota(I32, (Q_PER_KV * BLOCK, 3 * BLOCK), 1)
    band = (c >= r) & (c <= r + 2 * WINDOW)
    kcat, vcat = [], []
    for kh in range(N_KV_HEADS):
        ks = slice(kh * HEAD_DIM, (kh + 1) * HEAD_DIM)
        kcat.append(jnp.concatenate([kp_ref[0, :, ks], kc_ref[0, :, ks], kn_ref[0, :, ks]], axis=0))
        vcat.append(jnp.concatenate([vp_ref[0, :, ks], vc_ref[0, :, ks], vn_ref[0, :, ks]], axis=0))
    for j in range(MIX_BLOCKS):
        kj = t0 + (j - 1) * BLOCK + c
        mask = band & (kj >= 0) & (kj < seq)
        qrows = slice(j * BLOCK, (j + 1) * BLOCK)
        krows = slice(j * BLOCK, (j + 3) * BLOCK)
        heads = []
        for kh in range(N_KV_HEADS):
            q = jnp.concatenate(
                [q_ref[0, qrows, (kh * Q_PER_KV + g) * HEAD_DIM:(kh * Q_PER_KV + g + 1) * HEAD_DIM]
                 for g in range(Q_PER_KV)], axis=0)
            s = lax.dot_general(q, kcat[kh][krows], (((1,), (1,)), ((), ())),
                                preferred_element_type=F32)
            s = jnp.where(mask, s * (HEAD_DIM ** -0.5), -jnp.inf)
            ps, dens = [], []
            for g in range(Q_PER_KV):
                sg = s[g * BLOCK:(g + 1) * BLOCK]
                sink = sink_ref[layer, kh * Q_PER_KV + g]
                m = jnp.maximum(jnp.max(sg, axis=-1, keepdims=True), sink)
                pg = jnp.exp(sg - m)
                dens.append(jnp.sum(pg, axis=-1, keepdims=True) + jnp.exp(sink - m))
                ps.append(pg.astype(BF16))
            o = jnp.dot(jnp.concatenate(ps, axis=0), vcat[kh][krows], preferred_element_type=F32)
            heads.extend(o[g * BLOCK:(g + 1) * BLOCK] / dens[g] for g in range(Q_PER_KV))
        attn = jnp.concatenate(heads, axis=1)
        o_ref[0, qrows, POOL_WIDTH:] = _rms(attn, ga_ref[0]).astype(BF16)


def _mixer(l, u3, qkv3, sink, wp_bf, pscale, gpool, gattn):
    b, s, _ = u3.shape
    nb = s // BLOCK
    step_rows = MIX_BLOCKS * BLOCK
    n_steps = s // step_rows
    halo_per_step = step_rows // POOL_HALO
    n_halo = s // POOL_HALO
    kcol = ATTN_WIDTH // KV_WIDTH
    vcol = kcol + 1
    layer = lambda bi, n: (l, 0, 0)
    prev_blk = lambda n: jnp.maximum(n * MIX_BLOCKS - 1, 0)
    next_blk = lambda n: jnp.minimum((n + 1) * MIX_BLOCKS, nb - 1)
    edge_spec = lambda blk, col: pl.BlockSpec((1, BLOCK, KV_WIDTH), lambda bi, n: (bi, blk(n), col))
    body_spec = lambda col: pl.BlockSpec((1, step_rows, KV_WIDTH), lambda bi, n: (bi, n, col))
    return pl.pallas_call(
        functools.partial(_mixer_kernel, seq=s, layer=l),
        grid=(b, n_steps),
        in_specs=[
            pl.BlockSpec(memory_space=pltpu.SMEM),
            pl.BlockSpec((1, step_rows, POOL_WIDTH), lambda bi, n: (bi, n, 0)),
            pl.BlockSpec((1, POOL_HALO, POOL_WIDTH),
                         lambda bi, n: (bi, jnp.maximum(n * halo_per_step - 1, 0), 0)),
            pl.BlockSpec((1, POOL_HALO, POOL_WIDTH),
                         lambda bi, n: (bi, jnp.minimum((n + 1) * halo_per_step, n_halo - 1), 0)),
            pl.BlockSpec((1, step_rows, ATTN_WIDTH), lambda bi, n: (bi, n, 0)),
            edge_spec(prev_blk, kcol), body_spec(kcol), edge_spec(next_blk, kcol),
            edge_spec(prev_blk, vcol), body_spec(vcol), edge_spec(next_blk, vcol),
            pl.BlockSpec((1, len(POOL_WINDOWS), POOL_GROUP, POOL_GROUP), lambda bi, n: (l, 0, 0, 0)),
            pl.BlockSpec((1, 1, POOL_WIDTH), layer),
            pl.BlockSpec((1, 1, POOL_WIDTH), layer),
            pl.BlockSpec((1, 1, ATTN_WIDTH), layer),
        ],
        out_specs=pl.BlockSpec((1, step_rows, POOL_WIDTH + ATTN_WIDTH), lambda bi, n: (bi, n, 0)),
        out_shape=jax.ShapeDtypeStruct((b, s, POOL_WIDTH + ATTN_WIDTH), BF16),
        compiler_params=pltpu.CompilerParams(
            dimension_semantics=("arbitrary", "arbitrary"), vmem_limit_bytes=VMEM_LIMIT),
        name="mixer",
    )(sink, u3, u3, u3, qkv3, qkv3, qkv3, qkv3, qkv3, qkv3, qkv3, wp_bf, pscale, gpool, gattn)


def _split_dot(a, b_hi, b_lo):
    a_hi = a.astype(BF16)
    a_lo = (a - a_hi.astype(F32)).astype(BF16)
    return (jnp.dot(a_hi, b_hi, preferred_element_type=F32)
            + jnp.dot(a_lo, b_hi, preferred_element_type=F32)
            + jnp.dot(a_hi, b_lo, preferred_element_type=F32))


def _out_proj_kernel(m_ref, x_ref, w_ref, g_ref, rh_ref, rl_ref, rb_ref,
                     x1_ref, h_ref, ri_ref, rw_ref, cnt_ref, run_ref):
    @pl.when(pl.program_id(0) == 0)
    def _():
        run_ref[...] = jnp.zeros_like(run_ref)

    x1 = x_ref[...] + jnp.dot(m_ref[...], w_ref[0], preferred_element_type=F32)
    x1_ref[...] = x1
    h = _rms(x1, g_ref[0])
    h_ref[...] = h
    logits = _split_dot(h, rh_ref[0], rl_ref[0]) + rb_ref[0]
    lane = lax.broadcasted_iota(I32, logits.shape, 1)
    far = jnp.int32(ROUTER_LANES)
    neg = -jnp.inf

    def top(vals):
        best = jnp.max(vals, axis=-1, keepdims=True)
        idx = jnp.min(jnp.where(vals == best, lane, far), axis=-1, keepdims=True)
        return best, idx

    gl = jnp.where((lane >= N_EXPERTS) & (lane < N_EXPERTS + N_GROUPS), logits, neg)
    gmax, gidx = top(gl)
    g_w = 1.0 / jnp.sum(jnp.exp(gl - gmax), axis=-1, keepdims=True)
    in_group = (lane < N_EXPERTS) & ((lane >> 3) == gidx - N_EXPERTS)
    el = jnp.where(in_group, logits, neg)
    m1, i1 = top(el)
    m2, i2 = top(jnp.where(lane == i1, neg, el))
    esum = jnp.sum(jnp.exp(el - m1), axis=-1, keepdims=True)
    p1 = 1.0 / esum
    p2 = jnp.exp(m2 - m1) / esum
    w1 = g_w * (p1 / (p1 + p2))
    w2 = g_w * (p2 / (p1 + p2))

    pick1 = lane == i1
    pick2 = lane == i2
    picks = (pick1 | pick2).astype(BF16)
    tm = logits.shape[0]
    lower = (lax.broadcasted_iota(I32, (tm, tm), 1) < lax.broadcasted_iota(I32, (tm, tm), 0)).astype(BF16)
    before = jnp.dot(lower, picks, preferred_element_type=F32) + run_ref[...]
    r1 = jnp.sum(jnp.where(pick1, before, 0.0), axis=-1, keepdims=True).astype(I32)
    r2 = jnp.sum(jnp.where(pick2, before, 0.0), axis=-1, keepdims=True).astype(I32)
    total = run_ref[...] + jnp.sum(picks.astype(F32), axis=0, keepdims=True)
    run_ref[...] = total
    cnt_ref[...] = total.astype(I32)

    ri_ref[...] = jnp.where(lane == 0, i1, jnp.where(lane == 1, i2,
                            jnp.where(lane == 2, r1, jnp.where(lane == 3, r2, 0))))
    rw_ref[...] = jnp.where(lane == 0, w1, jnp.where(lane == 1, w2, 0.0))


def _out_proj(l, mixed2, x2, w_bf, gain, r_hi, r_lo, r_b):
    t = x2.shape[0]
    row = lambda i: (i, 0)
    layer = lambda i: (l, 0, 0)
    return pl.pallas_call(
        _out_proj_kernel,
        grid=(t // TM_PROJ,),
        in_specs=[
            pl.BlockSpec((TM_PROJ, D_MODEL), row),
            pl.BlockSpec((TM_PROJ, D_MODEL), row),
            pl.BlockSpec((1, D_MODEL, D_MODEL), layer),
            pl.BlockSpec((1, 1, D_MODEL), layer),
            pl.BlockSpec((1, D_MODEL, ROUTER_LANES), layer),
            pl.BlockSpec((1, D_MODEL, ROUTER_LANES), layer),
            pl.BlockSpec((1, 1, ROUTER_LANES), layer),
        ],
        out_specs=[
            pl.BlockSpec((TM_PROJ, D_MODEL), row),
            pl.BlockSpec((TM_PROJ, D_MODEL), row),
            pl.BlockSpec((TM_PROJ, ROUTER_LANES), row),
            pl.BlockSpec((TM_PROJ, ROUTER_LANES), row),
            pl.BlockSpec((1, ROUTER_LANES), lambda i: (0, 0)),
        ],
        out_shape=[
            jax.ShapeDtypeStruct((t, D_MODEL), F32),
            jax.ShapeDtypeStruct((t, D_MODEL), F32),
            jax.ShapeDtypeStruct((t, ROUTER_LANES), I32),
            jax.ShapeDtypeStruct((t, ROUTER_LANES), F32),
            jax.ShapeDtypeStruct((1, ROUTER_LANES), I32),
        ],
        scratch_shapes=[pltpu.VMEM((1, ROUTER_LANES), F32)],
        compiler_params=pltpu.CompilerParams(
            dimension_semantics=("arbitrary",), vmem_limit_bytes=VMEM_LIMIT),
        name="out_proj",
    )(mixed2, x2, w_bf, gain, r_hi, r_lo, r_b)


def _plan_kernel(cnt_ref, start_ref, slot_ref, src_ref):
    i = pl.program_id(0)
    base = i * PLAN_CHUNK

    def body(j, carry):
        for k in range(ISSUE_UNROLL):
            a = j * ISSUE_UNROLL + k
            src_ref[slot_ref[0, 0, a]] = (base + a) >> 1
        return carry
    lax.fori_loop(0, PLAN_CHUNK // ISSUE_UNROLL, body, 0)

    @pl.when(i == pl.num_programs(0) - 1)
    def _():
        n_slots = src_ref.shape[0]

        def per_expert(e, carry):
            lo = start_ref[e] + cnt_ref[e]
            hi = jnp.where(e == N_EXPERTS - 1, n_slots, start_ref[jnp.minimum(e + 1, N_EXPERTS - 1)])

            def fill(s, c2):
                src_ref[s] = 0
                return c2
            lax.fori_loop(lo, hi, fill, 0)
            return carry
        lax.fori_loop(0, N_EXPERTS, per_expert, 0)


def _plan(counts, start, slot_flat3, n_slots):
    n_chunks = slot_flat3.shape[0]
    return pl.pallas_call(
        _plan_kernel,
        grid_spec=pltpu.PrefetchScalarGridSpec(
            num_scalar_prefetch=2,
            grid=(n_chunks,),
            in_specs=[pl.BlockSpec((1, 1, PLAN_CHUNK), lambda i, c, s: (i, 0, 0),
                                   memory_space=pltpu.SMEM)],
            out_specs=pl.BlockSpec(memory_space=pltpu.SMEM),
        ),
        out_shape=jax.ShapeDtypeStruct((n_slots,), I32),
        compiler_params=pltpu.CompilerParams(dimension_semantics=("arbitrary",)),
        name="plan",
    )(counts, start, slot_flat3)


def _row_copy(src_hbm, idx_ref, dst, sem, r):
    return pltpu.make_async_copy(src_hbm.at[pl.ds(idx_ref[0, 0, r], 1)], dst.at[pl.ds(r, 1)], sem)


def _start_rows(src_hbm, idx_ref, dst, sem, n_rows):
    def body(j, carry):
        for k in range(ISSUE_UNROLL):
            _row_copy(src_hbm, idx_ref, dst, sem, j * ISSUE_UNROLL + k).start()
        return carry
    lax.fori_loop(0, n_rows // ISSUE_UNROLL, body, 0)


def _wait_rows(src_hbm, dst, sem, n_rows):
    pltpu.make_async_copy(src_hbm.at[pl.ds(0, n_rows)], dst, sem).wait()


def _moe_kernel(te_ref, grp_ref, nxt_ref, nu_ref, first_ref, ahead_ref, h_hbm, wg_hbm, wu_hbm, wd_hbm,
                y_ref, buf, sem, wg_st, wu_st, wd_st, wsem, wg_bf, wu_bf, wd_bf):
    i = pl.program_id(0)
    n_used = nu_ref[0]

    def weight_copies(e, par):
        return (pltpu.make_async_copy(wg_hbm.at[e], wg_st.at[par], wsem.at[par]),
                pltpu.make_async_copy(wu_hbm.at[e], wu_st.at[par], wsem.at[par]),
                pltpu.make_async_copy(wd_hbm.at[e], wd_st.at[par], wsem.at[par]))

    @pl.when(i == 0)
    def _():
        _start_rows(h_hbm, first_ref, buf.at[0], sem.at[0], TM_MOE)
        for c in weight_copies(te_ref[0], 0):
            c.start()

    @pl.when(i < n_used)
    def _():
        slot = i % 2
        other = 1 - slot
        _wait_rows(h_hbm, buf.at[slot], sem.at[slot], TM_MOE)

        @pl.when((i == 0) | (te_ref[i] != te_ref[jnp.maximum(i - 1, 0)]))
        def _():
            par = grp_ref[i] % 2
            for c in weight_copies(te_ref[i], par):
                c.wait()

            @pl.when(nxt_ref[i] >= 0)
            def _():
                for c in weight_copies(nxt_ref[i], 1 - par):
                    c.start()

            wg_bf[...] = wg_st[par].astype(BF16)
            wu_bf[...] = wu_st[par].astype(BF16)
            wd_bf[...] = wd_st[par].astype(BF16)

        for r in range(TM_MOE):
            _row_copy(h_hbm, ahead_ref, buf.at[other], sem.at[other], r).start()
        xb = buf[slot].astype(BF16)
        g = jnp.dot(xb, wg_bf[...], preferred_element_type=F32)
        u = jnp.dot(xb, wu_bf[...], preferred_element_type=F32)
        a = (g * jax.nn.sigmoid(g) * u).astype(BF16)
        y_ref[...] = jnp.dot(a, wd_bf[...], preferred_element_type=F32)

        @pl.when(i == n_used - 1)
        def _():
            _wait_rows(h_hbm, buf.at[other], sem.at[other], TM_MOE)

    @pl.when(i >= n_used)
    def _():
        y_ref[...] = jnp.zeros_like(y_ref)


def _moe_ffn(h2, w_gate, w_up, w_down, tile_expert, tile_group, tile_next, n_used, src3):
    n_tiles = src3.shape[0]
    smem_tile = lambda index_map: pl.BlockSpec((1, 1, TM_MOE), index_map, memory_space=pltpu.SMEM)
    hbm = pl.BlockSpec(memory_space=pl.ANY)
    return pl.pallas_call(
        _moe_kernel,
        grid_spec=pltpu.PrefetchScalarGridSpec(
            num_scalar_prefetch=4,
            grid=(n_tiles,),
            in_specs=[
                smem_tile(lambda i, te, gr, nx, nu: (0, 0, 0)),
                smem_tile(lambda i, te, gr, nx, nu:
                          (jnp.maximum(jnp.minimum(i + 1, nu[0] - 1), 0), 0, 0)),
                hbm, hbm, hbm, hbm,
            ],
            out_specs=pl.BlockSpec((TM_MOE, D_MODEL), lambda i, te, gr, nx, nu: (i, 0)),
            scratch_shapes=[
                pltpu.VMEM((2, TM_MOE, D_MODEL), F32),
                pltpu.SemaphoreType.DMA((2,)),
                pltpu.VMEM((2, D_MODEL, D_EXPERT), F32),
                pltpu.VMEM((2, D_MODEL, D_EXPERT), F32),
                pltpu.VMEM((2, D_EXPERT, D_MODEL), F32),
                pltpu.SemaphoreType.DMA((2,)),
                pltpu.VMEM((D_MODEL, D_EXPERT), BF16),
                pltpu.VMEM((D_MODEL, D_EXPERT), BF16),
                pltpu.VMEM((D_EXPERT, D_MODEL), BF16),
            ],
        ),
        out_shape=jax.ShapeDtypeStruct((n_tiles * TM_MOE, D_MODEL), F32),
        compiler_params=pltpu.CompilerParams(
            dimension_semantics=("arbitrary",), vmem_limit_bytes=VMEM_LIMIT),
        name="moe_ffn",
    )(tile_expert, tile_group, tile_next, n_used, src3, src3, h2, w_gate, w_up, w_down)


def _combine_kernel(first_ref, ahead_ref, x_ref, w_ref, y_hbm, o_ref, buf, sem):
    i = pl.program_id(0)
    n = pl.num_programs(0)
    rows = 2 * TM_COMBINE

    @pl.when(i == 0)
    def _():
        _start_rows(y_hbm, first_ref, buf.at[0], sem.at[0], rows)

    @pl.when(i + 1 < n)
    def _():
        nxt = (i + 1) % 2
        for r in range(rows):
            _row_copy(y_hbm, ahead_ref, buf.at[nxt], sem.at[nxt], r).start()

    slot = i % 2
    _wait_rows(y_hbm, buf.at[slot], sem.at[slot], rows)
    w = w_ref[...]
    o_ref[...] = (x_ref[...] + w[:, 0:1] * buf[slot, :TM_COMBINE]
                  + w[:, 1:2] * buf[slot, TM_COMBINE:])


def _combine(x1, rw, y, slots3):
    t = x1.shape[0]
    n = t // TM_COMBINE
    row = lambda i: (i, 0)
    smem_tile = lambda index_map: pl.BlockSpec((1, 1, 2 * TM_COMBINE), index_map,
                                               memory_space=pltpu.SMEM)
    return pl.pallas_call(
        _combine_kernel,
        grid=(n,),
        in_specs=[
            smem_tile(lambda i: (0, 0, 0)),
            smem_tile(lambda i: (jnp.minimum(i + 1, n - 1), 0, 0)),
            pl.BlockSpec((TM_COMBINE, D_MODEL), row),
            pl.BlockSpec((TM_COMBINE, ROUTER_LANES), row),
            pl.BlockSpec(memory_space=pl.ANY),
        ],
        out_specs=pl.BlockSpec((TM_COMBINE, D_MODEL), row),
        out_shape=jax.ShapeDtypeStruct((t, D_MODEL), F32),
        scratch_shapes=[
            pltpu.VMEM((2, 2 * TM_COMBINE, D_MODEL), F32),
            pltpu.SemaphoreType.DMA((2,)),
        ],
        compiler_params=pltpu.CompilerParams(
            dimension_semantics=("arbitrary",), vmem_limit_bytes=VMEM_LIMIT),
        name="combine",
    )(slots3, slots3, x1, rw, y)


def _slot_tables(l, ri, counts_row, n_tiles):
    t = ri.shape[0]
    experts = jnp.arange(N_EXPERTS, dtype=I32)
    counts = counts_row[0, :N_EXPERTS]
    tiles_per = (counts + TM_MOE - 1) // TM_MOE
    tile_end = jnp.cumsum(tiles_per)
    start = (tile_end - tiles_per) * TM_MOE
    n_used = tile_end[-1:].astype(I32)
    nonempty = counts > 0
    group_of = jnp.cumsum(nonempty.astype(I32)) - 1
    later = (experts[None, :] > experts[:, None]) & nonempty[None, :]
    next_of = jnp.min(jnp.where(later, experts[None, :], N_EXPERTS), axis=1)
    next_of = jnp.where(next_of < N_EXPERTS, next_of + l * N_EXPERTS, -1)
    tile_idx = jnp.arange(n_tiles, dtype=I32)
    tile_e = jnp.minimum(jnp.sum((tile_end[None, :] <= tile_idx[:, None]).astype(I32), axis=1),
                         N_EXPERTS - 1)
    pick = tile_e[:, None] == experts[None, :]
    tile_group = jnp.sum(jnp.where(pick, group_of[None, :], 0), axis=1)
    tile_next = jnp.sum(jnp.where(pick, next_of[None, :], 0), axis=1)
    tile_expert = tile_e + l * N_EXPERTS
    eid = ri[:, 0:2]
    slot = ri[:, 2:4] + jnp.sum(jnp.where(eid[..., None] == experts, start, 0), axis=-1)
    slot_flat3 = slot.reshape((2 * t) // PLAN_CHUNK, 1, PLAN_CHUNK)
    slot2 = slot.reshape(t // TM_COMBINE, TM_COMBINE, 2)
    slots3 = jnp.swapaxes(slot2, 1, 2).reshape(t // TM_COMBINE, 1, 2 * TM_COMBINE)
    tiles = (tile_expert.astype(I32), tile_group.astype(I32), tile_next.astype(I32), n_used)
    return counts, start.astype(I32), tiles, slot_flat3, slots3


def _rope_freq_row():
    inv_freq = ROPE_THETA ** (-(jnp.arange(0, ROT_DIM, 2, dtype=F32) / ROT_DIM))
    return jnp.concatenate(
        [inv_freq, inv_freq, jnp.zeros((HEAD_DIM - ROT_DIM,), F32)]).reshape(1, HEAD_DIM)


def _router_operands(w_rg, b_rg, w_re, b_re):
    depth = w_rg.shape[0]
    pad = ROUTER_LANES - N_EXPERTS - N_GROUPS
    w = jnp.concatenate([w_re, w_rg, jnp.zeros((depth, D_MODEL, pad), F32)], axis=2)
    b = jnp.concatenate([b_re, b_rg, jnp.zeros((depth, pad), F32)], axis=1)
    hi = w.astype(BF16)
    lo = (w - hi.astype(F32)).astype(BF16)
    return hi, lo, b


def kernel(x, positions, norm_mix, w_in, w_pool, pool_scale, q_norm, k_norm, sink, branch_gain_pool,
           branch_gain_attn, w_out, norm_ffn, w_router_group, b_router_group, w_router_expert,
           b_router_expert, w_gate, w_up, w_down):
    b, s, d = x.shape
    t = b * s
    depth = norm_mix.shape[0]
    x2 = x.reshape(t, d)
    pos2 = positions.reshape(t, 1)
    freq = _rope_freq_row()
    w_in_bf = w_in.astype(BF16)
    w_out_bf = w_out.astype(BF16)
    w_pool_bf = w_pool.astype(BF16)
    r_hi, r_lo, r_b = _router_operands(w_router_group, b_router_group, w_router_expert,
                                       b_router_expert)
    wg = w_gate.reshape(depth * N_EXPERTS, D_MODEL, D_EXPERT)
    wu = w_up.reshape(depth * N_EXPERTS, D_MODEL, D_EXPERT)
    wd = w_down.reshape(depth * N_EXPERTS, D_EXPERT, D_MODEL)
    n_tiles = (2 * t) // TM_MOE + N_EXPERTS
    rows = lambda v: v.reshape(depth, 1, v.shape[-1])
    for l in range(depth):
        u, qkv = _in_proj(l, x2, pos2, rows(norm_mix), w_in_bf, rows(q_norm), rows(k_norm), freq)
        mixed = _mixer(l, u.reshape(b, s, POOL_WIDTH), qkv.reshape(b, s, QKV_WIDTH), sink, w_pool_bf,
                       rows(pool_scale), rows(branch_gain_pool), rows(branch_gain_attn))
        x1, h, ri, rw, counts_row = _out_proj(l, mixed.reshape(t, D_MODEL), x2, w_out_bf,
                                              rows(norm_ffn), r_hi, r_lo, rows(r_b))
        counts, start, tiles, slot_flat3, slots3 = _slot_tables(l, ri, counts_row, n_tiles)
        src = _plan(counts, start, slot_flat3, n_tiles * TM_MOE)
        y = _moe_ffn(h, wg, wu, wd, *tiles, src.reshape(n_tiles, 1, TM_MOE))
        x2 = _combine(x1, rw, y, slots3)
    return x2.reshape(b, s, d)
```

```python
import functools

import jax
import jax.numpy as jnp
from jax import lax
from jax.experimental import pallas as pl
from jax.experimental.pallas import tpu as pltpu

F32 = jnp.float32
BF16 = jnp.bfloat16
I32 = jnp.int32

D_MODEL = 2048
POOL_WIDTH = 1024
POOL_WINDOWS = (2, 4, 8, 16)
POOL_GROUP = 256
HEAD_DIM = 128
N_Q_HEADS = 8
N_KV_HEADS = 2
Q_PER_KV = 4
ATTN_WIDTH = 1024
KV_WIDTH = 256
QKV_WIDTH = ATTN_WIDTH + 2 * KV_WIDTH
IN_WIDTH = POOL_WIDTH + QKV_WIDTH
WINDOW = 128
BLOCK = 128
ROPE_THETA = 500000.0
ROT_DIM = 32
ROT_HALF = ROT_DIM // 2
N_GROUPS = 4
EXPERTS_PER_GROUP = 8
N_EXPERTS = 32
D_EXPERT = 512
EPS = 1e-6

LANES = 128
POOL_HALO = 8
ROUTER_LANES = LANES
TM_PROJ = 256
TM_IN = 512
SUB_ROWS = 256
MIX_BLOCKS = 2
TM_MOE = 256
TM_COMBINE = 256
PLAN_CHUNK = 8192
ISSUE_UNROLL = 16
WEIGHT_DMA_PRIORITY = 1
VMEM_LIMIT = 56 * 1024 * 1024


def _rms(x, gain):
    return x * lax.rsqrt(jnp.mean(x * x, axis=-1, keepdims=True) + EPS) * gain


def _in_proj_kernel(x_ref, pos_ref, g_ref, w_ref, qn_ref, kn_ref, freq_ref, u_ref, qkv_ref):
    for sub in range(TM_IN // SUB_ROWS):
        rows = slice(sub * SUB_ROWS, (sub + 1) * SUB_ROWS)
        h = _rms(x_ref[rows], g_ref[0]).astype(BF16)
        z = jnp.dot(h, w_ref[0], preferred_element_type=F32)
        u_ref[rows] = z[:, :POOL_WIDTH]

        ang = pos_ref[rows].astype(F32) * freq_ref[...]
        cos = jnp.cos(ang)
        sin = jnp.sin(ang)
        lane = lax.broadcasted_iota(I32, ang.shape, 1)
        sin_hi = jnp.where(lane >= ROT_HALF, sin, 0.0)
        sin_lo = jnp.where(lane < ROT_HALF, -sin, 0.0)

        def norm_rope(t, gain):
            y = _rms(t, gain)
            return (y * cos + pltpu.roll(y, ROT_HALF, 1) * sin_hi
                    + pltpu.roll(y, HEAD_DIM - ROT_HALF, 1) * sin_lo)

        for hd in range(N_Q_HEADS + N_KV_HEADS):
            gain = qn_ref[0] if hd < N_Q_HEADS else kn_ref[0]
            src = POOL_WIDTH + hd * HEAD_DIM
            dst = hd * HEAD_DIM
            qkv_ref[rows, dst:dst + HEAD_DIM] = norm_rope(z[:, src:src + HEAD_DIM], gain).astype(BF16)
        v0 = POOL_WIDTH + ATTN_WIDTH + KV_WIDTH
        qkv_ref[rows, ATTN_WIDTH + KV_WIDTH:] = z[:, v0:].astype(BF16)


def _in_proj(l, x2, pos2, gain, w_bf, qn, kn, freq):
    t = x2.shape[0]
    row = lambda i: (i, 0)
    layer = lambda i: (l, 0, 0)
    return pl.pallas_call(
        _in_proj_kernel,
        grid=(t // TM_IN,),
        in_specs=[
            pl.BlockSpec((TM_IN, D_MODEL), row),
            pl.BlockSpec((TM_IN, 1), row),
            pl.BlockSpec((1, 1, D_MODEL), layer),
            pl.BlockSpec((1, D_MODEL, IN_WIDTH), layer),
            pl.BlockSpec((1, 1, HEAD_DIM), layer),
            pl.BlockSpec((1, 1, HEAD_DIM), layer),
            pl.BlockSpec((1, HEAD_DIM), lambda i: (0, 0)),
        ],
        out_specs=[
            pl.BlockSpec((TM_IN, POOL_WIDTH), row),
            pl.BlockSpec((TM_IN, QKV_WIDTH), row),
        ],
        out_shape=[
            jax.ShapeDtypeStruct((t, POOL_WIDTH), F32),
            jax.ShapeDtypeStruct((t, QKV_WIDTH), BF16),
        ],
        compiler_params=pltpu.CompilerParams(
            dimension_semantics=("arbitrary",), vmem_limit_bytes=VMEM_LIMIT),
        name="in_proj",
    )(x2, pos2, gain, w_bf, qn, kn, freq)


def _mixer_kernel(sink_ref, u_ref, up_ref, un_ref, q_ref, kp_ref, kc_ref, kn_ref,
                  vp_ref, vc_ref, vn_ref, wp_ref, ps_ref, gp_ref, ga_ref, o_ref, *, seq, layer):
    n = pl.program_id(1)
    n_steps = pl.num_programs(1)
    step_rows = MIX_BLOCKS * BLOCK
    t0 = n * step_rows

    prev = jnp.where(n > 0, up_ref[0], 0.0)
    nxt = jnp.where(n < n_steps - 1, un_ref[0], 0.0)
    cur = u_ref[0]
    ext = jnp.concatenate([prev, cur, nxt], axis=0)
    rows = ext.shape[0]
    tpos = t0 + lax.broadcasted_iota(I32, (step_rows, POOL_GROUP), 0)
    pooled = []
    for gi, w in enumerate(POOL_WINDOWS):
        sl = slice(gi * POOL_GROUP, (gi + 1) * POOL_GROUP)
        acc = ext[:, sl]
        span = 1
        while span < w:
            acc = acc + pltpu.roll(acc, span, 0)
            span *= 2
        lead = w // 2 - 1
        if lead:
            acc = pltpu.roll(acc, rows - lead, 0)
        wsum = acc[POOL_HALO:POOL_HALO + step_rows]
        lo = jnp.clip(tpos - w // 2, 0, seq)
        hi = jnp.clip(tpos + w // 2, 0, seq)
        y = wsum / (hi - lo).astype(F32) - cur[:, sl]
        y = jnp.dot(y.astype(BF16), wp_ref[0, gi], preferred_element_type=F32)
        pooled.append(y)
    pool = jnp.concatenate(pooled, axis=1) * ps_ref[0]
    o_ref[0, :, :POOL_WIDTH] = _rms(pool, gp_ref[0]).astype(BF16)

    r = lax.broadcasted_iota(I32, (Q_PER_KV * BLOCK, 3 * BLOCK), 0) % BLOCK
    c = lax.broadcasted_iota(I32, (Q_PER_KV * BLOCK, 3 * BLOCK), 1)
    band = (c >= r) & (c <= r + 2 * WINDOW)
    kcat, vcat = [], []
    for kh in range(N_KV_HEADS):
        ks = slice(kh * HEAD_DIM, (kh + 1) * HEAD_DIM)
        kcat.append(jnp.concatenate([kp_ref[0, :, ks], kc_ref[0, :, ks], kn_ref[0, :, ks]], axis=0))
        vcat.append(jnp.concatenate([vp_ref[0, :, ks], vc_ref[0, :, ks], vn_ref[0, :, ks]], axis=0))
    for j in range(MIX_BLOCKS):
        kj = t0 + (j - 1) * BLOCK + c
        mask = band & (kj >= 0) & (kj < seq)
        qrows = slice(j * BLOCK, (j + 1) * BLOCK)
        krows = slice(j * BLOCK, (j + 3) * BLOCK)
        heads = []
        for kh in range(N_KV_HEADS):
            q = jnp.concatenate(
                [q_ref[0, qrows, (kh * Q_PER_KV + g) * HEAD_DIM:(kh * Q_PER_KV + g + 1) * HEAD_DIM]
                 for g in range(Q_PER_KV)], axis=0)
            s = lax.dot_general(q, kcat[kh][krows], (((1,), (1,)), ((), ())),
                                preferred_element_type=F32)
            s = jnp.where(mask, s * (HEAD_DIM ** -0.5), -jnp.inf)
            ps, dens = [], []
            for g in range(Q_PER_KV):
                sg = s[g * BLOCK:(g + 1) * BLOCK]
                sink = sink_ref[layer, kh * Q_PER_KV + g]
                m = jnp.maximum(jnp.max(sg, axis=-1, keepdims=True), sink)
                pg = jnp.exp(sg - m)
                dens.append(jnp.sum(pg, axis=-1, keepdims=True) + jnp.exp(sink - m))
                ps.append(pg.astype(BF16))
            o = jnp.dot(jnp.concatenate(ps, axis=0), vcat[kh][krows], preferred_element_type=F32)
            heads.extend(o[g * BLOCK:(g + 1) * BLOCK] / dens[g] for g in range(Q_PER_KV))
        attn = jnp.concatenate(heads, axis=1)
        o_ref[0, qrows, POOL_WIDTH:] = _rms(attn, ga_ref[0]).astype(BF16)


def _mixer(l, u3, qkv3, sink, wp_bf, pscale, gpool, gattn):
    b, s, _ = u3.shape
    nb = s // BLOCK
    step_rows = MIX_BLOCKS * BLOCK
    n_steps = s // step_rows
    halo_per_step = step_rows // POOL_HALO
    n_halo = s // POOL_HALO
    kcol = ATTN_WIDTH // KV_WIDTH
    vcol = kcol + 1
    layer = lambda bi, n: (l, 0, 0)
    prev_blk = lambda n: jnp.maximum(n * MIX_BLOCKS - 1, 0)
    next_blk = lambda n: jnp.minimum((n + 1) * MIX_BLOCKS, nb - 1)
    edge_spec = lambda blk, col: pl.BlockSpec((1, BLOCK, KV_WIDTH), lambda bi, n: (bi, blk(n), col))
    body_spec = lambda col: pl.BlockSpec((1, step_rows, KV_WIDTH), lambda bi, n: (bi, n, col))
    return pl.pallas_call(
        functools.partial(_mixer_kernel, seq=s, layer=l),
        grid=(b, n_steps),
        in_specs=[
            pl.BlockSpec(memory_space=pltpu.SMEM),
            pl.BlockSpec((1, step_rows, POOL_WIDTH), lambda bi, n: (bi, n, 0)),
            pl.BlockSpec((1, POOL_HALO, POOL_WIDTH),
                         lambda bi, n: (bi, jnp.maximum(n * halo_per_step - 1, 0), 0)),
            pl.BlockSpec((1, POOL_HALO, POOL_WIDTH),
                         lambda bi, n: (bi, jnp.minimum((n + 1) * halo_per_step, n_halo - 1), 0)),
            pl.BlockSpec((1, step_rows, ATTN_WIDTH), lambda bi, n: (bi, n, 0)),
            edge_spec(prev_blk, kcol), body_spec(kcol), edge_spec(next_blk, kcol),
            edge_spec(prev_blk, vcol), body_spec(vcol), edge_spec(next_blk, vcol),
            pl.BlockSpec((1, len(POOL_WINDOWS), POOL_GROUP, POOL_GROUP), lambda bi, n: (l, 0, 0, 0)),
            pl.BlockSpec((1, 1, POOL_WIDTH), layer),
            pl.BlockSpec((1, 1, POOL_WIDTH), layer),
            pl.BlockSpec((1, 1, ATTN_WIDTH), layer),
        ],
        out_specs=pl.BlockSpec((1, step_rows, POOL_WIDTH + ATTN_WIDTH), lambda bi, n: (bi, n, 0)),
        out_shape=jax.ShapeDtypeStruct((b, s, POOL_WIDTH + ATTN_WIDTH), BF16),
        compiler_params=pltpu.CompilerParams(
            dimension_semantics=("arbitrary", "arbitrary"), vmem_limit_bytes=VMEM_LIMIT),
        name="mixer",
    )(sink, u3, u3, u3, qkv3, qkv3, qkv3, qkv3, qkv3, qkv3, qkv3, wp_bf, pscale, gpool, gattn)


def _split_dot(a, b_hi, b_lo):
    a_hi = a.astype(BF16)
    a_lo = (a - a_hi.astype(F32)).astype(BF16)
    return (jnp.dot(a_hi, b_hi, preferred_element_type=F32)
            + jnp.dot(a_lo, b_hi, preferred_element_type=F32)
            + jnp.dot(a_hi, b_lo, preferred_element_type=F32))


def _out_proj_kernel(m_ref, x_ref, w_ref, g_ref, rh_ref, rl_ref, rb_ref,
                     x1_ref, h_ref, ri_ref, rw_ref, cnt_ref, run_ref):
    @pl.when(pl.program_id(0) == 0)
    def _():
        run_ref[...] = jnp.zeros_like(run_ref)

    x1 = x_ref[...] + jnp.dot(m_ref[...], w_ref[0], preferred_element_type=F32)
    x1_ref[...] = x1
    h = _rms(x1, g_ref[0])
    h_ref[...] = h
    logits = _split_dot(h, rh_ref[0], rl_ref[0]) + rb_ref[0]
    lane = lax.broadcasted_iota(I32, logits.shape, 1)
    far = jnp.int32(ROUTER_LANES)
    neg = -jnp.inf

    def top(vals):
        best = jnp.max(vals, axis=-1, keepdims=True)
        idx = jnp.min(jnp.where(vals == best, lane, far), axis=-1, keepdims=True)
        return best, idx

    gl = jnp.where((lane >= N_EXPERTS) & (lane < N_EXPERTS + N_GROUPS), logits, neg)
    gmax, gidx = top(gl)
    g_w = 1.0 / jnp.sum(jnp.exp(gl - gmax), axis=-1, keepdims=True)
    in_group = (lane < N_EXPERTS) & ((lane >> 3) == gidx - N_EXPERTS)
    el = jnp.where(in_group, logits, neg)
    m1, i1 = top(el)
    m2, i2 = top(jnp.where(lane == i1, neg, el))
    esum = jnp.sum(jnp.exp(el - m1), axis=-1, keepdims=True)
    p1 = 1.0 / esum
    p2 = jnp.exp(m2 - m1) / esum
    w1 = g_w * (p1 / (p1 + p2))
    w2 = g_w * (p2 / (p1 + p2))

    pick1 = lane == i1
    pick2 = lane == i2
    picks = (pick1 | pick2).astype(BF16)
    tm = logits.shape[0]
    lower = (lax.broadcasted_iota(I32, (tm, tm), 1) < lax.broadcasted_iota(I32, (tm, tm), 0)).astype(BF16)
    before = jnp.dot(lower, picks, preferred_element_type=F32) + run_ref[...]
    r1 = jnp.sum(jnp.where(pick1, before, 0.0), axis=-1, keepdims=True).astype(I32)
    r2 = jnp.sum(jnp.where(pick2, before, 0.0), axis=-1, keepdims=True).astype(I32)
    total = run_ref[...] + jnp.sum(picks.astype(F32), axis=0, keepdims=True)
    run_ref[...] = total
    cnt_ref[...] = total.astype(I32)

    ri_ref[...] = jnp.where(lane == 0, i1, jnp.where(lane == 1, i2,
                            jnp.where(lane == 2, r1, jnp.where(lane == 3, r2, 0))))
    rw_ref[...] = jnp.where(lane == 0, w1, jnp.where(lane == 1, w2, 0.0))


def _out_proj(l, mixed2, x2, w_bf, gain, r_hi, r_lo, r_b):
    t = x2.shape[0]
    row = lambda i: (i, 0)
    layer = lambda i: (l, 0, 0)
    return pl.pallas_call(
        _out_proj_kernel,
        grid=(t // TM_PROJ,),
        in_specs=[
            pl.BlockSpec((TM_PROJ, D_MODEL), row),
            pl.BlockSpec((TM_PROJ, D_MODEL), row),
            pl.BlockSpec((1, D_MODEL, D_MODEL), layer),
            pl.BlockSpec((1, 1, D_MODEL), layer),
            pl.BlockSpec((1, D_MODEL, ROUTER_LANES), layer),
            pl.BlockSpec((1, D_MODEL, ROUTER_LANES), layer),
            pl.BlockSpec((1, 1, ROUTER_LANES), layer),
        ],
        out_specs=[
            pl.BlockSpec((TM_PROJ, D_MODEL), row),
            pl.BlockSpec((TM_PROJ, D_MODEL), row),
            pl.BlockSpec((TM_PROJ, ROUTER_LANES), row),
            pl.BlockSpec((TM_PROJ, ROUTER_LANES), row),
            pl.BlockSpec((1, ROUTER_LANES), lambda i: (0, 0)),
        ],
        out_shape=[
            jax.ShapeDtypeStruct((t, D_MODEL), F32),
            jax.ShapeDtypeStruct((t, D_MODEL), F32),
            jax.ShapeDtypeStruct((t, ROUTER_LANES), I32),
            jax.ShapeDtypeStruct((t, ROUTER_LANES), F32),
            jax.ShapeDtypeStruct((1, ROUTER_LANES), I32),
        ],
        scratch_shapes=[pltpu.VMEM((1, ROUTER_LANES), F32)],
        compiler_params=pltpu.CompilerParams(
            dimension_semantics=("arbitrary",), vmem_limit_bytes=VMEM_LIMIT),
        name="out_proj",
    )(mixed2, x2, w_bf, gain, r_hi, r_lo, r_b)


def _plan_kernel(cnt_ref, start_ref, slot_ref, src_ref):
    i = pl.program_id(0)
    base = i * PLAN_CHUNK

    def body(j, carry):
        first = j * ISSUE_UNROLL
        slots = [slot_ref[0, 0, first + k] for k in range(ISSUE_UNROLL)]
        for k in range(ISSUE_UNROLL):
            src_ref[slots[k]] = (base + first + k) >> 1
        return carry
    lax.fori_loop(0, PLAN_CHUNK // ISSUE_UNROLL, body, 0)

    @pl.when(i == pl.num_programs(0) - 1)
    def _():
        n_slots = src_ref.shape[0]

        def per_expert(e, carry):
            lo = start_ref[e] + cnt_ref[e]
            hi = jnp.where(e == N_EXPERTS - 1, n_slots, start_ref[jnp.minimum(e + 1, N_EXPERTS - 1)])

            def fill(s, c2):
                src_ref[s] = 0
                return c2
            lax.fori_loop(lo, hi, fill, 0)
            return carry
        lax.fori_loop(0, N_EXPERTS, per_expert, 0)


def _plan(counts, start, slot_flat3, n_slots):
    n_chunks = slot_flat3.shape[0]
    return pl.pallas_call(
        _plan_kernel,
        grid_spec=pltpu.PrefetchScalarGridSpec(
            num_scalar_prefetch=2,
            grid=(n_chunks,),
            in_specs=[pl.BlockSpec((1, 1, PLAN_CHUNK), lambda i, c, s: (i, 0, 0),
                                   memory_space=pltpu.SMEM)],
            out_specs=pl.BlockSpec(memory_space=pltpu.SMEM),
        ),
        out_shape=jax.ShapeDtypeStruct((n_slots,), I32),
        compiler_params=pltpu.CompilerParams(dimension_semantics=("arbitrary",)),
        name="plan",
    )(counts, start, slot_flat3)


def _row_copy(src_hbm, idx_ref, dst, sem, r):
    return pltpu.make_async_copy(src_hbm.at[pl.ds(idx_ref[0, 0, r], 1)], dst.at[pl.ds(r, 1)], sem)


def _start_rows(src_hbm, idx_ref, dst, sem, n_rows):
    def body(j, carry):
        for k in range(ISSUE_UNROLL):
            _row_copy(src_hbm, idx_ref, dst, sem, j * ISSUE_UNROLL + k).start()
        return carry
    lax.fori_loop(0, n_rows // ISSUE_UNROLL, body, 0)


def _wait_rows(src_hbm, dst, sem, n_rows):
    pltpu.make_async_copy(src_hbm.at[pl.ds(0, n_rows)], dst, sem).wait()


def _moe_kernel(te_ref, grp_ref, nxt_ref, nu_ref, first_ref, ahead_ref, h_hbm, wg_hbm, wu_hbm, wd_hbm,
                y_ref, buf, sem, wg_st, wu_st, wd_st, wsem, wg_bf, wu_bf, wd_bf):
    i = pl.program_id(0)
    n_used = nu_ref[0]

    def weight_copies(e, par):
        return (pltpu.make_async_copy(wg_hbm.at[e], wg_st.at[par], wsem.at[par]),
                pltpu.make_async_copy(wu_hbm.at[e], wu_st.at[par], wsem.at[par]),
                pltpu.make_async_copy(wd_hbm.at[e], wd_st.at[par], wsem.at[par]))

    @pl.when(i == 0)
    def _():
        _start_rows(h_hbm, first_ref, buf.at[0], sem.at[0], TM_MOE)
        for c in weight_copies(te_ref[0], 0):
            c.start(priority=WEIGHT_DMA_PRIORITY)

    @pl.when(i < n_used)
    def _():
        slot = i % 2
        other = 1 - slot
        _wait_rows(h_hbm, buf.at[slot], sem.at[slot], TM_MOE)

        @pl.when((i == 0) | (te_ref[i] != te_ref[jnp.maximum(i - 1, 0)]))
        def _():
            par = grp_ref[i] % 2
            for c in weight_copies(te_ref[i], par):
                c.wait()

            @pl.when(nxt_ref[i] >= 0)
            def _():
                for c in weight_copies(nxt_ref[i], 1 - par):
                    c.start(priority=WEIGHT_DMA_PRIORITY)

            wg_bf[...] = wg_st[par].astype(BF16)
            wu_bf[...] = wu_st[par].astype(BF16)
            wd_bf[...] = wd_st[par].astype(BF16)

        for r in range(TM_MOE):
            _row_copy(h_hbm, ahead_ref, buf.at[other], sem.at[other], r).start()
        xb = buf[slot].astype(BF16)
        g = jnp.dot(xb, wg_bf[...], preferred_element_type=F32)
        u = jnp.dot(xb, wu_bf[...], preferred_element_type=F32)
        a = (g * jax.nn.sigmoid(g) * u).astype(BF16)
        y_ref[...] = jnp.dot(a, wd_bf[...], preferred_element_type=F32)

        @pl.when(i == n_used - 1)
        def _():
            _wait_rows(h_hbm, buf.at[other], sem.at[other], TM_MOE)

    @pl.when(i >= n_used)
    def _():
        y_ref[...] = jnp.zeros_like(y_ref)


def _moe_ffn(h2, w_gate, w_up, w_down, tile_expert, tile_group, tile_next, n_used, src3):
    n_tiles = src3.shape[0]
    smem_tile = lambda index_map: pl.BlockSpec((1, 1, TM_MOE), index_map, memory_space=pltpu.SMEM)
    hbm = pl.BlockSpec(memory_space=pl.ANY)
    return pl.pallas_call(
        _moe_kernel,
        grid_spec=pltpu.PrefetchScalarGridSpec(
            num_scalar_prefetch=4,
            grid=(n_tiles,),
            in_specs=[
                smem_tile(lambda i, te, gr, nx, nu: (0, 0, 0)),
                smem_tile(lambda i, te, gr, nx, nu:
                          (jnp.maximum(jnp.minimum(i + 1, nu[0] - 1), 0), 0, 0)),
                hbm, hbm, hbm, hbm,
            ],
            out_specs=pl.BlockSpec((TM_MOE, D_MODEL), lambda i, te, gr, nx, nu: (i, 0)),
            scratch_shapes=[
                pltpu.VMEM((2, TM_MOE, D_MODEL), F32),
                pltpu.SemaphoreType.DMA((2,)),
                pltpu.VMEM((2, D_MODEL, D_EXPERT), F32),
                pltpu.VMEM((2, D_MODEL, D_EXPERT), F32),
                pltpu.VMEM((2, D_EXPERT, D_MODEL), F32),
                pltpu.SemaphoreType.DMA((2,)),
                pltpu.VMEM((D_MODEL, D_EXPERT), BF16),
                pltpu.VMEM((D_MODEL, D_EXPERT), BF16),
                pltpu.VMEM((D_EXPERT, D_MODEL), BF16),
            ],
        ),
        out_shape=jax.ShapeDtypeStruct((n_tiles * TM_MOE, D_MODEL), F32),
        compiler_params=pltpu.CompilerParams(
            dimension_semantics=("arbitrary",), vmem_limit_bytes=VMEM_LIMIT),
        name="moe_ffn",
    )(tile_expert, tile_group, tile_next, n_used, src3, src3, h2, w_gate, w_up, w_down)


def _combine_kernel(first_ref, ahead_ref, x_ref, w_ref, y_hbm, o_ref, buf, sem):
    i = pl.program_id(0)
    n = pl.num_programs(0)
    rows = 2 * TM_COMBINE

    @pl.when(i == 0)
    def _():
        _start_rows(y_hbm, first_ref, buf.at[0], sem.at[0], rows)

    @pl.when(i + 1 < n)
    def _():
        nxt = (i + 1) % 2
        for r in range(rows):
            _row_copy(y_hbm, ahead_ref, buf.at[nxt], sem.at[nxt], r).start()

    slot = i % 2
    _wait_rows(y_hbm, buf.at[slot], sem.at[slot], rows)
    w = w_ref[...]
    o_ref[...] = (x_ref[...] + w[:, 0:1] * buf[slot, :TM_COMBINE]
                  + w[:, 1:2] * buf[slot, TM_COMBINE:])


def _combine(x1, rw, y, slots3):
    t = x1.shape[0]
    n = t // TM_COMBINE
    row = lambda i: (i, 0)
    smem_tile = lambda index_map: pl.BlockSpec((1, 1, 2 * TM_COMBINE), index_map,
                                               memory_space=pltpu.SMEM)
    return pl.pallas_call(
        _combine_kernel,
        grid=(n,),
        in_specs=[
            smem_tile(lambda i: (0, 0, 0)),
            smem_tile(lambda i: (jnp.minimum(i + 1, n - 1), 0, 0)),
            pl.BlockSpec((TM_COMBINE, D_MODEL), row),
            pl.BlockSpec((TM_COMBINE, ROUTER_LANES), row),
            pl.BlockSpec(memory_space=pl.ANY),
        ],
        out_specs=pl.BlockSpec((TM_COMBINE, D_MODEL), row),
        out_shape=jax.ShapeDtypeStruct((t, D_MODEL), F32),
        scratch_shapes=[
            pltpu.VMEM((2, 2 * TM_COMBINE, D_MODEL), F32),
            pltpu.SemaphoreType.DMA((2,)),
        ],
        compiler_params=pltpu.CompilerParams(
            dimension_semantics=("arbitrary",), vmem_limit_bytes=VMEM_LIMIT),
        name="combine",
    )(slots3, slots3, x1, rw, y)


def _slot_tables(l, ri, counts_row, n_tiles):
    t = ri.shape[0]
    experts = jnp.arange(N_EXPERTS, dtype=I32)
    counts = counts_row[0, :N_EXPERTS]
    tiles_per = (counts + TM_MOE - 1) // TM_MOE
    tile_end = jnp.cumsum(tiles_per)
    start = (tile_end - tiles_per) * TM_MOE
    n_used = tile_end[-1:].astype(I32)
    nonempty = counts > 0
    group_of = jnp.cumsum(nonempty.astype(I32)) - 1
    later = (experts[None, :] > experts[:, None]) & nonempty[None, :]
    next_of = jnp.min(jnp.where(later, experts[None, :], N_EXPERTS), axis=1)
    next_of = jnp.where(next_of < N_EXPERTS, next_of + l * N_EXPERTS, -1)
    tile_idx = jnp.arange(n_tiles, dtype=I32)
    tile_e = jnp.minimum(jnp.sum((tile_end[None, :] <= tile_idx[:, None]).astype(I32), axis=1),
                         N_EXPERTS - 1)
    pick = tile_e[:, None] == experts[None, :]
    tile_group = jnp.sum(jnp.where(pick, group_of[None, :], 0), axis=1)
    tile_next = jnp.sum(jnp.where(pick, next_of[None, :], 0), axis=1)
    tile_expert = tile_e + l * N_EXPERTS
    eid = ri[:, 0:2]
    slot = ri[:, 2:4] + jnp.sum(jnp.where(eid[..., None] == experts, start, 0), axis=-1)
    slot_flat3 = slot.reshape((2 * t) // PLAN_CHUNK, 1, PLAN_CHUNK)
    slot2 = slot.reshape(t // TM_COMBINE, TM_COMBINE, 2)
    slots3 = jnp.swapaxes(slot2, 1, 2).reshape(t // TM_COMBINE, 1, 2 * TM_COMBINE)
    tiles = (tile_expert.astype(I32), tile_group.astype(I32), tile_next.astype(I32), n_used)
    return counts, start.astype(I32), tiles, slot_flat3, slots3


def _rope_freq_row():
    inv_freq = ROPE_THETA ** (-(jnp.arange(0, ROT_DIM, 2, dtype=F32) / ROT_DIM))
    return jnp.concatenate(
        [inv_freq, inv_freq, jnp.zeros((HEAD_DIM - ROT_DIM,), F32)]).reshape(1, HEAD_DIM)


def _router_operands(w_rg, b_rg, w_re, b_re):
    depth = w_rg.shape[0]
    pad = ROUTER_LANES - N_EXPERTS - N_GROUPS
    w = jnp.concatenate([w_re, w_rg, jnp.zeros((depth, D_MODEL, pad), F32)], axis=2)
    b = jnp.concatenate([b_re, b_rg, jnp.zeros((depth, pad), F32)], axis=1)
    hi = w.astype(BF16)
    lo = (w - hi.astype(F32)).astype(BF16)
    return hi, lo, b


def kernel(x, positions, norm_mix, w_in, w_pool, pool_scale, q_norm, k_norm, sink, branch_gain_pool,
           branch_gain_attn, w_out, norm_ffn, w_router_group, b_router_group, w_router_expert,
           b_router_expert, w_gate, w_up, w_down):
    b, s, d = x.shape
    t = b * s
    depth = norm_mix.shape[0]
    x2 = x.reshape(t, d)
    pos2 = positions.reshape(t, 1)
    freq = _rope_freq_row()
    w_in_bf = w_in.astype(BF16)
    w_out_bf = w_out.astype(BF16)
    w_pool_bf = w_pool.astype(BF16)
    r_hi, r_lo, r_b = _router_operands(w_router_group, b_router_group, w_router_expert,
                                       b_router_expert)
    wg = w_gate.reshape(depth * N_EXPERTS, D_MODEL, D_EXPERT)
    wu = w_up.reshape(depth * N_EXPERTS, D_MODEL, D_EXPERT)
    wd = w_down.reshape(depth * N_EXPERTS, D_EXPERT, D_MODEL)
    n_tiles = (2 * t) // TM_MOE + N_EXPERTS
    rows = lambda v: v.reshape(depth, 1, v.shape[-1])
    for l in range(depth):
        u, qkv = _in_proj(l, x2, pos2, rows(norm_mix), w_in_bf, rows(q_norm), rows(k_norm), freq)
        mixed = _mixer(l, u.reshape(b, s, POOL_WIDTH), qkv.reshape(b, s, QKV_WIDTH), sink, w_pool_bf,
                       rows(pool_scale), rows(branch_gain_pool), rows(branch_gain_attn))
        x1, h, ri, rw, counts_row = _out_proj(l, mixed.reshape(t, D_MODEL), x2, w_out_bf,
                                              rows(norm_ffn), r_hi, r_lo, rows(r_b))
        counts, start, tiles, slot_flat3, slots3 = _slot_tables(l, ri, counts_row, n_tiles)
        src = _plan(counts, start, slot_flat3, n_tiles * TM_MOE)
        y = _moe_ffn(h, wg, wu, wd, *tiles, src.reshape(n_tiles, 1, TM_MOE))
        x2 = _combine(x1, rw, y, slots3)
    return x2.reshape(b, s, d)
```

```python
import functools

import jax
import jax.numpy as jnp
from jax import lax
from jax.experimental import pallas as pl
from jax.experimental.pallas import tpu as pltpu

F32 = jnp.float32
BF16 = jnp.bfloat16
I32 = jnp.int32

D_MODEL = 2048
POOL_WIDTH = 1024
POOL_WINDOWS = (2, 4, 8, 16)
POOL_GROUP = 256
HEAD_DIM = 128
N_Q_HEADS = 8
N_KV_HEADS = 2
Q_PER_KV = 4
ATTN_WIDTH = 1024
KV_WIDTH = 256
QKV_WIDTH = ATTN_WIDTH + 2 * KV_WIDTH
IN_WIDTH = POOL_WIDTH + QKV_WIDTH
WINDOW = 128
BLOCK = 128
ROPE_THETA = 500000.0
ROT_DIM = 32
ROT_HALF = ROT_DIM // 2
N_GROUPS = 4
EXPERTS_PER_GROUP = 8
N_EXPERTS = 32
D_EXPERT = 512
EPS = 1e-6

LANES = 128
POOL_HALO = 8
ROUTER_LANES = LANES
TM_PROJ = 256
TM_IN = 512
SUB_ROWS = 256
MIX_BLOCKS = 2
TM_MOE = 256
TM_COMBINE = 256
PLAN_CHUNK = 8192
ISSUE_UNROLL = 16
WEIGHT_DMA_PRIORITY = 1
VMEM_LIMIT = 56 * 1024 * 1024


def _rms(x, gain):
    return x * lax.rsqrt(jnp.mean(x * x, axis=-1, keepdims=True) + EPS) * gain


def _in_proj_kernel(x_ref, pos_ref, g_ref, w_ref, qn_ref, kn_ref, freq_ref, u_ref, qkv_ref):
    for sub in range(TM_IN // SUB_ROWS):
        rows = slice(sub * SUB_ROWS, (sub + 1) * SUB_ROWS)
        h = _rms(x_ref[rows], g_ref[0]).astype(BF16)
        z = jnp.dot(h, w_ref[0], preferred_element_type=F32)
        u_ref[rows] = z[:, :POOL_WIDTH]

        ang = pos_ref[rows].astype(F32) * freq_ref[...]
        cos = jnp.cos(ang)
        sin = jnp.sin(ang)
        lane = lax.broadcasted_iota(I32, ang.shape, 1)
        sin_hi = jnp.where(lane >= ROT_HALF, sin, 0.0)
        sin_lo = jnp.where(lane < ROT_HALF, -sin, 0.0)

        def norm_rope(t, gain):
            y = _rms(t, gain)
            return (y * cos + pltpu.roll(y, ROT_HALF, 1) * sin_hi
                    + pltpu.roll(y, HEAD_DIM - ROT_HALF, 1) * sin_lo)

        for hd in range(N_Q_HEADS + N_KV_HEADS):
            gain = qn_ref[0] if hd < N_Q_HEADS else kn_ref[0]
            src = POOL_WIDTH + hd * HEAD_DIM
            dst = hd * HEAD_DIM
            qkv_ref[rows, dst:dst + HEAD_DIM] = norm_rope(z[:, src:src + HEAD_DIM], gain).astype(BF16)
        v0 = POOL_WIDTH + ATTN_WIDTH + KV_WIDTH
        qkv_ref[rows, ATTN_WIDTH + KV_WIDTH:] = z[:, v0:].astype(BF16)


def _in_proj(l, x2, pos2, gain, w_bf, qn, kn, freq):
    t = x2.shape[0]
    row = lambda i: (i, 0)
    layer = lambda i: (l, 0, 0)
    return pl.pallas_call(
        _in_proj_kernel,
        grid=(t // TM_IN,),
        in_specs=[
            pl.BlockSpec((TM_IN, D_MODEL), row),
            pl.BlockSpec((TM_IN, 1), row),
            pl.BlockSpec((1, 1, D_MODEL), layer),
            pl.BlockSpec((1, D_MODEL, IN_WIDTH), layer),
            pl.BlockSpec((1, 1, HEAD_DIM), layer),
            pl.BlockSpec((1, 1, HEAD_DIM), layer),
            pl.BlockSpec((1, HEAD_DIM), lambda i: (0, 0)),
        ],
        out_specs=[
            pl.BlockSpec((TM_IN, POOL_WIDTH), row),
            pl.BlockSpec((TM_IN, QKV_WIDTH), row),
        ],
        out_shape=[
            jax.ShapeDtypeStruct((t, POOL_WIDTH), F32),
            jax.ShapeDtypeStruct((t, QKV_WIDTH), BF16),
        ],
        compiler_params=pltpu.CompilerParams(
            dimension_semantics=("arbitrary",), vmem_limit_bytes=VMEM_LIMIT),
        name="in_proj",
    )(x2, pos2, gain, w_bf, qn, kn, freq)


def _mixer_kernel(sink_ref, u_ref, up_ref, un_ref, q_ref, kp_ref, kc_ref, kn_ref,
                  vp_ref, vc_ref, vn_ref, wp_ref, ps_ref, gp_ref, ga_ref, o_ref, *, seq, layer):
    n = pl.program_id(1)
    n_steps = pl.num_programs(1)
    step_rows = MIX_BLOCKS * BLOCK
    t0 = n * step_rows

    prev = jnp.where(n > 0, up_ref[0], 0.0)
    nxt = jnp.where(n < n_steps - 1, un_ref[0], 0.0)
    cur = u_ref[0]
    ext = jnp.concatenate([prev, cur, nxt], axis=0)
    rows = ext.shape[0]
    tpos = t0 + lax.broadcasted_iota(I32, (step_rows, POOL_GROUP), 0)
    pooled = []
    for gi, w in enumerate(POOL_WINDOWS):
        sl = slice(gi * POOL_GROUP, (gi + 1) * POOL_GROUP)
        acc = ext[:, sl]
        span = 1
        while span < w:
            acc = acc + pltpu.roll(acc, span, 0)
            span *= 2
        lead = w // 2 - 1
        if lead:
            acc = pltpu.roll(acc, rows - lead, 0)
        wsum = acc[POOL_HALO:POOL_HALO + step_rows]
        lo = jnp.clip(tpos - w // 2, 0, seq)
        hi = jnp.clip(tpos + w // 2, 0, seq)
        y = wsum / (hi - lo).astype(F32) - cur[:, sl]
        y = jnp.dot(y.astype(BF16), wp_ref[0, gi], preferred_element_type=F32)
        pooled.append(y)
    pool = jnp.concatenate(pooled, axis=1) * ps_ref[0]
    o_ref[0, :, :POOL_WIDTH] = _rms(pool, gp_ref[0]).astype(BF16)

    r = lax.broadcasted_iota(I32, (Q_PER_KV * BLOCK, 3 * BLOCK), 0) % BLOCK
    c = lax.broadcasted_iota(I32, (Q_PER_KV * BLOCK, 3 * BLOCK), 1)
    band = (c >= r) & (c <= r + 2 * WINDOW)
    kcat, vcat = [], []
    for kh in range(N_KV_HEADS):
        ks = slice(kh * HEAD_DIM, (kh + 1) * HEAD_DIM)
        kcat.append(jnp.concatenate([kp_ref[0, :, ks], kc_ref[0, :, ks], kn_ref[0, :, ks]], axis=0))
        vcat.append(jnp.concatenate([vp_ref[0, :, ks], vc_ref[0, :, ks], vn_ref[0, :, ks]], axis=0))
    for j in range(MIX_BLOCKS):
        kj = t0 + (j - 1) * BLOCK + c
        mask = band & (kj >= 0) & (kj < seq)
        qrows = slice(j * BLOCK, (j + 1) * BLOCK)
        krows = slice(j * BLOCK, (j + 3) * BLOCK)
        heads = []
        for kh in range(N_KV_HEADS):
            q = jnp.concatenate(
                [q_ref[0, qrows, (kh * Q_PER_KV + g) * HEAD_DIM:(kh * Q_PER_KV + g + 1) * HEAD_DIM]
                 for g in range(Q_PER_KV)], axis=0)
            s = lax.dot_general(q, kcat[kh][krows], (((1,), (1,)), ((), ())),
                                preferred_element_type=F32)
            s = jnp.where(mask, s * (HEAD_DIM ** -0.5), -jnp.inf)
            ps, dens = [], []
            for g in range(Q_PER_KV):
                sg = s[g * BLOCK:(g + 1) * BLOCK]
                sink = sink_ref[layer, kh * Q_PER_KV + g]
                m = jnp.maximum(jnp.max(sg, axis=-1, keepdims=True), sink)
                pg = jnp.exp(sg - m)
                dens.append(jnp.sum(pg, axis=-1, keepdims=True) + jnp.exp(sink - m))
                ps.append(pg.astype(BF16))
            o = jnp.dot(jnp.concatenate(ps, axis=0), vcat[kh][krows], preferred_element_type=F32)
            heads.extend(o[g * BLOCK:(g + 1) * BLOCK] / dens[g] for g in range(Q_PER_KV))
        attn = jnp.concatenate(heads, axis=1)
        o_ref[0, qrows, POOL_WIDTH:] = _rms(attn, ga_ref[0]).astype(BF16)


def _mixer(l, u3, qkv3, sink, wp_bf, pscale, gpool, gattn):
    b, s, _ = u3.shape
    nb = s // BLOCK
    step_rows = MIX_BLOCKS * BLOCK
    n_steps = s // step_rows
    halo_per_step = step_rows // POOL_HALO
    n_halo = s // POOL_HALO
    kcol = ATTN_WIDTH // KV_WIDTH
    vcol = kcol + 1
    layer = lambda bi, n: (l, 0, 0)
    prev_blk = lambda n: jnp.maximum(n * MIX_BLOCKS - 1, 0)
    next_blk = lambda n: jnp.minimum((n + 1) * MIX_BLOCKS, nb - 1)
    edge_spec = lambda blk, col: pl.BlockSpec((1, BLOCK, KV_WIDTH), lambda bi, n: (bi, blk(n), col))
    body_spec = lambda col: pl.BlockSpec((1, step_rows, KV_WIDTH), lambda bi, n: (bi, n, col))
    return pl.pallas_call(
        functools.partial(_mixer_kernel, seq=s, layer=l),
        grid=(b, n_steps),
        in_specs=[
            pl.BlockSpec(memory_space=pltpu.SMEM),
            pl.BlockSpec((1, step_rows, POOL_WIDTH), lambda bi, n: (bi, n, 0)),
            pl.BlockSpec((1, POOL_HALO, POOL_WIDTH),
                         lambda bi, n: (bi, jnp.maximum(n * halo_per_step - 1, 0), 0)),
            pl.BlockSpec((1, POOL_HALO, POOL_WIDTH),
                         lambda bi, n: (bi, jnp.minimum((n + 1) * halo_per_step, n_halo - 1), 0)),
            pl.BlockSpec((1, step_rows, ATTN_WIDTH), lambda bi, n: (bi, n, 0)),
            edge_spec(prev_blk, kcol), body_spec(kcol), edge_spec(next_blk, kcol),
            edge_spec(prev_blk, vcol), body_spec(vcol), edge_spec(next_blk, vcol),
            pl.BlockSpec((1, len(POOL_WINDOWS), POOL_GROUP, POOL_GROUP), lambda bi, n: (l, 0, 0, 0)),
            pl.BlockSpec((1, 1, POOL_WIDTH), layer),
            pl.BlockSpec((1, 1, POOL_WIDTH), layer),
            pl.BlockSpec((1, 1, ATTN_WIDTH), layer),
        ],
        out_specs=pl.BlockSpec((1, step_rows, POOL_WIDTH + ATTN_WIDTH), lambda bi, n: (bi, n, 0)),
        out_shape=jax.ShapeDtypeStruct((b, s, POOL_WIDTH + ATTN_WIDTH), BF16),
        compiler_params=pltpu.CompilerParams(
            dimension_semantics=("arbitrary", "arbitrary"), vmem_limit_bytes=VMEM_LIMIT),
        name="mixer",
    )(sink, u3, u3, u3, qkv3, qkv3, qkv3, qkv3, qkv3, qkv3, qkv3, wp_bf, pscale, gpool, gattn)


def _split_dot(a, b_hi, b_lo):
    a_hi = a.astype(BF16)
    a_lo = (a - a_hi.astype(F32)).astype(BF16)
    return (jnp.dot(a_hi, b_hi, preferred_element_type=F32)
            + jnp.dot(a_lo, b_hi, preferred_element_type=F32)
            + jnp.dot(a_hi, b_lo, preferred_element_type=F32))


def _out_proj_kernel(m_ref, x_ref, w_ref, g_ref, rh_ref, rl_ref, rb_ref,
                     x1_ref, h_ref, ri_ref, rw_ref, cnt_ref, run_ref):
    @pl.when(pl.program_id(0) == 0)
    def _():
        run_ref[...] = jnp.zeros_like(run_ref)

    x1 = x_ref[...] + jnp.dot(m_ref[...], w_ref[0], preferred_element_type=F32)
    x1_ref[...] = x1
    h = _rms(x1, g_ref[0])
    h_ref[...] = h
    logits = _split_dot(h, rh_ref[0], rl_ref[0]) + rb_ref[0]
    lane = lax.broadcasted_iota(I32, logits.shape, 1)
    far = jnp.int32(ROUTER_LANES)
    neg = -jnp.inf

    def top(vals):
        best = jnp.max(vals, axis=-1, keepdims=True)
        idx = jnp.min(jnp.where(vals == best, lane, far), axis=-1, keepdims=True)
        return best, idx

    gl = jnp.where((lane >= N_EXPERTS) & (lane < N_EXPERTS + N_GROUPS), logits, neg)
    gmax, gidx = top(gl)
    g_w = 1.0 / jnp.sum(jnp.exp(gl - gmax), axis=-1, keepdims=True)
    in_group = (lane < N_EXPERTS) & ((lane >> 3) == gidx - N_EXPERTS)
    el = jnp.where(in_group, logits, neg)
    m1, i1 = top(el)
    m2, i2 = top(jnp.where(lane == i1, neg, el))
    esum = jnp.sum(jnp.exp(el - m1), axis=-1, keepdims=True)
    p1 = 1.0 / esum
    p2 = jnp.exp(m2 - m1) / esum
    w1 = g_w * (p1 / (p1 + p2))
    w2 = g_w * (p2 / (p1 + p2))

    pick1 = lane == i1
    pick2 = lane == i2
    picks = (pick1 | pick2).astype(BF16)
    tm = logits.shape[0]
    lower = (lax.broadcasted_iota(I32, (tm, tm), 1) < lax.broadcasted_iota(I32, (tm, tm), 0)).astype(BF16)
    before = jnp.dot(lower, picks, preferred_element_type=F32) + run_ref[...]
    r1 = jnp.sum(jnp.where(pick1, before, 0.0), axis=-1, keepdims=True).astype(I32)
    r2 = jnp.sum(jnp.where(pick2, before, 0.0), axis=-1, keepdims=True).astype(I32)
    total = run_ref[...] + jnp.sum(picks.astype(F32), axis=0, keepdims=True)
    run_ref[...] = total
    cnt_ref[...] = total.astype(I32)

    ri_ref[...] = jnp.where(lane == 0, i1, jnp.where(lane == 1, i2,
                            jnp.where(lane == 2, r1, jnp.where(lane == 3, r2, 0))))
    rw_ref[...] = jnp.where(lane == 0, w1, jnp.where(lane == 1, w2, 0.0))


def _out_proj(l, mixed2, x2, w_bf, gain, r_hi, r_lo, r_b):
    t = x2.shape[0]
    row = lambda i: (i, 0)
    layer = lambda i: (l, 0, 0)
    return pl.pallas_call(
        _out_proj_kernel,
        grid=(t // TM_PROJ,),
        in_specs=[
            pl.BlockSpec((TM_PROJ, D_MODEL), row),
            pl.BlockSpec((TM_PROJ, D_MODEL), row),
            pl.BlockSpec((1, D_MODEL, D_MODEL), layer),
            pl.BlockSpec((1, 1, D_MODEL), layer),
            pl.BlockSpec((1, D_MODEL, ROUTER_LANES), layer),
            pl.BlockSpec((1, D_MODEL, ROUTER_LANES), layer),
            pl.BlockSpec((1, 1, ROUTER_LANES), layer),
        ],
        out_specs=[
            pl.BlockSpec((TM_PROJ, D_MODEL), row),
            pl.BlockSpec((TM_PROJ, D_MODEL), row),
            pl.BlockSpec((TM_PROJ, ROUTER_LANES), row),
            pl.BlockSpec((TM_PROJ, ROUTER_LANES), row),
            pl.BlockSpec((1, ROUTER_LANES), lambda i: (0, 0)),
        ],
        out_shape=[
            jax.ShapeDtypeStruct((t, D_MODEL), F32),
            jax.ShapeDtypeStruct((t, D_MODEL), F32),
            jax.ShapeDtypeStruct((t, ROUTER_LANES), I32),
            jax.ShapeDtypeStruct((t, ROUTER_LANES), F32),
            jax.ShapeDtypeStruct((1, ROUTER_LANES), I32),
        ],
        scratch_shapes=[pltpu.VMEM((1, ROUTER_LANES), F32)],
        compiler_params=pltpu.CompilerParams(
            dimension_semantics=("arbitrary",), vmem_limit_bytes=VMEM_LIMIT),
        name="out_proj",
    )(mixed2, x2, w_bf, gain, r_hi, r_lo, r_b)


def _plan_kernel(cnt_ref, start_ref, slot_ref, src_ref):
    i = pl.program_id(0)
    base = i * PLAN_CHUNK

    def body(j, carry):
        first = j * ISSUE_UNROLL
        slots = [slot_ref[0, 0, first + k] for k in range(ISSUE_UNROLL)]
        for k in range(ISSUE_UNROLL):
            src_ref[slots[k]] = (base + first + k) >> 1
        return carry
    lax.fori_loop(0, PLAN_CHUNK // ISSUE_UNROLL, body, 0)

    @pl.when(i == pl.num_programs(0) - 1)
    def _():
        n_slots = src_ref.shape[0]

        def per_expert(e, carry):
            lo = start_ref[e] + cnt_ref[e]
            hi = jnp.where(e == N_EXPERTS - 1, n_slots, start_ref[jnp.minimum(e + 1, N_EXPERTS - 1)])

            def fill(s, c2):
                src_ref[s] = 0
                return c2
            lax.fori_loop(lo, hi, fill, 0)
            return carry
        lax.fori_loop(0, N_EXPERTS, per_expert, 0)


def _plan(counts, start, slot_flat3, n_slots):
    n_chunks = slot_flat3.shape[0]
    return pl.pallas_call(
        _plan_kernel,
        grid_spec=pltpu.PrefetchScalarGridSpec(
            num_scalar_prefetch=2,
            grid=(n_chunks,),
            in_specs=[pl.BlockSpec((1, 1, PLAN_CHUNK), lambda i, c, s: (i, 0, 0),
                                   memory_space=pltpu.SMEM)],
            out_specs=pl.BlockSpec(memory_space=pltpu.SMEM),
        ),
        out_shape=jax.ShapeDtypeStruct((n_slots,), I32),
        compiler_params=pltpu.CompilerParams(dimension_semantics=("arbitrary",)),
        name="plan",
    )(counts, start, slot_flat3)


def _row_copy(src_hbm, idx_ref, dst, sem, r):
    return pltpu.make_async_copy(src_hbm.at[pl.ds(idx_ref[0, 0, r], 1)], dst.at[pl.ds(r, 1)], sem)


def _start_rows(src_hbm, idx_ref, dst, sem, n_rows):
    def body(j, carry):
        for k in range(ISSUE_UNROLL):
            _row_copy(src_hbm, idx_ref, dst, sem, j * ISSUE_UNROLL + k).start()
        return carry
    lax.fori_loop(0, n_rows // ISSUE_UNROLL, body, 0)


def _wait_rows(src_hbm, dst, sem, n_rows):
    pltpu.make_async_copy(src_hbm.at[pl.ds(0, n_rows)], dst, sem).wait()


def _moe_kernel(te_ref, grp_ref, nxt_ref, nu_ref, first_ref, ahead_ref, h_hbm, wg_hbm, wu_hbm, wd_hbm,
                y_ref, buf, sem, wg_st, wu_st, wd_st, wsem, wg_bf, wu_bf, wd_bf):
    i = pl.program_id(0)
    n_used = nu_ref[0]

    def weight_copies(e, par):
        return (pltpu.make_async_copy(wg_hbm.at[e], wg_st.at[par], wsem.at[par]),
                pltpu.make_async_copy(wu_hbm.at[e], wu_st.at[par], wsem.at[par]),
                pltpu.make_async_copy(wd_hbm.at[e], wd_st.at[par], wsem.at[par]))

    @pl.when(i == 0)
    def _():
        _start_rows(h_hbm, first_ref, buf.at[0], sem.at[0], TM_MOE)
        for c in weight_copies(te_ref[0], 0):
            c.start(priority=WEIGHT_DMA_PRIORITY)

    @pl.when(i < n_used)
    def _():
        slot = i % 2
        other = 1 - slot
        _wait_rows(h_hbm, buf.at[slot], sem.at[slot], TM_MOE)

        @pl.when((i == 0) | (te_ref[i] != te_ref[jnp.maximum(i - 1, 0)]))
        def _():
            par = grp_ref[i] % 2
            for c in weight_copies(te_ref[i], par):
                c.wait()

            @pl.when(nxt_ref[i] >= 0)
            def _():
                for c in weight_copies(nxt_ref[i], 1 - par):
                    c.start(priority=WEIGHT_DMA_PRIORITY)

            wg_bf[...] = wg_st[par].astype(BF16)
            wu_bf[...] = wu_st[par].astype(BF16)
            wd_bf[...] = wd_st[par].astype(BF16)

        def request(part):
            for r in range(part * TM_MOE // 4, (part + 1) * TM_MOE // 4):
                _row_copy(h_hbm, ahead_ref, buf.at[other], sem.at[other], r).start()

        request(0)
        xb = buf[slot].astype(BF16)
        g = jnp.dot(xb, wg_bf[...], preferred_element_type=F32)
        request(1)
        u = jnp.dot(xb, wu_bf[...], preferred_element_type=F32)
        request(2)
        a = (g * jax.nn.sigmoid(g) * u).astype(BF16)
        y = jnp.dot(a, wd_bf[...], preferred_element_type=F32)
        request(3)
        y_ref[...] = y

        @pl.when(i == n_used - 1)
        def _():
            _wait_rows(h_hbm, buf.at[other], sem.at[other], TM_MOE)

    @pl.when(i >= n_used)
    def _():
        y_ref[...] = jnp.zeros_like(y_ref)


def _moe_ffn(h2, w_gate, w_up, w_down, tile_expert, tile_group, tile_next, n_used, src3):
    n_tiles = src3.shape[0]
    smem_tile = lambda index_map: pl.BlockSpec((1, 1, TM_MOE), index_map, memory_space=pltpu.SMEM)
    hbm = pl.BlockSpec(memory_space=pl.ANY)
    return pl.pallas_call(
        _moe_kernel,
        grid_spec=pltpu.PrefetchScalarGridSpec(
            num_scalar_prefetch=4,
            grid=(n_tiles,),
            in_specs=[
                smem_tile(lambda i, te, gr, nx, nu: (0, 0, 0)),
                smem_tile(lambda i, te, gr, nx, nu:
                          (jnp.maximum(jnp.minimum(i + 1, nu[0] - 1), 0), 0, 0)),
                hbm, hbm, hbm, hbm,
            ],
            out_specs=pl.BlockSpec((TM_MOE, D_MODEL), lambda i, te, gr, nx, nu: (i, 0)),
            scratch_shapes=[
                pltpu.VMEM((2, TM_MOE, D_MODEL), F32),
                pltpu.SemaphoreType.DMA((2,)),
                pltpu.VMEM((2, D_MODEL, D_EXPERT), F32),
                pltpu.VMEM((2, D_MODEL, D_EXPERT), F32),
                pltpu.VMEM((2, D_EXPERT, D_MODEL), F32),
                pltpu.SemaphoreType.DMA((2,)),
                pltpu.VMEM((D_MODEL, D_EXPERT), BF16),
                pltpu.VMEM((D_MODEL, D_EXPERT), BF16),
                pltpu.VMEM((D_EXPERT, D_MODEL), BF16),
            ],
        ),
        out_shape=jax.ShapeDtypeStruct((n_tiles * TM_MOE, D_MODEL), F32),
        compiler_params=pltpu.CompilerParams(
            dimension_semantics=("arbitrary",), vmem_limit_bytes=VMEM_LIMIT),
        name="moe_ffn",
    )(tile_expert, tile_group, tile_next, n_used, src3, src3, h2, w_gate, w_up, w_down)


def _combine_kernel(first_ref, ahead_ref, x_ref, w_ref, y_hbm, o_ref, buf, sem):
    i = pl.program_id(0)
    n = pl.num_programs(0)
    rows = 2 * TM_COMBINE

    @pl.when(i == 0)
    def _():
        _start_rows(y_hbm, first_ref, buf.at[0], sem.at[0], rows)

    @pl.when(i + 1 < n)
    def _():
        nxt = (i + 1) % 2
        for r in range(rows):
            _row_copy(y_hbm, ahead_ref, buf.at[nxt], sem.at[nxt], r).start(priority=r % 2)

    slot = i % 2
    _wait_rows(y_hbm, buf.at[slot], sem.at[slot], rows)
    w = w_ref[...]
    o_ref[...] = (x_ref[...] + w[:, 0:1] * buf[slot, :TM_COMBINE]
                  + w[:, 1:2] * buf[slot, TM_COMBINE:])


def _combine(x1, rw, y, slots3):
    t = x1.shape[0]
    n = t // TM_COMBINE
    row = lambda i: (i, 0)
    smem_tile = lambda index_map: pl.BlockSpec((1, 1, 2 * TM_COMBINE), index_map,
                                               memory_space=pltpu.SMEM)
    return pl.pallas_call(
        _combine_kernel,
        grid=(n,),
        in_specs=[
            smem_tile(lambda i: (0, 0, 0)),
            smem_tile(lambda i: (jnp.minimum(i + 1, n - 1), 0, 0)),
            pl.BlockSpec((TM_COMBINE, D_MODEL), row),
            pl.BlockSpec((TM_COMBINE, ROUTER_LANES), row),
            pl.BlockSpec(memory_space=pl.ANY),
        ],
        out_specs=pl.BlockSpec((TM_COMBINE, D_MODEL), row),
        out_shape=jax.ShapeDtypeStruct((t, D_MODEL), F32),
        scratch_shapes=[
            pltpu.VMEM((2, 2 * TM_COMBINE, D_MODEL), F32),
            pltpu.SemaphoreType.DMA((2,)),
        ],
        compiler_params=pltpu.CompilerParams(
            dimension_semantics=("arbitrary",), vmem_limit_bytes=VMEM_LIMIT),
        name="combine",
    )(slots3, slots3, x1, rw, y)


def _slot_tables(l, ri, counts_row, n_tiles):
    t = ri.shape[0]
    experts = jnp.arange(N_EXPERTS, dtype=I32)
    counts = counts_row[0, :N_EXPERTS]
    tiles_per = (counts + TM_MOE - 1) // TM_MOE
    tile_end = jnp.cumsum(tiles_per)
    start = (tile_end - tiles_per) * TM_MOE
    n_used = tile_end[-1:].astype(I32)
    nonempty = counts > 0
    group_of = jnp.cumsum(nonempty.astype(I32)) - 1
    later = (experts[None, :] > experts[:, None]) & nonempty[None, :]
    next_of = jnp.min(jnp.where(later, experts[None, :], N_EXPERTS), axis=1)
    next_of = jnp.where(next_of < N_EXPERTS, next_of + l * N_EXPERTS, -1)
    tile_idx = jnp.arange(n_tiles, dtype=I32)
    tile_e = jnp.minimum(jnp.sum((tile_end[None, :] <= tile_idx[:, None]).astype(I32), axis=1),
                         N_EXPERTS - 1)
    pick = tile_e[:, None] == experts[None, :]
    tile_group = jnp.sum(jnp.where(pick, group_of[None, :], 0), axis=1)
    tile_next = jnp.sum(jnp.where(pick, next_of[None, :], 0), axis=1)
    tile_expert = tile_e + l * N_EXPERTS
    eid = ri[:, 0:2]
    slot = ri[:, 2:4] + jnp.sum(jnp.where(eid[..., None] == experts, start, 0), axis=-1)
    slot_flat3 = slot.reshape((2 * t) // PLAN_CHUNK, 1, PLAN_CHUNK)
    slot2 = slot.reshape(t // TM_COMBINE, TM_COMBINE, 2)
    slots3 = jnp.swapaxes(slot2, 1, 2).reshape(t // TM_COMBINE, 1, 2 * TM_COMBINE)
    tiles = (tile_expert.astype(I32), tile_group.astype(I32), tile_next.astype(I32), n_used)
    return counts, start.astype(I32), tiles, slot_flat3, slots3


def _rope_freq_row():
    inv_freq = ROPE_THETA ** (-(jnp.arange(0, ROT_DIM, 2, dtype=F32) / ROT_DIM))
    return jnp.concatenate(
        [inv_freq, inv_freq, jnp.zeros((HEAD_DIM - ROT_DIM,), F32)]).reshape(1, HEAD_DIM)


def _router_operands(w_rg, b_rg, w_re, b_re):
    depth = w_rg.shape[0]
    pad = ROUTER_LANES - N_EXPERTS - N_GROUPS
    w = jnp.concatenate([w_re, w_rg, jnp.zeros((depth, D_MODEL, pad), F32)], axis=2)
    b = jnp.concatenate([b_re, b_rg, jnp.zeros((depth, pad), F32)], axis=1)
    hi = w.astype(BF16)
    lo = (w - hi.astype(F32)).astype(BF16)
    return hi, lo, b


def kernel(x, positions, norm_mix, w_in, w_pool, pool_scale, q_norm, k_norm, sink, branch_gain_pool,
           branch_gain_attn, w_out, norm_ffn, w_router_group, b_router_group, w_router_expert,
           b_router_expert, w_gate, w_up, w_down):
    b, s, d = x.shape
    t = b * s
    depth = norm_mix.shape[0]
    x2 = x.reshape(t, d)
    pos2 = positions.reshape(t, 1)
    freq = _rope_freq_row()
    w_in_bf = w_in.astype(BF16)
    w_out_bf = w_out.astype(BF16)
    w_pool_bf = w_pool.astype(BF16)
    r_hi, r_lo, r_b = _router_operands(w_router_group, b_router_group, w_router_expert,
                                       b_router_expert)
    wg = w_gate.reshape(depth * N_EXPERTS, D_MODEL, D_EXPERT)
    wu = w_up.reshape(depth * N_EXPERTS, D_MODEL, D_EXPERT)
    wd = w_down.reshape(depth * N_EXPERTS, D_EXPERT, D_MODEL)
    n_tiles = (2 * t) // TM_MOE + N_EXPERTS
    rows = lambda v: v.reshape(depth, 1, v.shape[-1])
    for l in range(depth):
        u, qkv = _in_proj(l, x2, pos2, rows(norm_mix), w_in_bf, rows(q_norm), rows(k_norm), freq)
        mixed = _mixer(l, u.reshape(b, s, POOL_WIDTH), qkv.reshape(b, s, QKV_WIDTH), sink, w_pool_bf,
                       rows(pool_scale), rows(branch_gain_pool), rows(branch_gain_attn))
        x1, h, ri, rw, counts_row = _out_proj(l, mixed.reshape(t, D_MODEL), x2, w_out_bf,
                                              rows(norm_ffn), r_hi, r_lo, rows(r_b))
        counts, start, tiles, slot_flat3, slots3 = _slot_tables(l, ri, counts_row, n_tiles)
        src = _plan(counts, start, slot_flat3, n_tiles * TM_MOE)
        y = _moe_ffn(h, wg, wu, wd, *tiles, src.reshape(n_tiles, 1, TM_MOE))
        x2 = _combine(x1, rw, y, slots3)
    return x2.reshape(b, s, d)
```

```python
import functools

import jax
import jax.numpy as jnp
from jax import lax
from jax.experimental import pallas as pl
from jax.experimental.pallas import tpu as pltpu

F32 = jnp.float32
BF16 = jnp.bfloat16
I32 = jnp.int32

D_MODEL = 2048
POOL_WIDTH = 1024
POOL_WINDOWS = (2, 4, 8, 16)
POOL_GROUP = 256
HEAD_DIM = 128
N_Q_HEADS = 8
N_KV_HEADS = 2
Q_PER_KV = 4
ATTN_WIDTH = 1024
KV_WIDTH = 256
QKV_WIDTH = ATTN_WIDTH + 2 * KV_WIDTH
IN_WIDTH = POOL_WIDTH + QKV_WIDTH
WINDOW = 128
BLOCK = 128
ROPE_THETA = 500000.0
ROT_DIM = 32
ROT_HALF = ROT_DIM // 2
N_GROUPS = 4
EXPERTS_PER_GROUP = 8
N_EXPERTS = 32
D_EXPERT = 512
EPS = 1e-6

LANES = 128
POOL_HALO = 8
ROUTER_LANES = LANES
TM_PROJ = 256
TM_IN = 512
SUB_ROWS = 256
MIX_BLOCKS = 2
TM_MOE = 256
TM_COMBINE = 256
PLAN_CHUNK = 8192
ISSUE_UNROLL = 16
VMEM_LIMIT = 56 * 1024 * 1024


def _rms(x, gain):
    return x * lax.rsqrt(jnp.mean(x * x, axis=-1, keepdims=True) + EPS) * gain


def _in_proj_kernel(x_ref, pos_ref, g_ref, w_ref, qn_ref, kn_ref, freq_ref, u_ref, qkv_ref):
    for sub in range(TM_IN // SUB_ROWS):
        rows = slice(sub * SUB_ROWS, (sub + 1) * SUB_ROWS)
        h = _rms(x_ref[rows], g_ref[0]).astype(BF16)
        z = jnp.dot(h, w_ref[0], preferred_element_type=F32)
        u_ref[rows] = z[:, :POOL_WIDTH]

        ang = pos_ref[rows].astype(F32) * freq_ref[...]
        cos = jnp.cos(ang)
        sin = jnp.sin(ang)
        lane = lax.broadcasted_iota(I32, ang.shape, 1)
        sin_hi = jnp.where(lane >= ROT_HALF, sin, 0.0)
        sin_lo = jnp.where(lane < ROT_HALF, -sin, 0.0)

        def norm_rope(t, gain):
            y = _rms(t, gain)
            return (y * cos + pltpu.roll(y, ROT_HALF, 1) * sin_hi
                    + pltpu.roll(y, HEAD_DIM - ROT_HALF, 1) * sin_lo)

        for hd in range(N_Q_HEADS + N_KV_HEADS):
            gain = qn_ref[0] if hd < N_Q_HEADS else kn_ref[0]
            src = POOL_WIDTH + hd * HEAD_DIM
            dst = hd * HEAD_DIM
            qkv_ref[rows, dst:dst + HEAD_DIM] = norm_rope(z[:, src:src + HEAD_DIM], gain).astype(BF16)
        v0 = POOL_WIDTH + ATTN_WIDTH + KV_WIDTH
        qkv_ref[rows, ATTN_WIDTH + KV_WIDTH:] = z[:, v0:].astype(BF16)


def _in_proj(l, x2, pos2, gain, w_bf, qn, kn, freq):
    t = x2.shape[0]
    row = lambda i: (i, 0)
    layer = lambda i: (l, 0, 0)
    return pl.pallas_call(
        _in_proj_kernel,
        grid=(t // TM_IN,),
        in_specs=[
            pl.BlockSpec((TM_IN, D_MODEL), row),
            pl.BlockSpec((TM_IN, 1), row),
            pl.BlockSpec((1, 1, D_MODEL), layer),
            pl.BlockSpec((1, D_MODEL, IN_WIDTH), layer),
            pl.BlockSpec((1, 1, HEAD_DIM), layer),
            pl.BlockSpec((1, 1, HEAD_DIM), layer),
            pl.BlockSpec((1, HEAD_DIM), lambda i: (0, 0)),
        ],
        out_specs=[
            pl.BlockSpec((TM_IN, POOL_WIDTH), row),
            pl.BlockSpec((TM_IN, QKV_WIDTH), row),
        ],
        out_shape=[
            jax.ShapeDtypeStruct((t, POOL_WIDTH), F32),
            jax.ShapeDtypeStruct((t, QKV_WIDTH), BF16),
        ],
        compiler_params=pltpu.CompilerParams(
            dimension_semantics=("arbitrary",), vmem_limit_bytes=VMEM_LIMIT),
        name="in_proj",
    )(x2, pos2, gain, w_bf, qn, kn, freq)


def _mixer_kernel(sink_ref, u_ref, up_ref, un_ref, q_ref, kp_ref, kc_ref, kn_ref,
                  vp_ref, vc_ref, vn_ref, wp_ref, ps_ref, gp_ref, ga_ref, o_ref, *, seq, layer):
    n = pl.program_id(1)
    n_steps = pl.num_programs(1)
    step_rows = MIX_BLOCKS * BLOCK
    t0 = n * step_rows

    prev = jnp.where(n > 0, up_ref[0], 0.0)
    nxt = jnp.where(n < n_steps - 1, un_ref[0], 0.0)
    cur = u_ref[0]
    ext = jnp.concatenate([prev, cur, nxt], axis=0)
    rows = ext.shape[0]
    tpos = t0 + lax.broadcasted_iota(I32, (step_rows, POOL_GROUP), 0)
    pooled = []
    for gi, w in enumerate(POOL_WINDOWS):
        sl = slice(gi * POOL_GROUP, (gi + 1) * POOL_GROUP)
        acc = ext[:, sl]
        span = 1
        while span < w:
            acc = acc + pltpu.roll(acc, span, 0)
            span *= 2
        lead = w // 2 - 1
        if lead:
            acc = pltpu.roll(acc, rows - lead, 0)
        wsum = acc[POOL_HALO:POOL_HALO + step_rows]
        lo = jnp.clip(tpos - w // 2, 0, seq)
        hi = jnp.clip(tpos + w // 2, 0, seq)
        y = wsum / (hi - lo).astype(F32) - cur[:, sl]
        y = jnp.dot(y.astype(BF16), wp_ref[0, gi], preferred_element_type=F32)
        pooled.append(y)
    pool = jnp.concatenate(pooled, axis=1) * ps_ref[0]
    o_ref[0, :, :POOL_WIDTH] = _rms(pool, gp_ref[0]).astype(BF16)

    r = lax.broadcasted_iota(I32, (Q_PER_KV * BLOCK, 3 * BLOCK), 0) % BLOCK
    c = lax.broadcasted_iota(I32, (Q_PER_KV * BLOCK, 3 * BLOCK), 1)
    band = (c >= r) & (c <= r + 2 * WINDOW)
    kcat, vcat = [], []
    for kh in range(N_KV_HEADS):
        ks = slice(kh * HEAD_DIM, (kh + 1) * HEAD_DIM)
        kcat.append(jnp.concatenate([kp_ref[0, :, ks], kc_ref[0, :, ks], kn_ref[0, :, ks]], axis=0))
        vcat.append(jnp.concatenate([vp_ref[0, :, ks], vc_ref[0, :, ks], vn_ref[0, :, ks]], axis=0))
    for j in range(MIX_BLOCKS):
        kj = t0 + (j - 1) * BLOCK + c
        mask = band & (kj >= 0) & (kj < seq)
        qrows = slice(j * BLOCK, (j + 1) * BLOCK)
        krows = slice(j * BLOCK, (j + 3) * BLOCK)
        heads = []
        for kh in range(N_KV_HEADS):
            q = jnp.concatenate(
                [q_ref[0, qrows, (kh * Q_PER_KV + g) * HEAD_DIM:(kh * Q_PER_KV + g + 1) * HEAD_DIM]
                 for g in range(Q_PER_KV)], axis=0)
            s = lax.dot_general(q, kcat[kh][krows], (((1,), (1,)), ((), ())),
                                preferred_element_type=F32)
            s = jnp.where(mask, s * (HEAD_DIM ** -0.5), -jnp.inf)
            ps, dens = [], []
            for g in range(Q_PER_KV):
                sg = s[g * BLOCK:(g + 1) * BLOCK]
                sink = sink_ref[layer, kh * Q_PER_KV + g]
                m = jnp.maximum(jnp.max(sg, axis=-1, keepdims=True), sink)
                pg = jnp.exp(sg - m)
                dens.append(jnp.sum(pg, axis=-1, keepdims=True) + jnp.exp(sink - m))
                ps.append(pg.astype(BF16))
            o = jnp.dot(jnp.concatenate(ps, axis=0), vcat[kh][krows], preferred_element_type=F32)
            heads.extend(o[g * BLOCK:(g + 1) * BLOCK] / dens[g] for g in range(Q_PER_KV))
        attn = jnp.concatenate(heads, axis=1)
        o_ref[0, qrows, POOL_WIDTH:] = _rms(attn, ga_ref[0]).astype(BF16)


def _mixer(l, u3, qkv3, sink, wp_bf, pscale, gpool, gattn):
    b, s, _ = u3.shape
    nb = s // BLOCK
    step_rows = MIX_BLOCKS * BLOCK
    n_steps = s // step_rows
    halo_per_step = step_rows // POOL_HALO
    n_halo = s // POOL_HALO
    kcol = ATTN_WIDTH // KV_WIDTH
    vcol = kcol + 1
    layer = lambda bi, n: (l, 0, 0)
    prev_blk = lambda n: jnp.maximum(n * MIX_BLOCKS - 1, 0)
    next_blk = lambda n: jnp.minimum((n + 1) * MIX_BLOCKS, nb - 1)
    edge_spec = lambda blk, col: pl.BlockSpec((1, BLOCK, KV_WIDTH), lambda bi, n: (bi, blk(n), col))
    body_spec = lambda col: pl.BlockSpec((1, step_rows, KV_WIDTH), lambda bi, n: (bi, n, col))
    return pl.pallas_call(
        functools.partial(_mixer_kernel, seq=s, layer=l),
        grid=(b, n_steps),
        in_specs=[
            pl.BlockSpec(memory_space=pltpu.SMEM),
            pl.BlockSpec((1, step_rows, POOL_WIDTH), lambda bi, n: (bi, n, 0)),
            pl.BlockSpec((1, POOL_HALO, POOL_WIDTH),
                         lambda bi, n: (bi, jnp.maximum(n * halo_per_step - 1, 0), 0)),
            pl.BlockSpec((1, POOL_HALO, POOL_WIDTH),
                         lambda bi, n: (bi, jnp.minimum((n + 1) * halo_per_step, n_halo - 1), 0)),
            pl.BlockSpec((1, step_rows, ATTN_WIDTH), lambda bi, n: (bi, n, 0)),
            edge_spec(prev_blk, kcol), body_spec(kcol), edge_spec(next_blk, kcol),
            edge_spec(prev_blk, vcol), body_spec(vcol), edge_spec(next_blk, vcol),
            pl.BlockSpec((1, len(POOL_WINDOWS), POOL_GROUP, POOL_GROUP), lambda bi, n: (l, 0, 0, 0)),
            pl.BlockSpec((1, 1, POOL_WIDTH), layer),
            pl.BlockSpec((1, 1, POOL_WIDTH), layer),
            pl.BlockSpec((1, 1, ATTN_WIDTH), layer),
        ],
        out_specs=pl.BlockSpec((1, step_rows, POOL_WIDTH + ATTN_WIDTH), lambda bi, n: (bi, n, 0)),
        out_shape=jax.ShapeDtypeStruct((b, s, POOL_WIDTH + ATTN_WIDTH), BF16),
        compiler_params=pltpu.CompilerParams(
            dimension_semantics=("arbitrary", "arbitrary"), vmem_limit_bytes=VMEM_LIMIT),
        name="mixer",
    )(sink, u3, u3, u3, qkv3, qkv3, qkv3, qkv3, qkv3, qkv3, qkv3, wp_bf, pscale, gpool, gattn)


def _split_dot(a, b_hi, b_lo):
    a_hi = a.astype(BF16)
    a_lo = (a - a_hi.astype(F32)).astype(BF16)
    return (jnp.dot(a_hi, b_hi, preferred_element_type=F32)
            + jnp.dot(a_lo, b_hi, preferred_element_type=F32)
            + jnp.dot(a_hi, b_lo, preferred_element_type=F32))


def _out_proj_kernel(m_ref, x_ref, w_ref, g_ref, rh_ref, rl_ref, rb_ref,
                     x1_ref, h_ref, ri_ref, rw_ref, cnt_ref, run_ref):
    @pl.when(pl.program_id(0) == 0)
    def _():
        run_ref[...] = jnp.zeros_like(run_ref)

    x1 = x_ref[...] + jnp.dot(m_ref[...], w_ref[0], preferred_element_type=F32)
    x1_ref[...] = x1
    h = _rms(x1, g_ref[0])
    h_ref[...] = h
    logits = _split_dot(h, rh_ref[0], rl_ref[0]) + rb_ref[0]
    lane = lax.broadcasted_iota(I32, logits.shape, 1)
    far = jnp.int32(ROUTER_LANES)
    neg = -jnp.inf

    def top(vals):
        best = jnp.max(vals, axis=-1, keepdims=True)
        idx = jnp.min(jnp.where(vals == best, lane, far), axis=-1, keepdims=True)
        return best, idx

    gl = jnp.where((lane >= N_EXPERTS) & (lane < N_EXPERTS + N_GROUPS), logits, neg)
    gmax, gidx = top(gl)
    g_w = 1.0 / jnp.sum(jnp.exp(gl - gmax), axis=-1, keepdims=True)
    in_group = (lane < N_EXPERTS) & ((lane >> 3) == gidx - N_EXPERTS)
    el = jnp.where(in_group, logits, neg)
    m1, i1 = top(el)
    m2, i2 = top(jnp.where(lane == i1, neg, el))
    esum = jnp.sum(jnp.exp(el - m1), axis=-1, keepdims=True)
    p1 = 1.0 / esum
    p2 = jnp.exp(m2 - m1) / esum
    w1 = g_w * (p1 / (p1 + p2))
    w2 = g_w * (p2 / (p1 + p2))

    pick1 = lane == i1
    pick2 = lane == i2
    picks = (pick1 | pick2).astype(BF16)
    tm = logits.shape[0]
    lower = (lax.broadcasted_iota(I32, (tm, tm), 1) < lax.broadcasted_iota(I32, (tm, tm), 0)).astype(BF16)
    before = jnp.dot(lower, picks, preferred_element_type=F32) + run_ref[...]
    r1 = jnp.sum(jnp.where(pick1, before, 0.0), axis=-1, keepdims=True).astype(I32)
    r2 = jnp.sum(jnp.where(pick2, before, 0.0), axis=-1, keepdims=True).astype(I32)
    total = run_ref[...] + jnp.sum(picks.astype(F32), axis=0, keepdims=True)
    run_ref[...] = total
    cnt_ref[...] = total.astype(I32)

    ri_ref[...] = jnp.where(lane == 0, i1, jnp.where(lane == 1, i2,
                            jnp.where(lane == 2, r1, jnp.where(lane == 3, r2, 0))))
    rw_ref[...] = jnp.where(lane == 0, w1, jnp.where(lane == 1, w2, 0.0))


def _out_proj(l, mixed2, x2, w_bf, gain, r_hi, r_lo, r_b):
    t = x2.shape[0]
    row = lambda i: (i, 0)
    layer = lambda i: (l, 0, 0)
    return pl.pallas_call(
        _out_proj_kernel,
        grid=(t // TM_PROJ,),
        in_specs=[
            pl.BlockSpec((TM_PROJ, D_MODEL), row),
            pl.BlockSpec((TM_PROJ, D_MODEL), row),
            pl.BlockSpec((1, D_MODEL, D_MODEL), layer),
            pl.BlockSpec((1, 1, D_MODEL), layer),
            pl.BlockSpec((1, D_MODEL, ROUTER_LANES), layer),
            pl.BlockSpec((1, D_MODEL, ROUTER_LANES), layer),
            pl.BlockSpec((1, 1, ROUTER_LANES), layer),
        ],
        out_specs=[
            pl.BlockSpec((TM_PROJ, D_MODEL), row),
            pl.BlockSpec((TM_PROJ, D_MODEL), row),
            pl.BlockSpec((TM_PROJ, ROUTER_LANES), row),
            pl.BlockSpec((TM_PROJ, ROUTER_LANES), row),
            pl.BlockSpec((1, ROUTER_LANES), lambda i: (0, 0)),
        ],
        out_shape=[
            jax.ShapeDtypeStruct((t, D_MODEL), F32),
            jax.ShapeDtypeStruct((t, D_MODEL), F32),
            jax.ShapeDtypeStruct((t, ROUTER_LANES), I32),
            jax.ShapeDtypeStruct((t, ROUTER_LANES), F32),
            jax.ShapeDtypeStruct((1, ROUTER_LANES), I32),
        ],
        scratch_shapes=[pltpu.VMEM((1, ROUTER_LANES), F32)],
        compiler_params=pltpu.CompilerParams(
            dimension_semantics=("arbitrary",), vmem_limit_bytes=VMEM_LIMIT),
        name="out_proj",
    )(mixed2, x2, w_bf, gain, r_hi, r_lo, r_b)


def _plan_kernel(cnt_ref, start_ref, slot_ref, src_ref):
    i = pl.program_id(0)
    base = i * PLAN_CHUNK

    def body(j, carry):
        first = j * ISSUE_UNROLL
        slots = [slot_ref[0, 0, first + k] for k in range(ISSUE_UNROLL)]
        for k in range(ISSUE_UNROLL):
            src_ref[slots[k]] = (base + first + k) >> 1
        return carry
    lax.fori_loop(0, PLAN_CHUNK // ISSUE_UNROLL, body, 0)

    @pl.when(i == pl.num_programs(0) - 1)
    def _():
        n_slots = src_ref.shape[0]

        def per_expert(e, carry):
            lo = start_ref[e] + cnt_ref[e]
            hi = jnp.where(e == N_EXPERTS - 1, n_slots, start_ref[jnp.minimum(e + 1, N_EXPERTS - 1)])

            def fill(s, c2):
                src_ref[s] = 0
                return c2
            lax.fori_loop(lo, hi, fill, 0)
            return carry
        lax.fori_loop(0, N_EXPERTS, per_expert, 0)


def _plan(counts, start, slot_flat3, n_slots):
    n_chunks = slot_flat3.shape[0]
    return pl.pallas_call(
        _plan_kernel,
        grid_spec=pltpu.PrefetchScalarGridSpec(
            num_scalar_prefetch=2,
            grid=(n_chunks,),
            in_specs=[pl.BlockSpec((1, 1, PLAN_CHUNK), lambda i, c, s: (i, 0, 0),
                                   memory_space=pltpu.SMEM)],
            out_specs=pl.BlockSpec(memory_space=pltpu.SMEM),
        ),
        out_shape=jax.ShapeDtypeStruct((n_slots,), I32),
        compiler_params=pltpu.CompilerParams(dimension_semantics=("arbitrary",)),
        name="plan",
    )(counts, start, slot_flat3)


def _row_copy(src_hbm, idx_ref, dst, sem, r):
    return pltpu.make_async_copy(src_hbm.at[pl.ds(idx_ref[0, 0, r], 1)], dst.at[pl.ds(r, 1)], sem)


def _start_rows(src_hbm, idx_ref, dst, sem, n_rows):
    def body(j, carry):
        for k in range(ISSUE_UNROLL):
            _row_copy(src_hbm, idx_ref, dst, sem, j * ISSUE_UNROLL + k).start()
        return carry
    lax.fori_loop(0, n_rows // ISSUE_UNROLL, body, 0)


def _wait_rows(src_hbm, dst, sem, n_rows):
    pltpu.make_async_copy(src_hbm.at[pl.ds(0, n_rows)], dst, sem).wait()


def _sort_rows_kernel(nu_ref, first_ref, ahead_ref, h_hbm, o_ref, buf, sem):
    i = pl.program_id(0)
    n_used = nu_ref[0]

    @pl.when(i == 0)
    def _():
        _start_rows(h_hbm, first_ref, buf.at[0], sem.at[0], TM_MOE)

    @pl.when(i + 1 < n_used)
    def _():
        nxt = (i + 1) % 2
        for r in range(TM_MOE):
            _row_copy(h_hbm, ahead_ref, buf.at[nxt], sem.at[nxt], r).start()

    @pl.when(i < n_used)
    def _():
        slot = i % 2
        _wait_rows(h_hbm, buf.at[slot], sem.at[slot], TM_MOE)
        o_ref[...] = buf[slot].astype(BF16)

    @pl.when(i >= n_used)
    def _():
        o_ref[...] = jnp.zeros_like(o_ref)


def _sort_rows(h2, n_used, src3):
    n_tiles = src3.shape[0]
    smem_tile = lambda index_map: pl.BlockSpec((1, 1, TM_MOE), index_map, memory_space=pltpu.SMEM)
    return pl.pallas_call(
        _sort_rows_kernel,
        grid_spec=pltpu.PrefetchScalarGridSpec(
            num_scalar_prefetch=1,
            grid=(n_tiles,),
            in_specs=[
                smem_tile(lambda i, nu: (0, 0, 0)),
                smem_tile(lambda i, nu: (jnp.minimum(i + 1, n_tiles - 1), 0, 0)),
                pl.BlockSpec(memory_space=pl.ANY),
            ],
            out_specs=pl.BlockSpec((TM_MOE, D_MODEL), lambda i, nu: (i, 0)),
            scratch_shapes=[
                pltpu.VMEM((2, TM_MOE, D_MODEL), F32),
                pltpu.SemaphoreType.DMA((2,)),
            ],
        ),
        out_shape=jax.ShapeDtypeStruct((n_tiles * TM_MOE, D_MODEL), BF16),
        compiler_params=pltpu.CompilerParams(
            dimension_semantics=("arbitrary",), vmem_limit_bytes=VMEM_LIMIT),
        name="sort_rows",
    )(n_used, src3, src3, h2)


def _moe_kernel(te_ref, grp_ref, nxt_ref, nu_ref, x_ref, wg_hbm, wu_hbm, wd_hbm,
                y_ref, wg_st, wu_st, wd_st, wsem, wg_bf, wu_bf, wd_bf):
    i = pl.program_id(0)
    n_used = nu_ref[0]

    def weight_copies(e, par):
        return (pltpu.make_async_copy(wg_hbm.at[e], wg_st.at[par], wsem.at[par]),
                pltpu.make_async_copy(wu_hbm.at[e], wu_st.at[par], wsem.at[par]),
                pltpu.make_async_copy(wd_hbm.at[e], wd_st.at[par], wsem.at[par]))

    @pl.when(i == 0)
    def _():
        for c in weight_copies(te_ref[0], 0):
            c.start()

    @pl.when(i < n_used)
    def _():
        @pl.when((i == 0) | (te_ref[i] != te_ref[jnp.maximum(i - 1, 0)]))
        def _():
            par = grp_ref[i] % 2
            for c in weight_copies(te_ref[i], par):
                c.wait()

            @pl.when(nxt_ref[i] >= 0)
            def _():
                for c in weight_copies(nxt_ref[i], 1 - par):
                    c.start()

            wg_bf[...] = wg_st[par].astype(BF16)
            wu_bf[...] = wu_st[par].astype(BF16)
            wd_bf[...] = wd_st[par].astype(BF16)

        xb = x_ref[...]
        g = jnp.dot(xb, wg_bf[...], preferred_element_type=F32)
        u = jnp.dot(xb, wu_bf[...], preferred_element_type=F32)
        a = (g * jax.nn.sigmoid(g) * u).astype(BF16)
        y_ref[...] = jnp.dot(a, wd_bf[...], preferred_element_type=F32)

    @pl.when(i >= n_used)
    def _():
        y_ref[...] = jnp.zeros_like(y_ref)


def _moe_ffn(hs, w_gate, w_up, w_down, tile_expert, tile_group, tile_next, n_used):
    n_tiles = hs.shape[0] // TM_MOE
    hbm = pl.BlockSpec(memory_space=pl.ANY)
    tile = lambda i, te, gr, nx, nu: (i, 0)
    return pl.pallas_call(
        _moe_kernel,
        grid_spec=pltpu.PrefetchScalarGridSpec(
            num_scalar_prefetch=4,
            grid=(n_tiles,),
            in_specs=[pl.BlockSpec((TM_MOE, D_MODEL), tile), hbm, hbm, hbm],
            out_specs=pl.BlockSpec((TM_MOE, D_MODEL), tile),
            scratch_shapes=[
                pltpu.VMEM((2, D_MODEL, D_EXPERT), F32),
                pltpu.VMEM((2, D_MODEL, D_EXPERT), F32),
                pltpu.VMEM((2, D_EXPERT, D_MODEL), F32),
                pltpu.SemaphoreType.DMA((2,)),
                pltpu.VMEM((D_MODEL, D_EXPERT), BF16),
                pltpu.VMEM((D_MODEL, D_EXPERT), BF16),
                pltpu.VMEM((D_EXPERT, D_MODEL), BF16),
            ],
        ),
        out_shape=jax.ShapeDtypeStruct((n_tiles * TM_MOE, D_MODEL), F32),
        compiler_params=pltpu.CompilerParams(
            dimension_semantics=("arbitrary",), vmem_limit_bytes=VMEM_LIMIT),
        name="moe_ffn",
    )(tile_expert, tile_group, tile_next, n_used, hs, w_gate, w_up, w_down)


def _combine_kernel(first_ref, ahead_ref, x_ref, w_ref, y_hbm, o_ref, buf, sem):
    i = pl.program_id(0)
    n = pl.num_programs(0)
    rows = 2 * TM_COMBINE

    @pl.when(i == 0)
    def _():
        _start_rows(y_hbm, first_ref, buf.at[0], sem.at[0], rows)

    @pl.when(i + 1 < n)
    def _():
        nxt = (i + 1) % 2
        for r in range(rows):
            _row_copy(y_hbm, ahead_ref, buf.at[nxt], sem.at[nxt], r).start(priority=r % 2)

    slot = i % 2
    _wait_rows(y_hbm, buf.at[slot], sem.at[slot], rows)
    w = w_ref[...]
    o_ref[...] = (x_ref[...] + w[:, 0:1] * buf[slot, :TM_COMBINE]
                  + w[:, 1:2] * buf[slot, TM_COMBINE:])


def _combine(x1, rw, y, slots3):
    t = x1.shape[0]
    n = t // TM_COMBINE
    row = lambda i: (i, 0)
    smem_tile = lambda index_map: pl.BlockSpec((1, 1, 2 * TM_COMBINE), index_map,
                                               memory_space=pltpu.SMEM)
    return pl.pallas_call(
        _combine_kernel,
        grid=(n,),
        in_specs=[
            smem_tile(lambda i: (0, 0, 0)),
            smem_tile(lambda i: (jnp.minimum(i + 1, n - 1), 0, 0)),
            pl.BlockSpec((TM_COMBINE, D_MODEL), row),
            pl.BlockSpec((TM_COMBINE, ROUTER_LANES), row),
            pl.BlockSpec(memory_space=pl.ANY),
        ],
        out_specs=pl.BlockSpec((TM_COMBINE, D_MODEL), row),
        out_shape=jax.ShapeDtypeStruct((t, D_MODEL), F32),
        scratch_shapes=[
            pltpu.VMEM((2, 2 * TM_COMBINE, D_MODEL), F32),
            pltpu.SemaphoreType.DMA((2,)),
        ],
        compiler_params=pltpu.CompilerParams(
            dimension_semantics=("arbitrary",), vmem_limit_bytes=VMEM_LIMIT),
        name="combine",
    )(slots3, slots3, x1, rw, y)


def _slot_tables(l, ri, counts_row, n_tiles):
    t = ri.shape[0]
    experts = jnp.arange(N_EXPERTS, dtype=I32)
    counts = counts_row[0, :N_EXPERTS]
    tiles_per = (counts + TM_MOE - 1) // TM_MOE
    tile_end = jnp.cumsum(tiles_per)
    start = (tile_end - tiles_per) * TM_MOE
    n_used = tile_end[-1:].astype(I32)
    nonempty = counts > 0
    group_of = jnp.cumsum(nonempty.astype(I32)) - 1
    later = (experts[None, :] > experts[:, None]) & nonempty[None, :]
    next_of = jnp.min(jnp.where(later, experts[None, :], N_EXPERTS), axis=1)
    next_of = jnp.where(next_of < N_EXPERTS, next_of + l * N_EXPERTS, -1)
    tile_idx = jnp.arange(n_tiles, dtype=I32)
    tile_e = jnp.minimum(jnp.sum((tile_end[None, :] <= tile_idx[:, None]).astype(I32), axis=1),
                         N_EXPERTS - 1)
    pick = tile_e[:, None] == experts[None, :]
    tile_group = jnp.sum(jnp.where(pick, group_of[None, :], 0), axis=1)
    tile_next = jnp.sum(jnp.where(pick, next_of[None, :], 0), axis=1)
    tile_expert = tile_e + l * N_EXPERTS
    eid = ri[:, 0:2]
    slot = ri[:, 2:4] + jnp.sum(jnp.where(eid[..., None] == experts, start, 0), axis=-1)
    slot_flat3 = slot.reshape((2 * t) // PLAN_CHUNK, 1, PLAN_CHUNK)
    slot2 = slot.reshape(t // TM_COMBINE, TM_COMBINE, 2)
    slots3 = jnp.swapaxes(slot2, 1, 2).reshape(t // TM_COMBINE, 1, 2 * TM_COMBINE)
    tiles = (tile_expert.astype(I32), tile_group.astype(I32), tile_next.astype(I32), n_used)
    return counts, start.astype(I32), tiles, slot_flat3, slots3


def _rope_freq_row():
    inv_freq = ROPE_THETA ** (-(jnp.arange(0, ROT_DIM, 2, dtype=F32) / ROT_DIM))
    return jnp.concatenate(
        [inv_freq, inv_freq, jnp.zeros((HEAD_DIM - ROT_DIM,), F32)]).reshape(1, HEAD_DIM)


def _router_operands(w_rg, b_rg, w_re, b_re):
    depth = w_rg.shape[0]
    pad = ROUTER_LANES - N_EXPERTS - N_GROUPS
    w = jnp.concatenate([w_re, w_rg, jnp.zeros((depth, D_MODEL, pad), F32)], axis=2)
    b = jnp.concatenate([b_re, b_rg, jnp.zeros((depth, pad), F32)], axis=1)
    hi = w.astype(BF16)
    lo = (w - hi.astype(F32)).astype(BF16)
    return hi, lo, b


def kernel(x, positions, norm_mix, w_in, w_pool, pool_scale, q_norm, k_norm, sink, branch_gain_pool,
           branch_gain_attn, w_out, norm_ffn, w_router_group, b_router_group, w_router_expert,
           b_router_expert, w_gate, w_up, w_down):
    b, s, d = x.shape
    t = b * s
    depth = norm_mix.shape[0]
    x2 = x.reshape(t, d)
    pos2 = positions.reshape(t, 1)
    freq = _rope_freq_row()
    w_in_bf = w_in.astype(BF16)
    w_out_bf = w_out.astype(BF16)
    w_pool_bf = w_pool.astype(BF16)
    r_hi, r_lo, r_b = _router_operands(w_router_group, b_router_group, w_router_expert,
                                       b_router_expert)
    wg = w_gate.reshape(depth * N_EXPERTS, D_MODEL, D_EXPERT)
    wu = w_up.reshape(depth * N_EXPERTS, D_MODEL, D_EXPERT)
    wd = w_down.reshape(depth * N_EXPERTS, D_EXPERT, D_MODEL)
    n_tiles = (2 * t) // TM_MOE + N_EXPERTS
    rows = lambda v: v.reshape(depth, 1, v.shape[-1])
    for l in range(depth):
        u, qkv = _in_proj(l, x2, pos2, rows(norm_mix), w_in_bf, rows(q_norm), rows(k_norm), freq)
        mixed = _mixer(l, u.reshape(b, s, POOL_WIDTH), qkv.reshape(b, s, QKV_WIDTH), sink, w_pool_bf,
                       rows(pool_scale), rows(branch_gain_pool), rows(branch_gain_attn))
        x1, h, ri, rw, counts_row = _out_proj(l, mixed.reshape(t, D_MODEL), x2, w_out_bf,
                                              rows(norm_ffn), r_hi, r_lo, rows(r_b))
        counts, start, tiles, slot_flat3, slots3 = _slot_tables(l, ri, counts_row, n_tiles)
        src = _plan(counts, start, slot_flat3, n_tiles * TM_MOE)
        hs = _sort_rows(h, tiles[3], src.reshape(n_tiles, 1, TM_MOE))
        y = _moe_ffn(hs, wg, wu, wd, *tiles)
        x2 = _combine(x1, rw, y, slots3)
    return x2.reshape(b, s, d)
```

```python
import functools

import jax
import jax.numpy as jnp
from jax import lax
from jax.experimental import pallas as pl
from jax.experimental.pallas import tpu as pltpu

F32 = jnp.float32
BF16 = jnp.bfloat16
I32 = jnp.int32

D_MODEL = 2048
POOL_WIDTH = 1024
POOL_WINDOWS = (2, 4, 8, 16)
POOL_GROUP = 256
HEAD_DIM = 128
N_Q_HEADS = 8
N_KV_HEADS = 2
Q_PER_KV = 4
ATTN_WIDTH = 1024
KV_WIDTH = 256
QKV_WIDTH = ATTN_WIDTH + 2 * KV_WIDTH
IN_WIDTH = POOL_WIDTH + QKV_WIDTH
WINDOW = 128
BLOCK = 128
ROPE_THETA = 500000.0
ROT_DIM = 32
ROT_HALF = ROT_DIM // 2
N_GROUPS = 4
EXPERTS_PER_GROUP = 8
N_EXPERTS = 32
D_EXPERT = 512
EPS = 1e-6

LANES = 128
POOL_HALO = 8
ROUTER_LANES = LANES
TM_PROJ = 256
TM_IN = 512
SUB_ROWS = 256
MIX_BLOCKS = 2
TM_MOE = 256
TM_COMBINE = 256
ISSUE_UNROLL = 16
PAD_ALIGN = 8
PAD_PIECES = (128, 64, 32, 16, 8)
WEIGHT_DMA_PRIORITY = 1
VMEM_LIMIT = 56 * 1024 * 1024


def _rms(x, gain):
    return x * lax.rsqrt(jnp.mean(x * x, axis=-1, keepdims=True) + EPS) * gain


def _in_proj_kernel(x_ref, pos_ref, g_ref, w_ref, qn_ref, kn_ref, freq_ref, u_ref, qkv_ref):
    for sub in range(TM_IN // SUB_ROWS):
        rows = slice(sub * SUB_ROWS, (sub + 1) * SUB_ROWS)
        h = _rms(x_ref[rows], g_ref[0]).astype(BF16)
        z = jnp.dot(h, w_ref[0], preferred_element_type=F32)
        u_ref[rows] = z[:, :POOL_WIDTH]

        ang = pos_ref[rows].astype(F32) * freq_ref[...]
        cos = jnp.cos(ang)
        sin = jnp.sin(ang)
        lane = lax.broadcasted_iota(I32, ang.shape, 1)
        sin_hi = jnp.where(lane >= ROT_HALF, sin, 0.0)
        sin_lo = jnp.where(lane < ROT_HALF, -sin, 0.0)

        def norm_rope(t, gain):
            y = _rms(t, gain)
            return (y * cos + pltpu.roll(y, ROT_HALF, 1) * sin_hi
                    + pltpu.roll(y, HEAD_DIM - ROT_HALF, 1) * sin_lo)

        for hd in range(N_Q_HEADS + N_KV_HEADS):
            gain = qn_ref[0] if hd < N_Q_HEADS else kn_ref[0]
            src = POOL_WIDTH + hd * HEAD_DIM
            dst = hd * HEAD_DIM
            qkv_ref[rows, dst:dst + HEAD_DIM] = norm_rope(z[:, src:src + HEAD_DIM], gain).astype(BF16)
        v0 = POOL_WIDTH + ATTN_WIDTH + KV_WIDTH
        qkv_ref[rows, ATTN_WIDTH + KV_WIDTH:] = z[:, v0:].astype(BF16)


def _in_proj(l, x2, pos2, gain, w_bf, qn, kn, freq):
    t = x2.shape[0]
    row = lambda i: (i, 0)
    layer = lambda i: (l, 0, 0)
    return pl.pallas_call(
        _in_proj_kernel,
        grid=(t // TM_IN,),
        in_specs=[
            pl.BlockSpec((TM_IN, D_MODEL), row),
            pl.BlockSpec((TM_IN, 1), row),
            pl.BlockSpec((1, 1, D_MODEL), layer),
            pl.BlockSpec((1, D_MODEL, IN_WIDTH), layer),
            pl.BlockSpec((1, 1, HEAD_DIM), layer),
            pl.BlockSpec((1, 1, HEAD_DIM), layer),
            pl.BlockSpec((1, HEAD_DIM), lambda i: (0, 0)),
        ],
        out_specs=[
            pl.BlockSpec((TM_IN, POOL_WIDTH), row),
            pl.BlockSpec((TM_IN, QKV_WIDTH), row),
        ],
        out_shape=[
            jax.ShapeDtypeStruct((t, POOL_WIDTH), F32),
            jax.ShapeDtypeStruct((t, QKV_WIDTH), BF16),
        ],
        compiler_params=pltpu.CompilerParams(
            dimension_semantics=("arbitrary",), vmem_limit_bytes=VMEM_LIMIT),
        name="in_proj",
    )(x2, pos2, gain, w_bf, qn, kn, freq)


def _mixer_kernel(sink_ref, u_ref, up_ref, un_ref, q_ref, kp_ref, kc_ref, kn_ref,
                  vp_ref, vc_ref, vn_ref, wp_ref, ps_ref, gp_ref, ga_ref, o_ref, *, seq, layer):
    n = pl.program_id(1)
    n_steps = pl.num_programs(1)
    step_rows = MIX_BLOCKS * BLOCK
    t0 = n * step_rows

    prev = jnp.where(n > 0, up_ref[0], 0.0)
    nxt = jnp.where(n < n_steps - 1, un_ref[0], 0.0)
    cur = u_ref[0]
    ext = jnp.concatenate([prev, cur, nxt], axis=0)
    rows = ext.shape[0]
    tpos = t0 + lax.broadcasted_iota(I32, (step_rows, POOL_GROUP), 0)
    pooled = []
    for gi, w in enumerate(POOL_WINDOWS):
        sl = slice(gi * POOL_GROUP, (gi + 1) * POOL_GROUP)
        acc = ext[:, sl]
        span = 1
        while span < w:
            acc = acc + pltpu.roll(acc, span, 0)
            span *= 2
        lead = w // 2 - 1
        if lead:
            acc = pltpu.roll(acc, rows - lead, 0)
        wsum = acc[POOL_HALO:POOL_HALO + step_rows]
        lo = jnp.clip(tpos - w // 2, 0, seq)
        hi = jnp.clip(tpos + w // 2, 0, seq)
        y = wsum / (hi - lo).astype(F32) - cur[:, sl]
        y = jnp.dot(y.astype(BF16), wp_ref[0, gi], preferred_element_type=F32)
        pooled.append(y)
    pool = jnp.concatenate(pooled, axis=1) * ps_ref[0]
    o_ref[0, :, :POOL_WIDTH] = _rms(pool, gp_ref[0]).astype(BF16)

    r = lax.broadcasted_iota(I32, (Q_PER_KV * BLOCK, 3 * BLOCK), 0) % BLOCK
    c = lax.broadcasted_iota(I32, (Q_PER_KV * BLOCK, 3 * BLOCK), 1)
    band = (c >= r) & (c <= r + 2 * WINDOW)
    kcat, vcat = [], []
    for kh in range(N_KV_HEADS):
        ks = slice(kh * HEAD_DIM, (kh + 1) * HEAD_DIM)
        kcat.append(jnp.concatenate([kp_ref[0, :, ks], kc_ref[0, :, ks], kn_ref[0, :, ks]], axis=0))
        vcat.append(jnp.concatenate([vp_ref[0, :, ks], vc_ref[0, :, ks], vn_ref[0, :, ks]], axis=0))
    for j in range(MIX_BLOCKS):
        kj = t0 + (j - 1) * BLOCK + c
        mask = band & (kj >= 0) & (kj < seq)
        qrows = slice(j * BLOCK, (j + 1) * BLOCK)
        krows = slice(j * BLOCK, (j + 3) * BLOCK)
        heads = []
        for kh in range(N_KV_HEADS):
            q = jnp.concatenate(
                [q_ref[0, qrows, (kh * Q_PER_KV + g) * HEAD_DIM:(kh * Q_PER_KV + g + 1) * HEAD_DIM]
                 for g in range(Q_PER_KV)], axis=0)
            s = lax.dot_general(q, kcat[kh][krows], (((1,), (1,)), ((), ())),
                                preferred_element_type=F32)
            s = jnp.where(mask, s * (HEAD_DIM ** -0.5), -jnp.inf)
            ps, dens = [], []
            for g in range(Q_PER_KV):
                sg = s[g * BLOCK:(g + 1) * BLOCK]
                sink = sink_ref[layer, kh * Q_PER_KV + g]
                m = jnp.maximum(jnp.max(sg, axis=-1, keepdims=True), sink)
                pg = jnp.exp(sg - m)
                dens.append(jnp.sum(pg, axis=-1, keepdims=True) + jnp.exp(sink - m))
                ps.append(pg.astype(BF16))
            o = jnp.dot(jnp.concatenate(ps, axis=0), vcat[kh][krows], preferred_element_type=F32)
            heads.extend(o[g * BLOCK:(g + 1) * BLOCK] / dens[g] for g in range(Q_PER_KV))
        attn = jnp.concatenate(heads, axis=1)
        o_ref[0, qrows, POOL_WIDTH:] = _rms(attn, ga_ref[0]).astype(BF16)


def _mixer(l, u3, qkv3, sink, wp_bf, pscale, gpool, gattn):
    b, s, _ = u3.shape
    nb = s // BLOCK
    step_rows = MIX_BLOCKS * BLOCK
    n_steps = s // step_rows
    halo_per_step = step_rows // POOL_HALO
    n_halo = s // POOL_HALO
    kcol = ATTN_WIDTH // KV_WIDTH
    vcol = kcol + 1
    layer = lambda bi, n: (l, 0, 0)
    prev_blk = lambda n: jnp.maximum(n * MIX_BLOCKS - 1, 0)
    next_blk = lambda n: jnp.minimum((n + 1) * MIX_BLOCKS, nb - 1)
    edge_spec = lambda blk, col: pl.BlockSpec((1, BLOCK, KV_WIDTH), lambda bi, n: (bi, blk(n), col))
    body_spec = lambda col: pl.BlockSpec((1, step_rows, KV_WIDTH), lambda bi, n: (bi, n, col))
    return pl.pallas_call(
        functools.partial(_mixer_kernel, seq=s, layer=l),
        grid=(b, n_steps),
        in_specs=[
            pl.BlockSpec(memory_space=pltpu.SMEM),
            pl.BlockSpec((1, step_rows, POOL_WIDTH), lambda bi, n: (bi, n, 0)),
            pl.BlockSpec((1, POOL_HALO, POOL_WIDTH),
                         lambda bi, n: (bi, jnp.maximum(n * halo_per_step - 1, 0), 0)),
            pl.BlockSpec((1, POOL_HALO, POOL_WIDTH),
                         lambda bi, n: (bi, jnp.minimum((n + 1) * halo_per_step, n_halo - 1), 0)),
            pl.BlockSpec((1, step_rows, ATTN_WIDTH), lambda bi, n: (bi, n, 0)),
            edge_spec(prev_blk, kcol), body_spec(kcol), edge_spec(next_blk, kcol),
            edge_spec(prev_blk, vcol), body_spec(vcol), edge_spec(next_blk, vcol),
            pl.BlockSpec((1, len(POOL_WINDOWS), POOL_GROUP, POOL_GROUP), lambda bi, n: (l, 0, 0, 0)),
            pl.BlockSpec((1, 1, POOL_WIDTH), layer),
            pl.BlockSpec((1, 1, POOL_WIDTH), layer),
            pl.BlockSpec((1, 1, ATTN_WIDTH), layer),
        ],
        out_specs=pl.BlockSpec((1, step_rows, POOL_WIDTH + ATTN_WIDTH), lambda bi, n: (bi, n, 0)),
        out_shape=jax.ShapeDtypeStruct((b, s, POOL_WIDTH + ATTN_WIDTH), BF16),
        compiler_params=pltpu.CompilerParams(
            dimension_semantics=("arbitrary", "arbitrary"), vmem_limit_bytes=VMEM_LIMIT),
        name="mixer",
    )(sink, u3, u3, u3, qkv3, qkv3, qkv3, qkv3, qkv3, qkv3, qkv3, wp_bf, pscale, gpool, gattn)


def _split_dot(a, b_hi, b_lo):
    a_hi = a.astype(BF16)
    a_lo = (a - a_hi.astype(F32)).astype(BF16)
    return (jnp.dot(a_hi, b_hi, preferred_element_type=F32)
            + jnp.dot(a_lo, b_hi, preferred_element_type=F32)
            + jnp.dot(a_hi, b_lo, preferred_element_type=F32))


def _out_proj_kernel(m_ref, x_ref, w_ref, g_ref, rh_ref, rl_ref, rb_ref,
                     x1_ref, h_ref, ri_ref, rw_ref, cnt_ref, run_ref):
    @pl.when(pl.program_id(0) == 0)
    def _():
        run_ref[...] = jnp.zeros_like(run_ref)

    x1 = x_ref[...] + jnp.dot(m_ref[...], w_ref[0], preferred_element_type=F32)
    x1_ref[...] = x1
    h = _rms(x1, g_ref[0])
    h_ref[...] = h
    logits = _split_dot(h, rh_ref[0], rl_ref[0]) + rb_ref[0]
    lane = lax.broadcasted_iota(I32, logits.shape, 1)
    far = jnp.int32(ROUTER_LANES)
    neg = -jnp.inf

    def top(vals):
        best = jnp.max(vals, axis=-1, keepdims=True)
        idx = jnp.min(jnp.where(vals == best, lane, far), axis=-1, keepdims=True)
        return best, idx

    gl = jnp.where((lane >= N_EXPERTS) & (lane < N_EXPERTS + N_GROUPS), logits, neg)
    gmax, gidx = top(gl)
    g_w = 1.0 / jnp.sum(jnp.exp(gl - gmax), axis=-1, keepdims=True)
    in_group = (lane < N_EXPERTS) & ((lane >> 3) == gidx - N_EXPERTS)
    el = jnp.where(in_group, logits, neg)
    m1, i1 = top(el)
    m2, i2 = top(jnp.where(lane == i1, neg, el))
    esum = jnp.sum(jnp.exp(el - m1), axis=-1, keepdims=True)
    p1 = 1.0 / esum
    p2 = jnp.exp(m2 - m1) / esum
    w1 = g_w * (p1 / (p1 + p2))
    w2 = g_w * (p2 / (p1 + p2))

    pick1 = lane == i1
    pick2 = lane == i2
    picks = (pick1 | pick2).astype(BF16)
    tm = logits.shape[0]
    lower = (lax.broadcasted_iota(I32, (tm, tm), 1) < lax.broadcasted_iota(I32, (tm, tm), 0)).astype(BF16)
    before = jnp.dot(lower, picks, preferred_element_type=F32) + run_ref[...]
    r1 = jnp.sum(jnp.where(pick1, before, 0.0), axis=-1, keepdims=True).astype(I32)
    r2 = jnp.sum(jnp.where(pick2, before, 0.0), axis=-1, keepdims=True).astype(I32)
    total = run_ref[...] + jnp.sum(picks.astype(F32), axis=0, keepdims=True)
    run_ref[...] = total
    cnt_ref[...] = total.astype(I32)

    ri_ref[...] = jnp.where(lane == 0, i1, jnp.where(lane == 1, i2,
                            jnp.where(lane == 2, r1, jnp.where(lane == 3, r2, 0))))
    rw_ref[...] = jnp.where(lane == 0, w1, jnp.where(lane == 1, w2, 0.0))


def _out_proj(l, mixed2, x2, w_bf, gain, r_hi, r_lo, r_b):
    t = x2.shape[0]
    row = lambda i: (i, 0)
    layer = lambda i: (l, 0, 0)
    return pl.pallas_call(
        _out_proj_kernel,
        grid=(t // TM_PROJ,),
        in_specs=[
            pl.BlockSpec((TM_PROJ, D_MODEL), row),
            pl.BlockSpec((TM_PROJ, D_MODEL), row),
            pl.BlockSpec((1, D_MODEL, D_MODEL), layer),
            pl.BlockSpec((1, 1, D_MODEL), layer),
            pl.BlockSpec((1, D_MODEL, ROUTER_LANES), layer),
            pl.BlockSpec((1, D_MODEL, ROUTER_LANES), layer),
            pl.BlockSpec((1, 1, ROUTER_LANES), layer),
        ],
        out_specs=[
            pl.BlockSpec((TM_PROJ, D_MODEL), row),
            pl.BlockSpec((TM_PROJ, D_MODEL), row),
            pl.BlockSpec((TM_PROJ, ROUTER_LANES), row),
            pl.BlockSpec((TM_PROJ, ROUTER_LANES), row),
            pl.BlockSpec((1, ROUTER_LANES), lambda i: (0, 0)),
        ],
        out_shape=[
            jax.ShapeDtypeStruct((t, D_MODEL), F32),
            jax.ShapeDtypeStruct((t, D_MODEL), F32),
            jax.ShapeDtypeStruct((t, ROUTER_LANES), I32),
            jax.ShapeDtypeStruct((t, ROUTER_LANES), F32),
            jax.ShapeDtypeStruct((1, ROUTER_LANES), I32),
        ],
        scratch_shapes=[pltpu.VMEM((1, ROUTER_LANES), F32)],
        compiler_params=pltpu.CompilerParams(
            dimension_semantics=("arbitrary",), vmem_limit_bytes=VMEM_LIMIT),
        name="out_proj",
    )(mixed2, x2, w_bf, gain, r_hi, r_lo, r_b)


def _row_copy(src_hbm, idx_ref, dst, sem, r):
    return pltpu.make_async_copy(src_hbm.at[pl.ds(idx_ref[0, 0, r], 1)], dst.at[pl.ds(r, 1)], sem)


def _start_rows(src_hbm, idx_ref, dst, sem, n_rows):
    def body(j, carry):
        for k in range(ISSUE_UNROLL):
            _row_copy(src_hbm, idx_ref, dst, sem, j * ISSUE_UNROLL + k).start()
        return carry
    lax.fori_loop(0, n_rows // ISSUE_UNROLL, body, 0)


def _wait_rows(src_hbm, dst, sem, n_rows):
    pltpu.make_async_copy(src_hbm.at[pl.ds(0, n_rows)], dst, sem).wait()


def _dispatch_kernel(pad_lo_ref, pad_len_ref, slots_ref, h_ref, hs_hbm, stage, zeros, sem, pad_sem):
    i = pl.program_id(0)
    n = pl.num_programs(0)
    s = i % 2
    pad_rows = PAD_ALIGN - 1 + sum(PAD_PIECES)
    n_slots = hs_hbm.shape[0] - N_EXPERTS * TM_MOE

    def wait_stage(which):
        for _ in range(2):
            pltpu.make_async_copy(stage.at[which], hs_hbm.at[pl.ds(0, TM_COMBINE)], sem.at[which]).wait()

    @pl.when(i == 0)
    def _():
        zeros[...] = jnp.zeros_like(zeros)

        def per_expert(e, carry):
            lo = pad_lo_ref[e]
            length = pad_len_ref[e]
            head = (-lo) & (PAD_ALIGN - 1)
            spare = n_slots + e * TM_MOE
            for j in range(PAD_ALIGN - 1):
                dst = jnp.where(j < head, lo + j, spare + j)
                pltpu.make_async_copy(zeros.at[pl.ds(0, 1)], hs_hbm.at[pl.ds(dst, 1)], pad_sem).start()
            inside = lo + head
            outside = spare + PAD_ALIGN
            for p in PAD_PIECES:
                use = ((length - head) & p) != 0
                dst = pl.multiple_of(jnp.where(use, inside, outside), PAD_ALIGN)
                pltpu.make_async_copy(zeros.at[pl.ds(0, p)], hs_hbm.at[pl.ds(dst, p)], pad_sem).start()
                inside = inside + jnp.where(use, p, 0)
                outside = outside + p
            return carry
        lax.fori_loop(0, N_EXPERTS, per_expert, 0)

    @pl.when(i >= 2)
    def _():
        wait_stage(s)

    stage[s] = h_ref[...]
    for r in range(2 * TM_COMBINE):
        pltpu.make_async_copy(stage.at[s, pl.ds(r % TM_COMBINE, 1)],
                              hs_hbm.at[pl.ds(slots_ref[0, 0, r], 1)], sem.at[s]).start()

    @pl.when(i == n - 1)
    def _():
        wait_stage(s)
        wait_stage(1 - s)
        pltpu.make_async_copy(hs_hbm.at[pl.ds(0, N_EXPERTS * pad_rows)],
                              hs_hbm.at[pl.ds(0, N_EXPERTS * pad_rows)], pad_sem).wait()


def _dispatch(h2, pad_lo, pad_len, slots3, n_slots):
    t = h2.shape[0]
    n = t // TM_COMBINE
    return pl.pallas_call(
        _dispatch_kernel,
        grid_spec=pltpu.PrefetchScalarGridSpec(
            num_scalar_prefetch=2,
            grid=(n,),
            in_specs=[
                pl.BlockSpec((1, 1, 2 * TM_COMBINE), lambda i, lo, ln: (i, 0, 0), memory_space=pltpu.SMEM),
                pl.BlockSpec((TM_COMBINE, D_MODEL), lambda i, lo, ln: (i, 0)),
            ],
            out_specs=pl.BlockSpec(memory_space=pl.ANY),
            scratch_shapes=[
                pltpu.VMEM((2, TM_COMBINE, D_MODEL), F32),
                pltpu.VMEM((PAD_PIECES[0], D_MODEL), F32),
                pltpu.SemaphoreType.DMA((2,)),
                pltpu.SemaphoreType.DMA(()),
            ],
        ),
        out_shape=jax.ShapeDtypeStruct((n_slots + N_EXPERTS * TM_MOE, D_MODEL), F32),
        compiler_params=pltpu.CompilerParams(
            dimension_semantics=("arbitrary",), vmem_limit_bytes=VMEM_LIMIT),
        name="dispatch",
    )(pad_lo, pad_len, slots3, h2)


def _moe_kernel(te_ref, grp_ref, nxt_ref, nu_ref, x_ref, wg_hbm, wu_hbm, wd_hbm,
                y_ref, wg_st, wu_st, wd_st, wsem, wg_bf, wu_bf, wd_bf):
    i = pl.program_id(0)
    n_used = nu_ref[0]

    def weight_copies(e, par):
        return (pltpu.make_async_copy(wg_hbm.at[e], wg_st.at[par], wsem.at[par]),
                pltpu.make_async_copy(wu_hbm.at[e], wu_st.at[par], wsem.at[par]),
                pltpu.make_async_copy(wd_hbm.at[e], wd_st.at[par], wsem.at[par]))

    @pl.when(i == 0)
    def _():
        for c in weight_copies(te_ref[0], 0):
            c.start(priority=WEIGHT_DMA_PRIORITY)

    @pl.when(i < n_used)
    def _():
        @pl.when((i == 0) | (te_ref[i] != te_ref[jnp.maximum(i - 1, 0)]))
        def _():
            par = grp_ref[i] % 2
            for c in weight_copies(te_ref[i], par):
                c.wait()

            @pl.when(nxt_ref[i] >= 0)
            def _():
                for c in weight_copies(nxt_ref[i], 1 - par):
                    c.start(priority=WEIGHT_DMA_PRIORITY)

            wg_bf[...] = wg_st[par].astype(BF16)
            wu_bf[...] = wu_st[par].astype(BF16)
            wd_bf[...] = wd_st[par].astype(BF16)

        xb = x_ref[...].astype(BF16)
        g = jnp.dot(xb, wg_bf[...], preferred_element_type=F32)
        u = jnp.dot(xb, wu_bf[...], preferred_element_type=F32)
        a = (g * jax.nn.sigmoid(g) * u).astype(BF16)
        y_ref[...] = jnp.dot(a, wd_bf[...], preferred_element_type=F32)

    @pl.when(i >= n_used)
    def _():
        y_ref[...] = jnp.zeros_like(y_ref)


def _moe_ffn(hs, n_tiles, w_gate, w_up, w_down, tile_expert, tile_group, tile_next, n_used):
    hbm = pl.BlockSpec(memory_space=pl.ANY)
    tile = lambda i, te, gr, nx, nu: (i, 0)
    return pl.pallas_call(
        _moe_kernel,
        grid_spec=pltpu.PrefetchScalarGridSpec(
            num_scalar_prefetch=4,
            grid=(n_tiles,),
            in_specs=[pl.BlockSpec((TM_MOE, D_MODEL), tile), hbm, hbm, hbm],
            out_specs=pl.BlockSpec((TM_MOE, D_MODEL), tile),
            scratch_shapes=[
                pltpu.VMEM((2, D_MODEL, D_EXPERT), F32),
                pltpu.VMEM((2, D_MODEL, D_EXPERT), F32),
                pltpu.VMEM((2, D_EXPERT, D_MODEL), F32),
                pltpu.SemaphoreType.DMA((2,)),
                pltpu.VMEM((D_MODEL, D_EXPERT), BF16),
                pltpu.VMEM((D_MODEL, D_EXPERT), BF16),
                pltpu.VMEM((D_EXPERT, D_MODEL), BF16),
            ],
        ),
        out_shape=jax.ShapeDtypeStruct((n_tiles * TM_MOE, D_MODEL), F32),
        compiler_params=pltpu.CompilerParams(
            dimension_semantics=("arbitrary",), vmem_limit_bytes=VMEM_LIMIT),
        name="moe_ffn",
    )(tile_expert, tile_group, tile_next, n_used, hs, w_gate, w_up, w_down)


def _combine_kernel(first_ref, ahead_ref, x_ref, w_ref, y_hbm, o_ref, buf, sem):
    i = pl.program_id(0)
    n = pl.num_programs(0)
    rows = 2 * TM_COMBINE

    @pl.when(i == 0)
    def _():
        _start_rows(y_hbm, first_ref, buf.at[0], sem.at[0], rows)

    @pl.when(i + 1 < n)
    def _():
        nxt = (i + 1) % 2
        for r in range(rows):
            _row_copy(y_hbm, ahead_ref, buf.at[nxt], sem.at[nxt], r).start(priority=r % 2)

    slot = i % 2
    _wait_rows(y_hbm, buf.at[slot], sem.at[slot], rows)
    w = w_ref[...]
    o_ref[...] = (x_ref[...] + w[:, 0:1] * buf[slot, :TM_COMBINE]
                  + w[:, 1:2] * buf[slot, TM_COMBINE:])


def _combine(x1, rw, y, slots3):
    t = x1.shape[0]
    n = t // TM_COMBINE
    row = lambda i: (i, 0)
    smem_tile = lambda index_map: pl.BlockSpec((1, 1, 2 * TM_COMBINE), index_map,
                                               memory_space=pltpu.SMEM)
    return pl.pallas_call(
        _combine_kernel,
        grid=(n,),
        in_specs=[
            smem_tile(lambda i: (0, 0, 0)),
            smem_tile(lambda i: (jnp.minimum(i + 1, n - 1), 0, 0)),
            pl.BlockSpec((TM_COMBINE, D_MODEL), row),
            pl.BlockSpec((TM_COMBINE, ROUTER_LANES), row),
            pl.BlockSpec(memory_space=pl.ANY),
        ],
        out_specs=pl.BlockSpec((TM_COMBINE, D_MODEL), row),
        out_shape=jax.ShapeDtypeStruct((t, D_MODEL), F32),
        scratch_shapes=[
            pltpu.VMEM((2, 2 * TM_COMBINE, D_MODEL), F32),
            pltpu.SemaphoreType.DMA((2,)),
        ],
        compiler_params=pltpu.CompilerParams(
            dimension_semantics=("arbitrary",), vmem_limit_bytes=VMEM_LIMIT),
        name="combine",
    )(slots3, slots3, x1, rw, y)


def _slot_tables(l, ri, counts_row, n_tiles):
    t = ri.shape[0]
    experts = jnp.arange(N_EXPERTS, dtype=I32)
    counts = counts_row[0, :N_EXPERTS]
    tiles_per = (counts + TM_MOE - 1) // TM_MOE
    tile_end = jnp.cumsum(tiles_per)
    start = (tile_end - tiles_per) * TM_MOE
    n_used = tile_end[-1:].astype(I32)
    nonempty = counts > 0
    group_of = jnp.cumsum(nonempty.astype(I32)) - 1
    later = (experts[None, :] > experts[:, None]) & nonempty[None, :]
    next_of = jnp.min(jnp.where(later, experts[None, :], N_EXPERTS), axis=1)
    next_of = jnp.where(next_of < N_EXPERTS, next_of + l * N_EXPERTS, -1)
    tile_idx = jnp.arange(n_tiles, dtype=I32)
    tile_e = jnp.minimum(jnp.sum((tile_end[None, :] <= tile_idx[:, None]).astype(I32), axis=1),
                         N_EXPERTS - 1)
    pick = tile_e[:, None] == experts[None, :]
    tile_group = jnp.sum(jnp.where(pick, group_of[None, :], 0), axis=1)
    tile_next = jnp.sum(jnp.where(pick, next_of[None, :], 0), axis=1)
    tile_expert = tile_e + l * N_EXPERTS
    eid = ri[:, 0:2]
    slot = ri[:, 2:4] + jnp.sum(jnp.where(eid[..., None] == experts, start, 0), axis=-1)
    slot2 = slot.reshape(t // TM_COMBINE, TM_COMBINE, 2)
    slots3 = jnp.swapaxes(slot2, 1, 2).reshape(t // TM_COMBINE, 1, 2 * TM_COMBINE)
    tiles = (tile_expert.astype(I32), tile_group.astype(I32), tile_next.astype(I32), n_used)
    pad_lo = (start + counts).astype(I32)
    pad_len = (tiles_per * TM_MOE - counts).astype(I32)
    return pad_lo, pad_len, tiles, slots3


def _rope_freq_row():
    inv_freq = ROPE_THETA ** (-(jnp.arange(0, ROT_DIM, 2, dtype=F32) / ROT_DIM))
    return jnp.concatenate(
        [inv_freq, inv_freq, jnp.zeros((HEAD_DIM - ROT_DIM,), F32)]).reshape(1, HEAD_DIM)


def _router_operands(w_rg, b_rg, w_re, b_re):
    depth = w_rg.shape[0]
    pad = ROUTER_LANES - N_EXPERTS - N_GROUPS
    w = jnp.concatenate([w_re, w_rg, jnp.zeros((depth, D_MODEL, pad), F32)], axis=2)
    b = jnp.concatenate([b_re, b_rg, jnp.zeros((depth, pad), F32)], axis=1)
    hi = w.astype(BF16)
    lo = (w - hi.astype(F32)).astype(BF16)
    return hi, lo, b


def kernel(x, positions, norm_mix, w_in, w_pool, pool_scale, q_norm, k_norm, sink, branch_gain_pool,
           branch_gain_attn, w_out, norm_ffn, w_router_group, b_router_group, w_router_expert,
           b_router_expert, w_gate, w_up, w_down):
    b, s, d = x.shape
    t = b * s
    depth = norm_mix.shape[0]
    x2 = x.reshape(t, d)
    pos2 = positions.reshape(t, 1)
    freq = _rope_freq_row()
    w_in_bf = w_in.astype(BF16)
    w_out_bf = w_out.astype(BF16)
    w_pool_bf = w_pool.astype(BF16)
    r_hi, r_lo, r_b = _router_operands(w_router_group, b_router_group, w_router_expert,
                                       b_router_expert)
    wg = w_gate.reshape(depth * N_EXPERTS, D_MODEL, D_EXPERT)
    wu = w_up.reshape(depth * N_EXPERTS, D_MODEL, D_EXPERT)
    wd = w_down.reshape(depth * N_EXPERTS, D_EXPERT, D_MODEL)
    n_tiles = (2 * t) // TM_MOE + N_EXPERTS
    rows = lambda v: v.reshape(depth, 1, v.shape[-1])
    for l in range(depth):
        u, qkv = _in_proj(l, x2, pos2, rows(norm_mix), w_in_bf, rows(q_norm), rows(k_norm), freq)
        mixed = _mixer(l, u.reshape(b, s, POOL_WIDTH), qkv.reshape(b, s, QKV_WIDTH), sink, w_pool_bf,
                       rows(pool_scale), rows(branch_gain_pool), rows(branch_gain_attn))
        x1, h, ri, rw, counts_row = _out_proj(l, mixed.reshape(t, D_MODEL), x2, w_out_bf,
                                              rows(norm_ffn), r_hi, r_lo, rows(r_b))
        pad_lo, pad_len, tiles, slots3 = _slot_tables(l, ri, counts_row, n_tiles)
        hs = _dispatch(h, pad_lo, pad_len, slots3, n_tiles * TM_MOE)
        y = _moe_ffn(hs, n_tiles, wg, wu, wd, *tiles)
        x2 = _combine(x1, rw, y, slots3)
    return x2.reshape(b, s, d)
```

```python
import functools

import jax
import jax.numpy as jnp
from jax import lax
from jax.experimental import pallas as pl
from jax.experimental.pallas import tpu as pltpu

F32 = jnp.float32
BF16 = jnp.bfloat16
I32 = jnp.int32

D_MODEL = 2048
POOL_WIDTH = 1024
POOL_WINDOWS = (2, 4, 8, 16)
POOL_GROUP = 256
HEAD_DIM = 128
N_Q_HEADS = 8
N_KV_HEADS = 2
Q_PER_KV = 4
ATTN_WIDTH = 1024
KV_WIDTH = 256
QKV_WIDTH = ATTN_WIDTH + 2 * KV_WIDTH
IN_WIDTH = POOL_WIDTH + QKV_WIDTH
WINDOW = 128
BLOCK = 128
ROPE_THETA = 500000.0
ROT_DIM = 32
ROT_HALF = ROT_DIM // 2
N_GROUPS = 4
EXPERTS_PER_GROUP = 8
N_EXPERTS = 32
D_EXPERT = 512
EPS = 1e-6

LANES = 128
POOL_HALO = 8
ROUTER_LANES = LANES
TM_PROJ = 256
TM_IN = 512
SUB_ROWS = 256
MIX_BLOCKS = 2
TM_MOE = 256
TM_COMBINE = 256
ISSUE_UNROLL = 16
PAD_ALIGN = 8
PAD_PIECES = (128, 64, 32, 16, 8)
WEIGHT_DMA_PRIORITY = 1
VMEM_LIMIT = 56 * 1024 * 1024


def _rms(x, gain):
    return x * lax.rsqrt(jnp.mean(x * x, axis=-1, keepdims=True) + EPS) * gain


def _in_proj_kernel(x_ref, pos_ref, g_ref, w_ref, qn_ref, kn_ref, freq_ref, u_ref, qkv_ref):
    for sub in range(TM_IN // SUB_ROWS):
        rows = slice(sub * SUB_ROWS, (sub + 1) * SUB_ROWS)
        h = _rms(x_ref[rows], g_ref[0]).astype(BF16)
        z = jnp.dot(h, w_ref[0], preferred_element_type=F32)
        u_ref[rows] = z[:, :POOL_WIDTH]

        ang = pos_ref[rows].astype(F32) * freq_ref[...]
        cos = jnp.cos(ang)
        sin = jnp.sin(ang)
        lane = lax.broadcasted_iota(I32, ang.shape, 1)
        sin_hi = jnp.where(lane >= ROT_HALF, sin, 0.0)
        sin_lo = jnp.where(lane < ROT_HALF, -sin, 0.0)

        def norm_rope(t, gain):
            y = _rms(t, gain)
            return (y * cos + pltpu.roll(y, ROT_HALF, 1) * sin_hi
                    + pltpu.roll(y, HEAD_DIM - ROT_HALF, 1) * sin_lo)

        for hd in range(N_Q_HEADS + N_KV_HEADS):
            gain = qn_ref[0] if hd < N_Q_HEADS else kn_ref[0]
            src = POOL_WIDTH + hd * HEAD_DIM
            dst = hd * HEAD_DIM
            qkv_ref[rows, dst:dst + HEAD_DIM] = norm_rope(z[:, src:src + HEAD_DIM], gain).astype(BF16)
        v0 = POOL_WIDTH + ATTN_WIDTH + KV_WIDTH
        qkv_ref[rows, ATTN_WIDTH + KV_WIDTH:] = z[:, v0:].astype(BF16)


def _in_proj(l, x2, pos2, gain, w_bf, qn, kn, freq):
    t = x2.shape[0]
    row = lambda i: (i, 0)
    layer = lambda i: (l, 0, 0)
    return pl.pallas_call(
        _in_proj_kernel,
        grid=(t // TM_IN,),
        in_specs=[
            pl.BlockSpec((TM_IN, D_MODEL), row),
            pl.BlockSpec((TM_IN, 1), row),
            pl.BlockSpec((1, 1, D_MODEL), layer),
            pl.BlockSpec((1, D_MODEL, IN_WIDTH), layer),
            pl.BlockSpec((1, 1, HEAD_DIM), layer),
            pl.BlockSpec((1, 1, HEAD_DIM), layer),
            pl.BlockSpec((1, HEAD_DIM), lambda i: (0, 0)),
        ],
        out_specs=[
            pl.BlockSpec((TM_IN, POOL_WIDTH), row),
            pl.BlockSpec((TM_IN, QKV_WIDTH), row),
        ],
        out_shape=[
            jax.ShapeDtypeStruct((t, POOL_WIDTH), F32),
            jax.ShapeDtypeStruct((t, QKV_WIDTH), BF16),
        ],
        compiler_params=pltpu.CompilerParams(
            dimension_semantics=("arbitrary",), vmem_limit_bytes=VMEM_LIMIT),
        name="in_proj",
    )(x2, pos2, gain, w_bf, qn, kn, freq)


def _mixer_kernel(sink_ref, u_ref, up_ref, un_ref, q_ref, kp_ref, kc_ref, kn_ref,
                  vp_ref, vc_ref, vn_ref, wp_ref, ps_ref, gp_ref, ga_ref, o_ref, *, seq, layer):
    n = pl.program_id(1)
    n_steps = pl.num_programs(1)
    step_rows = MIX_BLOCKS * BLOCK
    t0 = n * step_rows

    prev = jnp.where(n > 0, up_ref[0], 0.0)
    nxt = jnp.where(n < n_steps - 1, un_ref[0], 0.0)
    cur = u_ref[0]
    ext = jnp.concatenate([prev, cur, nxt], axis=0)
    rows = ext.shape[0]
    tpos = t0 + lax.broadcasted_iota(I32, (step_rows, POOL_GROUP), 0)
    pooled = []
    for gi, w in enumerate(POOL_WINDOWS):
        sl = slice(gi * POOL_GROUP, (gi + 1) * POOL_GROUP)
        acc = ext[:, sl]
        span = 1
        while span < w:
            acc = acc + pltpu.roll(acc, span, 0)
            span *= 2
        lead = w // 2 - 1
        if lead:
            acc = pltpu.roll(acc, rows - lead, 0)
        wsum = acc[POOL_HALO:POOL_HALO + step_rows]
        lo = jnp.clip(tpos - w // 2, 0, seq)
        hi = jnp.clip(tpos + w // 2, 0, seq)
        y = wsum / (hi - lo).astype(F32) - cur[:, sl]
        y = jnp.dot(y.astype(BF16), wp_ref[0, gi], preferred_element_type=F32)
        pooled.append(y)
    pool = jnp.concatenate(pooled, axis=1) * ps_ref[0]
    o_ref[0, :, :POOL_WIDTH] = _rms(pool, gp_ref[0]).astype(BF16)

    r = lax.broadcasted_iota(I32, (Q_PER_KV * BLOCK, 3 * BLOCK), 0) % BLOCK
    c = lax.broadcasted_iota(I32, (Q_PER_KV * BLOCK, 3 * BLOCK), 1)
    band = (c >= r) & (c <= r + 2 * WINDOW)
    kcat, vcat = [], []
    for kh in range(N_KV_HEADS):
        ks = slice(kh * HEAD_DIM, (kh + 1) * HEAD_DIM)
        kcat.append(jnp.concatenate([kp_ref[0, :, ks], kc_ref[0, :, ks], kn_ref[0, :, ks]], axis=0))
        vcat.append(jnp.concatenate([vp_ref[0, :, ks], vc_ref[0, :, ks], vn_ref[0, :, ks]], axis=0))
    for j in range(MIX_BLOCKS):
        kj = t0 + (j - 1) * BLOCK + c
        mask = band & (kj >= 0) & (kj < seq)
        qrows = slice(j * BLOCK, (j + 1) * BLOCK)
        krows = slice(j * BLOCK, (j + 3) * BLOCK)
        heads = []
        for kh in range(N_KV_HEADS):
            q = jnp.concatenate(
                [q_ref[0, qrows, (kh * Q_PER_KV + g) * HEAD_DIM:(kh * Q_PER_KV + g + 1) * HEAD_DIM]
                 for g in range(Q_PER_KV)], axis=0)
            s = lax.dot_general(q, kcat[kh][krows], (((1,), (1,)), ((), ())),
                                preferred_element_type=F32)
            s = jnp.where(mask, s * (HEAD_DIM ** -0.5), -jnp.inf)
            ps, dens = [], []
            for g in range(Q_PER_KV):
                sg = s[g * BLOCK:(g + 1) * BLOCK]
                sink = sink_ref[layer, kh * Q_PER_KV + g]
                m = jnp.maximum(jnp.max(sg, axis=-1, keepdims=True), sink)
                pg = jnp.exp(sg - m)
                dens.append(jnp.sum(pg, axis=-1, keepdims=True) + jnp.exp(sink - m))
                ps.append(pg.astype(BF16))
            o = jnp.dot(jnp.concatenate(ps, axis=0), vcat[kh][krows], preferred_element_type=F32)
            heads.extend(o[g * BLOCK:(g + 1) * BLOCK] / dens[g] for g in range(Q_PER_KV))
        attn = jnp.concatenate(heads, axis=1)
        o_ref[0, qrows, POOL_WIDTH:] = _rms(attn, ga_ref[0]).astype(BF16)


def _mixer(l, u3, qkv3, sink, wp_bf, pscale, gpool, gattn):
    b, s, _ = u3.shape
    nb = s // BLOCK
    step_rows = MIX_BLOCKS * BLOCK
    n_steps = s // step_rows
    halo_per_step = step_rows // POOL_HALO
    n_halo = s // POOL_HALO
    kcol = ATTN_WIDTH // KV_WIDTH
    vcol = kcol + 1
    layer = lambda bi, n: (l, 0, 0)
    prev_blk = lambda n: jnp.maximum(n * MIX_BLOCKS - 1, 0)
    next_blk = lambda n: jnp.minimum((n + 1) * MIX_BLOCKS, nb - 1)
    edge_spec = lambda blk, col: pl.BlockSpec((1, BLOCK, KV_WIDTH), lambda bi, n: (bi, blk(n), col))
    body_spec = lambda col: pl.BlockSpec((1, step_rows, KV_WIDTH), lambda bi, n: (bi, n, col))
    return pl.pallas_call(
        functools.partial(_mixer_kernel, seq=s, layer=l),
        grid=(b, n_steps),
        in_specs=[
            pl.BlockSpec(memory_space=pltpu.SMEM),
            pl.BlockSpec((1, step_rows, POOL_WIDTH), lambda bi, n: (bi, n, 0)),
            pl.BlockSpec((1, POOL_HALO, POOL_WIDTH),
                         lambda bi, n: (bi, jnp.maximum(n * halo_per_step - 1, 0), 0)),
            pl.BlockSpec((1, POOL_HALO, POOL_WIDTH),
                         lambda bi, n: (bi, jnp.minimum((n + 1) * halo_per_step, n_halo - 1), 0)),
            pl.BlockSpec((1, step_rows, ATTN_WIDTH), lambda bi, n: (bi, n, 0)),
            edge_spec(prev_blk, kcol), body_spec(kcol), edge_spec(next_blk, kcol),
            edge_spec(prev_blk, vcol), body_spec(vcol), edge_spec(next_blk, vcol),
            pl.BlockSpec((1, len(POOL_WINDOWS), POOL_GROUP, POOL_GROUP), lambda bi, n: (l, 0, 0, 0)),
            pl.BlockSpec((1, 1, POOL_WIDTH), layer),
            pl.BlockSpec((1, 1, POOL_WIDTH), layer),
            pl.BlockSpec((1, 1, ATTN_WIDTH), layer),
        ],
        out_specs=pl.BlockSpec((1, step_rows, POOL_WIDTH + ATTN_WIDTH), lambda bi, n: (bi, n, 0)),
        out_shape=jax.ShapeDtypeStruct((b, s, POOL_WIDTH + ATTN_WIDTH), BF16),
        compiler_params=pltpu.CompilerParams(
            dimension_semantics=("arbitrary", "arbitrary"), vmem_limit_bytes=VMEM_LIMIT),
        name="mixer",
    )(sink, u3, u3, u3, qkv3, qkv3, qkv3, qkv3, qkv3, qkv3, qkv3, wp_bf, pscale, gpool, gattn)


def _split_dot(a, b_hi, b_lo):
    a_hi = a.astype(BF16)
    a_lo = (a - a_hi.astype(F32)).astype(BF16)
    return (jnp.dot(a_hi, b_hi, preferred_element_type=F32)
            + jnp.dot(a_lo, b_hi, preferred_element_type=F32)
            + jnp.dot(a_hi, b_lo, preferred_element_type=F32))


def _out_proj_kernel(m_ref, x_ref, w_ref, g_ref, rh_ref, rl_ref, rb_ref,
                     x1_ref, h_ref, ri_ref, rw_ref, cnt_ref, run_ref):
    @pl.when(pl.program_id(0) == 0)
    def _():
        run_ref[...] = jnp.zeros_like(run_ref)

    x1 = x_ref[...] + jnp.dot(m_ref[...], w_ref[0], preferred_element_type=F32)
    x1_ref[...] = x1
    h = _rms(x1, g_ref[0])
    h_ref[...] = h
    logits = _split_dot(h, rh_ref[0], rl_ref[0]) + rb_ref[0]
    lane = lax.broadcasted_iota(I32, logits.shape, 1)
    far = jnp.int32(ROUTER_LANES)
    neg = -jnp.inf

    def top(vals):
        best = jnp.max(vals, axis=-1, keepdims=True)
        idx = jnp.min(jnp.where(vals == best, lane, far), axis=-1, keepdims=True)
        return best, idx

    gl = jnp.where((lane >= N_EXPERTS) & (lane < N_EXPERTS + N_GROUPS), logits, neg)
    gmax, gidx = top(gl)
    g_w = 1.0 / jnp.sum(jnp.exp(gl - gmax), axis=-1, keepdims=True)
    in_group = (lane < N_EXPERTS) & ((lane >> 3) == gidx - N_EXPERTS)
    el = jnp.where(in_group, logits, neg)
    m1, i1 = top(el)
    m2, i2 = top(jnp.where(lane == i1, neg, el))
    esum = jnp.sum(jnp.exp(el - m1), axis=-1, keepdims=True)
    p1 = 1.0 / esum
    p2 = jnp.exp(m2 - m1) / esum
    w1 = g_w * (p1 / (p1 + p2))
    w2 = g_w * (p2 / (p1 + p2))

    pick1 = lane == i1
    pick2 = lane == i2
    picks = (pick1 | pick2).astype(BF16)
    tm = logits.shape[0]
    lower = (lax.broadcasted_iota(I32, (tm, tm), 1) < lax.broadcasted_iota(I32, (tm, tm), 0)).astype(BF16)
    before = jnp.dot(lower, picks, preferred_element_type=F32) + run_ref[...]
    r1 = jnp.sum(jnp.where(pick1, before, 0.0), axis=-1, keepdims=True).astype(I32)
    r2 = jnp.sum(jnp.where(pick2, before, 0.0), axis=-1, keepdims=True).astype(I32)
    total = run_ref[...] + jnp.sum(picks.astype(F32), axis=0, keepdims=True)
    run_ref[...] = total
    cnt_ref[...] = total.astype(I32)

    ri_ref[...] = jnp.where(lane == 0, i1, jnp.where(lane == 1, i2,
                            jnp.where(lane == 2, r1, jnp.where(lane == 3, r2, 0))))
    rw_ref[...] = jnp.where(lane == 0, w1, jnp.where(lane == 1, w2, 0.0))


def _out_proj(l, mixed2, x2, w_bf, gain, r_hi, r_lo, r_b):
    t = x2.shape[0]
    row = lambda i: (i, 0)
    layer = lambda i: (l, 0, 0)
    return pl.pallas_call(
        _out_proj_kernel,
        grid=(t // TM_PROJ,),
        in_specs=[
            pl.BlockSpec((TM_PROJ, D_MODEL), row),
            pl.BlockSpec((TM_PROJ, D_MODEL), row),
            pl.BlockSpec((1, D_MODEL, D_MODEL), layer),
            pl.BlockSpec((1, 1, D_MODEL), layer),
            pl.BlockSpec((1, D_MODEL, ROUTER_LANES), layer),
            pl.BlockSpec((1, D_MODEL, ROUTER_LANES), layer),
            pl.BlockSpec((1, 1, ROUTER_LANES), layer),
        ],
        out_specs=[
            pl.BlockSpec((TM_PROJ, D_MODEL), row),
            pl.BlockSpec((TM_PROJ, D_MODEL), row),
            pl.BlockSpec((TM_PROJ, ROUTER_LANES), row),
            pl.BlockSpec((TM_PROJ, ROUTER_LANES), row),
            pl.BlockSpec((1, ROUTER_LANES), lambda i: (0, 0)),
        ],
        out_shape=[
            jax.ShapeDtypeStruct((t, D_MODEL), F32),
            jax.ShapeDtypeStruct((t, D_MODEL), F32),
            jax.ShapeDtypeStruct((t, ROUTER_LANES), I32),
            jax.ShapeDtypeStruct((t, ROUTER_LANES), F32),
            jax.ShapeDtypeStruct((1, ROUTER_LANES), I32),
        ],
        scratch_shapes=[pltpu.VMEM((1, ROUTER_LANES), F32)],
        compiler_params=pltpu.CompilerParams(
            dimension_semantics=("arbitrary",), vmem_limit_bytes=VMEM_LIMIT),
        name="out_proj",
    )(mixed2, x2, w_bf, gain, r_hi, r_lo, r_b)


def _row_copy(src_hbm, idx_ref, dst, sem, r):
    return pltpu.make_async_copy(src_hbm.at[pl.ds(idx_ref[0, 0, r], 1)], dst.at[pl.ds(r, 1)], sem)


def _start_rows(src_hbm, idx_ref, dst, sem, n_rows):
    def body(j, carry):
        for k in range(ISSUE_UNROLL):
            _row_copy(src_hbm, idx_ref, dst, sem, j * ISSUE_UNROLL + k).start()
        return carry
    lax.fori_loop(0, n_rows // ISSUE_UNROLL, body, 0)


def _wait_rows(src_hbm, dst, sem, n_rows):
    pltpu.make_async_copy(src_hbm.at[pl.ds(0, n_rows)], dst, sem).wait()


def _dispatch_kernel(pad_lo_ref, pad_len_ref, nu_ref, slots_ref, h_ref, hs_hbm, stage, zeros, sem, pad_sem):
    i = pl.program_id(0)
    n = pl.num_programs(0)
    s = i % 2
    n_tiles = hs_hbm.shape[0] // TM_MOE
    min_used = (2 * n * TM_COMBINE) // TM_MOE

    def wait_stage(which):
        for _ in range(2):
            pltpu.make_async_copy(stage.at[which], hs_hbm.at[pl.ds(0, TM_COMBINE)], sem.at[which]).wait()

    def zero_copy(first_row, n_rows):
        return pltpu.make_async_copy(zeros.at[pl.ds(0, n_rows)], hs_hbm.at[pl.ds(first_row, n_rows)], pad_sem)

    def pad_copies(e):
        lo = pad_lo_ref[e]
        length = pad_len_ref[e]
        head = (-lo) & (PAD_ALIGN - 1)
        pairs = [(j < head, zero_copy(lo + j, 1)) for j in range(PAD_ALIGN - 1)]
        first = lo + head
        for p in PAD_PIECES:
            use = ((length - head) & p) != 0
            pairs.append((use, zero_copy(pl.multiple_of(first, PAD_ALIGN), p)))
            first = first + jnp.where(use, p, 0)
        return pairs

    def unused_tile_copies():
        half = TM_MOE // 2
        return [(j >= nu_ref[0], zero_copy(j * TM_MOE + k * half, half))
                for j in range(min_used, n_tiles) for k in range(2)]

    def for_each_zero_copy(action):
        def per_expert(e, carry):
            for cond, copy in pad_copies(e):
                pl.when(cond)(functools.partial(action, copy))
            return carry
        lax.fori_loop(0, N_EXPERTS, per_expert, 0)
        for cond, copy in unused_tile_copies():
            pl.when(cond)(functools.partial(action, copy))

    @pl.when(i == 0)
    def _():
        zeros[...] = jnp.zeros_like(zeros)
        for_each_zero_copy(lambda copy: copy.start())

    @pl.when(i >= 2)
    def _():
        wait_stage(s)

    stage[s] = h_ref[...]
    for r in range(2 * TM_COMBINE):
        pltpu.make_async_copy(stage.at[s, pl.ds(r % TM_COMBINE, 1)],
                              hs_hbm.at[pl.ds(slots_ref[0, 0, r], 1)], sem.at[s]).start()

    @pl.when(i == n - 1)
    def _():
        wait_stage(s)
        wait_stage(1 - s)
        for_each_zero_copy(lambda copy: copy.wait())


def _dispatch(h2, pad_lo, pad_len, n_used, slots3, n_slots):
    t = h2.shape[0]
    n = t // TM_COMBINE
    return pl.pallas_call(
        _dispatch_kernel,
        grid_spec=pltpu.PrefetchScalarGridSpec(
            num_scalar_prefetch=3,
            grid=(n,),
            in_specs=[
                pl.BlockSpec((1, 1, 2 * TM_COMBINE), lambda i, lo, ln, nu: (i, 0, 0),
                             memory_space=pltpu.SMEM),
                pl.BlockSpec((TM_COMBINE, D_MODEL), lambda i, lo, ln, nu: (i, 0)),
            ],
            out_specs=pl.BlockSpec(memory_space=pl.ANY),
            scratch_shapes=[
                pltpu.VMEM((2, TM_COMBINE, D_MODEL), F32),
                pltpu.VMEM((PAD_PIECES[0], D_MODEL), F32),
                pltpu.SemaphoreType.DMA((2,)),
                pltpu.SemaphoreType.DMA(()),
            ],
        ),
        out_shape=jax.ShapeDtypeStruct((n_slots, D_MODEL), F32),
        compiler_params=pltpu.CompilerParams(
            dimension_semantics=("arbitrary",), vmem_limit_bytes=VMEM_LIMIT),
        name="dispatch",
    )(pad_lo, pad_len, n_used, slots3, h2)


def _moe_kernel(te_ref, grp_ref, nxt_ref, nu_ref, x_ref, wg_hbm, wu_hbm, wd_hbm,
                y_ref, wg_st, wu_st, wd_st, wsem, wg_bf, wu_bf, wd_bf):
    i = pl.program_id(0)
    n_used = nu_ref[0]

    def weight_copies(e, par):
        return (pltpu.make_async_copy(wg_hbm.at[e], wg_st.at[par], wsem.at[par]),
                pltpu.make_async_copy(wu_hbm.at[e], wu_st.at[par], wsem.at[par]),
                pltpu.make_async_copy(wd_hbm.at[e], wd_st.at[par], wsem.at[par]))

    @pl.when(i == 0)
    def _():
        for c in weight_copies(te_ref[0], 0):
            c.start(priority=WEIGHT_DMA_PRIORITY)

    @pl.when(i < n_used)
    def _():
        @pl.when((i == 0) | (te_ref[i] != te_ref[jnp.maximum(i - 1, 0)]))
        def _():
            par = grp_ref[i] % 2
            for c in weight_copies(te_ref[i], par):
                c.wait()

            @pl.when(nxt_ref[i] >= 0)
            def _():
                for c in weight_copies(nxt_ref[i], 1 - par):
                    c.start(priority=WEIGHT_DMA_PRIORITY)

            wg_bf[...] = wg_st[par].astype(BF16)
            wu_bf[...] = wu_st[par].astype(BF16)
            wd_bf[...] = wd_st[par].astype(BF16)

        xb = x_ref[...].astype(BF16)
        g = jnp.dot(xb, wg_bf[...], preferred_element_type=F32)
        u = jnp.dot(xb, wu_bf[...], preferred_element_type=F32)
        a = (g * jax.nn.sigmoid(g) * u).astype(BF16)
        y_ref[...] = jnp.dot(a, wd_bf[...], preferred_element_type=F32)

    @pl.when(i >= n_used)
    def _():
        y_ref[...] = jnp.zeros_like(y_ref)


def _moe_ffn(hs, n_tiles, w_gate, w_up, w_down, tile_expert, tile_group, tile_next, n_used):
    hbm = pl.BlockSpec(memory_space=pl.ANY)
    tile = lambda i, te, gr, nx, nu: (i, 0)
    return pl.pallas_call(
        _moe_kernel,
        grid_spec=pltpu.PrefetchScalarGridSpec(
            num_scalar_prefetch=4,
            grid=(n_tiles,),
            in_specs=[pl.BlockSpec((TM_MOE, D_MODEL), tile), hbm, hbm, hbm],
            out_specs=pl.BlockSpec((TM_MOE, D_MODEL), tile),
            scratch_shapes=[
                pltpu.VMEM((2, D_MODEL, D_EXPERT), F32),
                pltpu.VMEM((2, D_MODEL, D_EXPERT), F32),
                pltpu.VMEM((2, D_EXPERT, D_MODEL), F32),
                pltpu.SemaphoreType.DMA((2,)),
                pltpu.VMEM((D_MODEL, D_EXPERT), BF16),
                pltpu.VMEM((D_MODEL, D_EXPERT), BF16),
                pltpu.VMEM((D_EXPERT, D_MODEL), BF16),
            ],
        ),
        out_shape=jax.ShapeDtypeStruct((n_tiles * TM_MOE, D_MODEL), F32),
        compiler_params=pltpu.CompilerParams(
            dimension_semantics=("arbitrary",), vmem_limit_bytes=VMEM_LIMIT),
        name="moe_ffn",
    )(tile_expert, tile_group, tile_next, n_used, hs, w_gate, w_up, w_down)


def _combine_kernel(first_ref, ahead_ref, x_ref, w_ref, y_hbm, o_ref, buf, sem):
    i = pl.program_id(0)
    n = pl.num_programs(0)
    rows = 2 * TM_COMBINE

    @pl.when(i == 0)
    def _():
        _start_rows(y_hbm, first_ref, buf.at[0], sem.at[0], rows)

    @pl.when(i + 1 < n)
    def _():
        nxt = (i + 1) % 2
        for r in range(rows):
            _row_copy(y_hbm, ahead_ref, buf.at[nxt], sem.at[nxt], r).start(priority=r % 2)

    slot = i % 2
    _wait_rows(y_hbm, buf.at[slot], sem.at[slot], rows)
    w = w_ref[...]
    o_ref[...] = (x_ref[...] + w[:, 0:1] * buf[slot, :TM_COMBINE]
                  + w[:, 1:2] * buf[slot, TM_COMBINE:])


def _combine(x1, rw, y, slots3):
    t = x1.shape[0]
    n = t // TM_COMBINE
    row = lambda i: (i, 0)
    smem_tile = lambda index_map: pl.BlockSpec((1, 1, 2 * TM_COMBINE), index_map,
                                               memory_space=pltpu.SMEM)
    return pl.pallas_call(
        _combine_kernel,
        grid=(n,),
        in_specs=[
            smem_tile(lambda i: (0, 0, 0)),
            smem_tile(lambda i: (jnp.minimum(i + 1, n - 1), 0, 0)),
            pl.BlockSpec((TM_COMBINE, D_MODEL), row),
            pl.BlockSpec((TM_COMBINE, ROUTER_LANES), row),
            pl.BlockSpec(memory_space=pl.ANY),
        ],
        out_specs=pl.BlockSpec((TM_COMBINE, D_MODEL), row),
        out_shape=jax.ShapeDtypeStruct((t, D_MODEL), F32),
        scratch_shapes=[
            pltpu.VMEM((2, 2 * TM_COMBINE, D_MODEL), F32),
            pltpu.SemaphoreType.DMA((2,)),
        ],
        compiler_params=pltpu.CompilerParams(
            dimension_semantics=("arbitrary",), vmem_limit_bytes=VMEM_LIMIT),
        name="combine",
    )(slots3, slots3, x1, rw, y)


def _slot_tables(l, ri, counts_row, n_tiles):
    t = ri.shape[0]
    experts = jnp.arange(N_EXPERTS, dtype=I32)
    counts = counts_row[0, :N_EXPERTS]
    tiles_per = (counts + TM_MOE - 1) // TM_MOE
    tile_end = jnp.cumsum(tiles_per)
    start = (tile_end - tiles_per) * TM_MOE
    n_used = tile_end[-1:].astype(I32)
    nonempty = counts > 0
    group_of = jnp.cumsum(nonempty.astype(I32)) - 1
    later = (experts[None, :] > experts[:, None]) & nonempty[None, :]
    next_of = jnp.min(jnp.where(later, experts[None, :], N_EXPERTS), axis=1)
    next_of = jnp.where(next_of < N_EXPERTS, next_of + l * N_EXPERTS, -1)
    tile_idx = jnp.arange(n_tiles, dtype=I32)
    tile_e = jnp.minimum(jnp.sum((tile_end[None, :] <= tile_idx[:, None]).astype(I32), axis=1),
                         N_EXPERTS - 1)
    pick = tile_e[:, None] == experts[None, :]
    tile_group = jnp.sum(jnp.where(pick, group_of[None, :], 0), axis=1)
    tile_next = jnp.sum(jnp.where(pick, next_of[None, :], 0), axis=1)
    tile_expert = tile_e + l * N_EXPERTS
    eid = ri[:, 0:2]
    slot = ri[:, 2:4] + jnp.sum(jnp.where(eid[..., None] == experts, start, 0), axis=-1)
    slot2 = slot.reshape(t // TM_COMBINE, TM_COMBINE, 2)
    slots3 = jnp.swapaxes(slot2, 1, 2).reshape(t // TM_COMBINE, 1, 2 * TM_COMBINE)
    tiles = (tile_expert.astype(I32), tile_group.astype(I32), tile_next.astype(I32), n_used)
    pad_lo = (start + counts).astype(I32)
    pad_len = (tiles_per * TM_MOE - counts).astype(I32)
    return pad_lo, pad_len, tiles, slots3


def _rope_freq_row():
    inv_freq = ROPE_THETA ** (-(jnp.arange(0, ROT_DIM, 2, dtype=F32) / ROT_DIM))
    return jnp.concatenate(
        [inv_freq, inv_freq, jnp.zeros((HEAD_DIM - ROT_DIM,), F32)]).reshape(1, HEAD_DIM)


def _router_operands(w_rg, b_rg, w_re, b_re):
    depth = w_rg.shape[0]
    pad = ROUTER_LANES - N_EXPERTS - N_GROUPS
    w = jnp.concatenate([w_re, w_rg, jnp.zeros((depth, D_MODEL, pad), F32)], axis=2)
    b = jnp.concatenate([b_re, b_rg, jnp.zeros((depth, pad), F32)], axis=1)
    hi = w.astype(BF16)
    lo = (w - hi.astype(F32)).astype(BF16)
    return hi, lo, b


def kernel(x, positions, norm_mix, w_in, w_pool, pool_scale, q_norm, k_norm, sink, branch_gain_pool,
           branch_gain_attn, w_out, norm_ffn, w_router_group, b_router_group, w_router_expert,
           b_router_expert, w_gate, w_up, w_down):
    b, s, d = x.shape
    t = b * s
    depth = norm_mix.shape[0]
    x2 = x.reshape(t, d)
    pos2 = positions.reshape(t, 1)
    freq = _rope_freq_row()
    w_in_bf = w_in.astype(BF16)
    w_out_bf = w_out.astype(BF16)
    w_pool_bf = w_pool.astype(BF16)
    r_hi, r_lo, r_b = _router_operands(w_router_group, b_router_group, w_router_expert,
                                       b_router_expert)
    wg = w_gate.reshape(depth * N_EXPERTS, D_MODEL, D_EXPERT)
    wu = w_up.reshape(depth * N_EXPERTS, D_MODEL, D_EXPERT)
    wd = w_down.reshape(depth * N_EXPERTS, D_EXPERT, D_MODEL)
    n_tiles = (2 * t) // TM_MOE + N_EXPERTS
    rows = lambda v: v.reshape(depth, 1, v.shape[-1])
    for l in range(depth):
        u, qkv = _in_proj(l, x2, pos2, rows(norm_mix), w_in_bf, rows(q_norm), rows(k_norm), freq)
        mixed = _mixer(l, u.reshape(b, s, POOL_WIDTH), qkv.reshape(b, s, QKV_WIDTH), sink, w_pool_bf,
                       rows(pool_scale), rows(branch_gain_pool), rows(branch_gain_attn))
        x1, h, ri, rw, counts_row = _out_proj(l, mixed.reshape(t, D_MODEL), x2, w_out_bf,
                                              rows(norm_ffn), r_hi, r_lo, rows(r_b))
        pad_lo, pad_len, tiles, slots3 = _slot_tables(l, ri, counts_row, n_tiles)
        hs = _dispatch(h, pad_lo, pad_len, tiles[3], slots3, n_tiles * TM_MOE)
        y = _moe_ffn(hs, n_tiles, wg, wu, wd, *tiles)
        x2 = _combine(x1, rw, y, slots3)
    return x2.reshape(b, s, d)
```

```python
import functools

import jax
import jax.numpy as jnp
from jax import lax
from jax.experimental import pallas as pl
from jax.experimental.pallas import tpu as pltpu

F32 = jnp.float32
BF16 = jnp.bfloat16
I32 = jnp.int32

D_MODEL = 2048
POOL_WIDTH = 1024
POOL_WINDOWS = (2, 4, 8, 16)
POOL_GROUP = 256
HEAD_DIM = 128
N_Q_HEADS = 8
N_KV_HEADS = 2
Q_PER_KV = 4
ATTN_WIDTH = 1024
KV_WIDTH = 256
QKV_WIDTH = ATTN_WIDTH + 2 * KV_WIDTH
IN_WIDTH = POOL_WIDTH + QKV_WIDTH
WINDOW = 128
BLOCK = 128
ROPE_THETA = 500000.0
ROT_DIM = 32
ROT_HALF = ROT_DIM // 2
N_GROUPS = 4
EXPERTS_PER_GROUP = 8
N_EXPERTS = 32
D_EXPERT = 512
EPS = 1e-6
LOG2_E = 1.4426950408889634

LANES = 128
POOL_HALO = 8
ROUTER_LANES = LANES
TM_PROJ = 512
TM_IN = 512
SUB_ROWS = 256
MIX_BLOCKS = 2
TM_MOE = 256
TM_COMBINE = 256
ISSUE_UNROLL = 16
PAD_ALIGN = 8
PAD_PIECES = (128, 64, 32, 16, 8)
WEIGHT_DMA_PRIORITY = 1
VMEM_LIMIT = 56 * 1024 * 1024


def _rms(x, gain):
    return x * lax.rsqrt(jnp.mean(x * x, axis=-1, keepdims=True) + EPS) * gain


def _in_proj_kernel(x_ref, pos_ref, g_ref, w_ref, qn_ref, kn_ref, freq_ref, u_ref, qkv_ref):
    for sub in range(TM_IN // SUB_ROWS):
        rows = slice(sub * SUB_ROWS, (sub + 1) * SUB_ROWS)
        h = _rms(x_ref[rows], g_ref[0]).astype(BF16)
        z = jnp.dot(h, w_ref[0], preferred_element_type=F32)
        u_ref[rows] = z[:, :POOL_WIDTH]

        ang = pos_ref[rows].astype(F32) * freq_ref[...]
        cos = jnp.cos(ang)
        sin = jnp.sin(ang)
        lane = lax.broadcasted_iota(I32, ang.shape, 1)
        sin_hi = jnp.where(lane >= ROT_HALF, sin, 0.0)
        sin_lo = jnp.where(lane < ROT_HALF, -sin, 0.0)

        def norm_rope(t, gain):
            y = _rms(t, gain)
            return (y * cos + pltpu.roll(y, ROT_HALF, 1) * sin_hi
                    + pltpu.roll(y, HEAD_DIM - ROT_HALF, 1) * sin_lo)

        for hd in range(N_Q_HEADS + N_KV_HEADS):
            gain = qn_ref[0] if hd < N_Q_HEADS else kn_ref[0]
            src = POOL_WIDTH + hd * HEAD_DIM
            dst = hd * HEAD_DIM
            qkv_ref[rows, dst:dst + HEAD_DIM] = norm_rope(z[:, src:src + HEAD_DIM], gain).astype(BF16)
        v0 = POOL_WIDTH + ATTN_WIDTH + KV_WIDTH
        qkv_ref[rows, ATTN_WIDTH + KV_WIDTH:] = z[:, v0:].astype(BF16)


def _in_proj(l, x2, pos2, gain, w_bf, qn, kn, freq):
    t = x2.shape[0]
    row = lambda i: (i, 0)
    layer = lambda i: (l, 0, 0)
    return pl.pallas_call(
        _in_proj_kernel,
        grid=(t // TM_IN,),
        in_specs=[
            pl.BlockSpec((TM_IN, D_MODEL), row),
            pl.BlockSpec((TM_IN, 1), row),
            pl.BlockSpec((1, 1, D_MODEL), layer),
            pl.BlockSpec((1, D_MODEL, IN_WIDTH), layer),
            pl.BlockSpec((1, 1, HEAD_DIM), layer),
            pl.BlockSpec((1, 1, HEAD_DIM), layer),
            pl.BlockSpec((1, HEAD_DIM), lambda i: (0, 0)),
        ],
        out_specs=[
            pl.BlockSpec((TM_IN, POOL_WIDTH), row),
            pl.BlockSpec((TM_IN, QKV_WIDTH), row),
        ],
        out_shape=[
            jax.ShapeDtypeStruct((t, POOL_WIDTH), F32),
            jax.ShapeDtypeStruct((t, QKV_WIDTH), BF16),
        ],
        compiler_params=pltpu.CompilerParams(
            dimension_semantics=("arbitrary",), vmem_limit_bytes=VMEM_LIMIT),
        name="in_proj",
    )(x2, pos2, gain, w_bf, qn, kn, freq)


def _mixer_kernel(sink_ref, u_ref, up_ref, un_ref, q_ref, kp_ref, kc_ref, kn_ref,
                  vp_ref, vc_ref, vn_ref, wp_ref, ps_ref, gp_ref, ga_ref, o_ref, *, seq, layer):
    n = pl.program_id(1)
    n_steps = pl.num_programs(1)
    step_rows = MIX_BLOCKS * BLOCK
    t0 = n * step_rows

    prev = jnp.where(n > 0, up_ref[0], 0.0)
    nxt = jnp.where(n < n_steps - 1, un_ref[0], 0.0)
    cur = u_ref[0]
    ext = jnp.concatenate([prev, cur, nxt], axis=0)
    rows = ext.shape[0]
    tpos = t0 + lax.broadcasted_iota(I32, (step_rows, LANES), 0)
    pooled = []
    for gi, w in enumerate(POOL_WINDOWS):
        sl = slice(gi * POOL_GROUP, (gi + 1) * POOL_GROUP)
        acc = ext[:, sl]
        span = 1
        while span < w:
            acc = acc + pltpu.roll(acc, span, 0)
            span *= 2
        lead = w // 2 - 1
        if lead:
            acc = pltpu.roll(acc, rows - lead, 0)
        wsum = acc[POOL_HALO:POOL_HALO + step_rows]
        lo = jnp.clip(tpos - w // 2, 0, seq)
        hi = jnp.clip(tpos + w // 2, 0, seq)
        inv = 1.0 / (hi - lo).astype(F32)
        y = wsum * jnp.concatenate([inv] * (POOL_GROUP // LANES), axis=1) - cur[:, sl]
        y = jnp.dot(y.astype(BF16), wp_ref[0, gi], preferred_element_type=F32)
        pooled.append(y)
    pool = jnp.concatenate(pooled, axis=1) * ps_ref[0]
    o_ref[0, :, :POOL_WIDTH] = _rms(pool, gp_ref[0]).astype(BF16)

    r = lax.broadcasted_iota(I32, (Q_PER_KV * BLOCK, BLOCK), 0) % BLOCK
    c = lax.broadcasted_iota(I32, (Q_PER_KV * BLOCK, BLOCK), 1)
    in_prev = c >= r
    in_next = c <= r
    to_log2 = (HEAD_DIM ** -0.5) * LOG2_E
    kcat, vcat = [], []
    for kh in range(N_KV_HEADS):
        ks = slice(kh * HEAD_DIM, (kh + 1) * HEAD_DIM)
        kcat.append(jnp.concatenate([kp_ref[0, :, ks], kc_ref[0, :, ks], kn_ref[0, :, ks]], axis=0))
        vcat.append(jnp.concatenate([vp_ref[0, :, ks], vc_ref[0, :, ks], vn_ref[0, :, ks]], axis=0))
    for j in range(MIX_BLOCKS):
        blk = n * MIX_BLOCKS + j
        keep_prev = jnp.logical_and(in_prev, blk > 0)
        keep_next = jnp.logical_and(in_next, blk < seq // BLOCK - 1)
        qrows = slice(j * BLOCK, (j + 1) * BLOCK)
        krows = slice(j * BLOCK, (j + 3) * BLOCK)
        heads = []
        for kh in range(N_KV_HEADS):
            q = jnp.concatenate(
                [q_ref[0, qrows, (kh * Q_PER_KV + g) * HEAD_DIM:(kh * Q_PER_KV + g + 1) * HEAD_DIM]
                 for g in range(Q_PER_KV)], axis=0)
            s = lax.dot_general(q, kcat[kh][krows], (((1,), (1,)), ((), ())),
                                preferred_element_type=F32)
            s = jnp.concatenate([
                jnp.where(keep_prev, s[:, :BLOCK] * to_log2, -jnp.inf),
                s[:, BLOCK:2 * BLOCK] * to_log2,
                jnp.where(keep_next, s[:, 2 * BLOCK:] * to_log2, -jnp.inf)], axis=1)
            ps, inv_dens = [], []
            for g in range(Q_PER_KV):
                sg = s[g * BLOCK:(g + 1) * BLOCK]
                sink = sink_ref[layer, kh * Q_PER_KV + g] * LOG2_E
                m = jnp.maximum(jnp.max(sg, axis=-1, keepdims=True), sink)
                pg = jnp.exp2(sg - m)
                inv_dens.append(1.0 / (jnp.sum(pg, axis=-1, keepdims=True) + jnp.exp2(sink - m)))
                ps.append(pg.astype(BF16))
            o = jnp.dot(jnp.concatenate(ps, axis=0), vcat[kh][krows], preferred_element_type=F32)
            heads.extend(o[g * BLOCK:(g + 1) * BLOCK] * inv_dens[g] for g in range(Q_PER_KV))
        attn = jnp.concatenate(heads, axis=1)
        o_ref[0, qrows, POOL_WIDTH:] = _rms(attn, ga_ref[0]).astype(BF16)


def _mixer(l, u3, qkv3, sink, wp_bf, pscale, gpool, gattn):
    b, s, _ = u3.shape
    nb = s // BLOCK
    step_rows = MIX_BLOCKS * BLOCK
    n_steps = s // step_rows
    halo_per_step = step_rows // POOL_HALO
    n_halo = s // POOL_HALO
    kcol = ATTN_WIDTH // KV_WIDTH
    vcol = kcol + 1
    layer = lambda bi, n: (l, 0, 0)
    prev_blk = lambda n: jnp.maximum(n * MIX_BLOCKS - 1, 0)
    next_blk = lambda n: jnp.minimum((n + 1) * MIX_BLOCKS, nb - 1)
    edge_spec = lambda blk, col: pl.BlockSpec((1, BLOCK, KV_WIDTH), lambda bi, n: (bi, blk(n), col))
    body_spec = lambda col: pl.BlockSpec((1, step_rows, KV_WIDTH), lambda bi, n: (bi, n, col))
    return pl.pallas_call(
        functools.partial(_mixer_kernel, seq=s, layer=l),
        grid=(b, n_steps),
        in_specs=[
            pl.BlockSpec(memory_space=pltpu.SMEM),
            pl.BlockSpec((1, step_rows, POOL_WIDTH), lambda bi, n: (bi, n, 0)),
            pl.BlockSpec((1, POOL_HALO, POOL_WIDTH),
                         lambda bi, n: (bi, jnp.maximum(n * halo_per_step - 1, 0), 0)),
            pl.BlockSpec((1, POOL_HALO, POOL_WIDTH),
                         lambda bi, n: (bi, jnp.minimum((n + 1) * halo_per_step, n_halo - 1), 0)),
            pl.BlockSpec((1, step_rows, ATTN_WIDTH), lambda bi, n: (bi, n, 0)),
            edge_spec(prev_blk, kcol), body_spec(kcol), edge_spec(next_blk, kcol),
            edge_spec(prev_blk, vcol), body_spec(vcol), edge_spec(next_blk, vcol),
            pl.BlockSpec((1, len(POOL_WINDOWS), POOL_GROUP, POOL_GROUP), lambda bi, n: (l, 0, 0, 0)),
            pl.BlockSpec((1, 1, POOL_WIDTH), layer),
            pl.BlockSpec((1, 1, POOL_WIDTH), layer),
            pl.BlockSpec((1, 1, ATTN_WIDTH), layer),
        ],
        out_specs=pl.BlockSpec((1, step_rows, POOL_WIDTH + ATTN_WIDTH), lambda bi, n: (bi, n, 0)),
        out_shape=jax.ShapeDtypeStruct((b, s, POOL_WIDTH + ATTN_WIDTH), BF16),
        compiler_params=pltpu.CompilerParams(
            dimension_semantics=("arbitrary", "arbitrary"), vmem_limit_bytes=VMEM_LIMIT),
        name="mixer",
    )(sink, u3, u3, u3, qkv3, qkv3, qkv3, qkv3, qkv3, qkv3, qkv3, wp_bf, pscale, gpool, gattn)


def _split_dot(a, b_hi, b_lo):
    a_hi = a.astype(BF16)
    a_lo = (a - a_hi.astype(F32)).astype(BF16)
    return (jnp.dot(a_hi, b_hi, preferred_element_type=F32)
            + jnp.dot(a_lo, b_hi, preferred_element_type=F32)
            + jnp.dot(a_hi, b_lo, preferred_element_type=F32))


def _out_proj_kernel(m_ref, x_ref, w_ref, g_ref, rh_ref, rl_ref, rb_ref,
                     x1_ref, h_ref, ri_ref, rw_ref, cnt_ref, run_ref):
    @pl.when(pl.program_id(0) == 0)
    def _():
        run_ref[...] = jnp.zeros_like(run_ref)

    lane = lax.broadcasted_iota(I32, (SUB_ROWS, ROUTER_LANES), 1)
    far = jnp.int32(ROUTER_LANES)
    neg = -jnp.inf
    lower = (lax.broadcasted_iota(I32, (SUB_ROWS, SUB_ROWS), 1)
             < lax.broadcasted_iota(I32, (SUB_ROWS, SUB_ROWS), 0)).astype(BF16)

    def top(vals):
        best = jnp.max(vals, axis=-1, keepdims=True)
        idx = jnp.min(jnp.where(vals == best, lane, far), axis=-1, keepdims=True)
        return best, idx

    run = run_ref[...]
    for sub in range(TM_PROJ // SUB_ROWS):
        rows = slice(sub * SUB_ROWS, (sub + 1) * SUB_ROWS)
        x1 = x_ref[rows] + jnp.dot(m_ref[rows], w_ref[0], preferred_element_type=F32)
        x1_ref[rows] = x1
        h = _rms(x1, g_ref[0])
        h_ref[rows] = h
        logits = _split_dot(h, rh_ref[0], rl_ref[0]) + rb_ref[0]

        gl = jnp.where((lane >= N_EXPERTS) & (lane < N_EXPERTS + N_GROUPS), logits, neg)
        gmax, gidx = top(gl)
        g_w = 1.0 / jnp.sum(jnp.exp(gl - gmax), axis=-1, keepdims=True)
        in_group = (lane < N_EXPERTS) & ((lane >> 3) == gidx - N_EXPERTS)
        el = jnp.where(in_group, logits, neg)
        m1, i1 = top(el)
        m2, i2 = top(jnp.where(lane == i1, neg, el))
        esum = jnp.sum(jnp.exp(el - m1), axis=-1, keepdims=True)
        p1 = 1.0 / esum
        p2 = jnp.exp(m2 - m1) / esum
        w1 = g_w * (p1 / (p1 + p2))
        w2 = g_w * (p2 / (p1 + p2))

        pick1 = lane == i1
        pick2 = lane == i2
        picks = (pick1 | pick2).astype(BF16)
        before = jnp.dot(lower, picks, preferred_element_type=F32) + run
        r1 = jnp.sum(jnp.where(pick1, before, 0.0), axis=-1, keepdims=True).astype(I32)
        r2 = jnp.sum(jnp.where(pick2, before, 0.0), axis=-1, keepdims=True).astype(I32)
        run = run + jnp.sum(picks.astype(F32), axis=0, keepdims=True)

        ri_ref[rows] = jnp.where(lane == 0, i1, jnp.where(lane == 1, i2,
                                 jnp.where(lane == 2, r1, jnp.where(lane == 3, r2, 0))))
        rw_ref[rows] = jnp.where(lane == 0, w1, jnp.where(lane == 1, w2, 0.0))
    run_ref[...] = run
    cnt_ref[...] = run.astype(I32)


def _out_proj(l, mixed2, x2, w_bf, gain, r_hi, r_lo, r_b):
    t = x2.shape[0]
    row = lambda i: (i, 0)
    layer = lambda i: (l, 0, 0)
    return pl.pallas_call(
        _out_proj_kernel,
        grid=(t // TM_PROJ,),
        in_specs=[
            pl.BlockSpec((TM_PROJ, D_MODEL), row),
            pl.BlockSpec((TM_PROJ, D_MODEL), row),
            pl.BlockSpec((1, D_MODEL, D_MODEL), layer),
            pl.BlockSpec((1, 1, D_MODEL), layer),
            pl.BlockSpec((1, D_MODEL, ROUTER_LANES), layer),
            pl.BlockSpec((1, D_MODEL, ROUTER_LANES), layer),
            pl.BlockSpec((1, 1, ROUTER_LANES), layer),
        ],
        out_specs=[
            pl.BlockSpec((TM_PROJ, D_MODEL), row),
            pl.BlockSpec((TM_PROJ, D_MODEL), row),
            pl.BlockSpec((TM_PROJ, ROUTER_LANES), row),
            pl.BlockSpec((TM_PROJ, ROUTER_LANES), row),
            pl.BlockSpec((1, ROUTER_LANES), lambda i: (0, 0)),
        ],
        out_shape=[
            jax.ShapeDtypeStruct((t, D_MODEL), F32),
            jax.ShapeDtypeStruct((t, D_MODEL), F32),
            jax.ShapeDtypeStruct((t, ROUTER_LANES), I32),
            jax.ShapeDtypeStruct((t, ROUTER_LANES), F32),
            jax.ShapeDtypeStruct((1, ROUTER_LANES), I32),
        ],
        scratch_shapes=[pltpu.VMEM((1, ROUTER_LANES), F32)],
        compiler_params=pltpu.CompilerParams(
            dimension_semantics=("arbitrary",), vmem_limit_bytes=VMEM_LIMIT),
        name="out_proj",
    )(mixed2, x2, w_bf, gain, r_hi, r_lo, r_b)


def _row_copy(src_hbm, idx_ref, dst, sem, r):
    return pltpu.make_async_copy(src_hbm.at[pl.ds(idx_ref[0, 0, r], 1)], dst.at[pl.ds(r, 1)], sem)


def _start_rows(src_hbm, idx_ref, dst, sem, n_rows):
    def body(j, carry):
        for k in range(ISSUE_UNROLL):
            _row_copy(src_hbm, idx_ref, dst, sem, j * ISSUE_UNROLL + k).start()
        return carry
    lax.fori_loop(0, n_rows // ISSUE_UNROLL, body, 0)


def _wait_rows(src_hbm, dst, sem, n_rows):
    pltpu.make_async_copy(src_hbm.at[pl.ds(0, n_rows)], dst, sem).wait()


def _dispatch_kernel(pad_lo_ref, pad_len_ref, nu_ref, slots_ref, h_ref, hs_hbm, stage, zeros, sem, pad_sem):
    i = pl.program_id(0)
    n = pl.num_programs(0)
    s = i % 2
    n_tiles = hs_hbm.shape[0] // TM_MOE
    min_used = (2 * n * TM_COMBINE) // TM_MOE

    def wait_stage(which):
        for _ in range(2):
            pltpu.make_async_copy(stage.at[which], hs_hbm.at[pl.ds(0, TM_COMBINE)], sem.at[which]).wait()

    def zero_copy(first_row, n_rows):
        return pltpu.make_async_copy(zeros.at[pl.ds(0, n_rows)], hs_hbm.at[pl.ds(first_row, n_rows)], pad_sem)

    def pad_copies(e):
        lo = pad_lo_ref[e]
        length = pad_len_ref[e]
        head = (-lo) & (PAD_ALIGN - 1)
        pairs = [(j < head, zero_copy(lo + j, 1)) for j in range(PAD_ALIGN - 1)]
        first = lo + head
        for p in PAD_PIECES:
            use = ((length - head) & p) != 0
            pairs.append((use, zero_copy(pl.multiple_of(first, PAD_ALIGN), p)))
            first = first + jnp.where(use, p, 0)
        return pairs

    def unused_tile_copies():
        half = TM_MOE // 2
        return [(j >= nu_ref[0], zero_copy(j * TM_MOE + k * half, half))
                for j in range(min_used, n_tiles) for k in range(2)]

    def for_each_zero_copy(action):
        def per_expert(e, carry):
            for cond, copy in pad_copies(e):
                pl.when(cond)(functools.partial(action, copy))
            return carry
        lax.fori_loop(0, N_EXPERTS, per_expert, 0)
        for cond, copy in unused_tile_copies():
            pl.when(cond)(functools.partial(action, copy))

    @pl.when(i == 0)
    def _():
        zeros[...] = jnp.zeros_like(zeros)
        for_each_zero_copy(lambda copy: copy.start())

    @pl.when(i >= 2)
    def _():
        wait_stage(s)

    stage[s] = h_ref[...]
    for r in range(2 * TM_COMBINE):
        pltpu.make_async_copy(stage.at[s, pl.ds(r % TM_COMBINE, 1)],
                              hs_hbm.at[pl.ds(slots_ref[0, 0, r], 1)], sem.at[s]).start()

    @pl.when(i == n - 1)
    def _():
        wait_stage(s)
        wait_stage(1 - s)
        for_each_zero_copy(lambda copy: copy.wait())


def _dispatch(h2, pad_lo, pad_len, n_used, slots3, n_slots):
    t = h2.shape[0]
    n = t // TM_COMBINE
    return pl.pallas_call(
        _dispatch_kernel,
        grid_spec=pltpu.PrefetchScalarGridSpec(
            num_scalar_prefetch=3,
            grid=(n,),
            in_specs=[
                pl.BlockSpec((1, 1, 2 * TM_COMBINE), lambda i, lo, ln, nu: (i, 0, 0),
                             memory_space=pltpu.SMEM),
                pl.BlockSpec((TM_COMBINE, D_MODEL), lambda i, lo, ln, nu: (i, 0)),
            ],
            out_specs=pl.BlockSpec(memory_space=pl.ANY),
            scratch_shapes=[
                pltpu.VMEM((2, TM_COMBINE, D_MODEL), F32),
                pltpu.VMEM((PAD_PIECES[0], D_MODEL), F32),
                pltpu.SemaphoreType.DMA((2,)),
                pltpu.SemaphoreType.DMA(()),
            ],
        ),
        out_shape=jax.ShapeDtypeStruct((n_slots, D_MODEL), F32),
        compiler_params=pltpu.CompilerParams(
            dimension_semantics=("arbitrary",), vmem_limit_bytes=VMEM_LIMIT),
        name="dispatch",
    )(pad_lo, pad_len, n_used, slots3, h2)


def _moe_kernel(te_ref, grp_ref, nxt_ref, nu_ref, x_ref, wg_hbm, wu_hbm, wd_hbm,
                y_ref, wg_st, wu_st, wd_st, wsem, wg_bf, wu_bf, wd_bf):
    i = pl.program_id(0)
    n_used = nu_ref[0]

    def weight_copies(e, par):
        return (pltpu.make_async_copy(wg_hbm.at[e], wg_st.at[par], wsem.at[par]),
                pltpu.make_async_copy(wu_hbm.at[e], wu_st.at[par], wsem.at[par]),
                pltpu.make_async_copy(wd_hbm.at[e], wd_st.at[par], wsem.at[par]))

    @pl.when(i == 0)
    def _():
        for c in weight_copies(te_ref[0], 0):
            c.start(priority=WEIGHT_DMA_PRIORITY)

    @pl.when(i < n_used)
    def _():
        @pl.when((i == 0) | (te_ref[i] != te_ref[jnp.maximum(i - 1, 0)]))
        def _():
            par = grp_ref[i] % 2
            for c in weight_copies(te_ref[i], par):
                c.wait()

            @pl.when(nxt_ref[i] >= 0)
            def _():
                for c in weight_copies(nxt_ref[i], 1 - par):
                    c.start(priority=WEIGHT_DMA_PRIORITY)

            wg_bf[...] = wg_st[par].astype(BF16)
            wu_bf[...] = wu_st[par].astype(BF16)
            wd_bf[...] = wd_st[par].astype(BF16)

        xb = x_ref[...].astype(BF16)
        g = jnp.dot(xb, wg_bf[...], preferred_element_type=F32)
        u = jnp.dot(xb, wu_bf[...], preferred_element_type=F32)
        a = (g * jax.nn.sigmoid(g) * u).astype(BF16)
        y_ref[...] = jnp.dot(a, wd_bf[...], preferred_element_type=F32)

    @pl.when(i >= n_used)
    def _():
        y_ref[...] = jnp.zeros_like(y_ref)


def _moe_ffn(hs, n_tiles, w_gate, w_up, w_down, tile_expert, tile_group, tile_next, n_used):
    hbm = pl.BlockSpec(memory_space=pl.ANY)
    tile = lambda i, te, gr, nx, nu: (i, 0)
    used_tile = lambda i, te, gr, nx, nu: (jnp.minimum(i, nu[0] - 1), 0)
    return pl.pallas_call(
        _moe_kernel,
        grid_spec=pltpu.PrefetchScalarGridSpec(
            num_scalar_prefetch=4,
            grid=(n_tiles,),
            in_specs=[pl.BlockSpec((TM_MOE, D_MODEL), used_tile), hbm, hbm, hbm],
            out_specs=pl.BlockSpec((TM_MOE, D_MODEL), tile),
            scratch_shapes=[
                pltpu.VMEM((2, D_MODEL, D_EXPERT), F32),
                pltpu.VMEM((2, D_MODEL, D_EXPERT), F32),
                pltpu.VMEM((2, D_EXPERT, D_MODEL), F32),
                pltpu.SemaphoreType.DMA((2,)),
                pltpu.VMEM((D_MODEL, D_EXPERT), BF16),
                pltpu.VMEM((D_MODEL, D_EXPERT), BF16),
                pltpu.VMEM((D_EXPERT, D_MODEL), BF16),
            ],
        ),
        out_shape=jax.ShapeDtypeStruct((n_tiles * TM_MOE, D_MODEL), F32),
        compiler_params=pltpu.CompilerParams(
            dimension_semantics=("arbitrary",), vmem_limit_bytes=VMEM_LIMIT),
        name="moe_ffn",
    )(tile_expert, tile_group, tile_next, n_used, hs, w_gate, w_up, w_down)


def _combine_kernel(first_ref, ahead_ref, x_ref, w_ref, y_hbm, o_ref, buf, sem):
    i = pl.program_id(0)
    n = pl.num_programs(0)
    rows = 2 * TM_COMBINE

    @pl.when(i == 0)
    def _():
        _start_rows(y_hbm, first_ref, buf.at[0], sem.at[0], rows)

    @pl.when(i + 1 < n)
    def _():
        nxt = (i + 1) % 2
        for r in range(rows):
            _row_copy(y_hbm, ahead_ref, buf.at[nxt], sem.at[nxt], r).start(priority=r % 2)

    slot = i % 2
    _wait_rows(y_hbm, buf.at[slot], sem.at[slot], rows)
    w = w_ref[...]
    o_ref[...] = (x_ref[...] + w[:, 0:1] * buf[slot, :TM_COMBINE]
                  + w[:, 1:2] * buf[slot, TM_COMBINE:])


def _combine(x1, rw, y, slots3):
    t = x1.shape[0]
    n = t // TM_COMBINE
    row = lambda i: (i, 0)
    smem_tile = lambda index_map: pl.BlockSpec((1, 1, 2 * TM_COMBINE), index_map,
                                               memory_space=pltpu.SMEM)
    return pl.pallas_call(
        _combine_kernel,
        grid=(n,),
        in_specs=[
            smem_tile(lambda i: (0, 0, 0)),
            smem_tile(lambda i: (jnp.minimum(i + 1, n - 1), 0, 0)),
            pl.BlockSpec((TM_COMBINE, D_MODEL), row),
            pl.BlockSpec((TM_COMBINE, ROUTER_LANES), row),
            pl.BlockSpec(memory_space=pl.ANY),
        ],
        out_specs=pl.BlockSpec((TM_COMBINE, D_MODEL), row),
        out_shape=jax.ShapeDtypeStruct((t, D_MODEL), F32),
        scratch_shapes=[
            pltpu.VMEM((2, 2 * TM_COMBINE, D_MODEL), F32),
            pltpu.SemaphoreType.DMA((2,)),
        ],
        compiler_params=pltpu.CompilerParams(
            dimension_semantics=("arbitrary",), vmem_limit_bytes=VMEM_LIMIT),
        name="combine",
    )(slots3, slots3, x1, rw, y)


def _slot_tables(l, ri, counts_row, n_tiles):
    t = ri.shape[0]
    experts = jnp.arange(N_EXPERTS, dtype=I32)
    counts = counts_row[0, :N_EXPERTS]
    tiles_per = (counts + TM_MOE - 1) // TM_MOE
    tile_end = jnp.cumsum(tiles_per)
    start = (tile_end - tiles_per) * TM_MOE
    n_used = tile_end[-1:].astype(I32)
    nonempty = counts > 0
    group_of = jnp.cumsum(nonempty.astype(I32)) - 1
    later = (experts[None, :] > experts[:, None]) & nonempty[None, :]
    next_of = jnp.min(jnp.where(later, experts[None, :], N_EXPERTS), axis=1)
    next_of = jnp.where(next_of < N_EXPERTS, next_of + l * N_EXPERTS, -1)
    tile_idx = jnp.arange(n_tiles, dtype=I32)
    tile_e = jnp.minimum(jnp.sum((tile_end[None, :] <= tile_idx[:, None]).astype(I32), axis=1),
                         N_EXPERTS - 1)
    pick = tile_e[:, None] == experts[None, :]
    tile_group = jnp.sum(jnp.where(pick, group_of[None, :], 0), axis=1)
    tile_next = jnp.sum(jnp.where(pick, next_of[None, :], 0), axis=1)
    tile_expert = tile_e + l * N_EXPERTS
    eid = ri[:, 0:2]
    slot = ri[:, 2:4] + jnp.sum(jnp.where(eid[..., None] == experts, start, 0), axis=-1)
    slot2 = slot.reshape(t // TM_COMBINE, TM_COMBINE, 2)
    slots3 = jnp.swapaxes(slot2, 1, 2).reshape(t // TM_COMBINE, 1, 2 * TM_COMBINE)
    tiles = (tile_expert.astype(I32), tile_group.astype(I32), tile_next.astype(I32), n_used)
    pad_lo = (start + counts).astype(I32)
    pad_len = (tiles_per * TM_MOE - counts).astype(I32)
    return pad_lo, pad_len, tiles, slots3


def _rope_freq_row():
    inv_freq = ROPE_THETA ** (-(jnp.arange(0, ROT_DIM, 2, dtype=F32) / ROT_DIM))
    return jnp.concatenate(
        [inv_freq, inv_freq, jnp.zeros((HEAD_DIM - ROT_DIM,), F32)]).reshape(1, HEAD_DIM)


def _router_operands(w_rg, b_rg, w_re, b_re):
    depth = w_rg.shape[0]
    pad = ROUTER_LANES - N_EXPERTS - N_GROUPS
    w = jnp.concatenate([w_re, w_rg, jnp.zeros((depth, D_MODEL, pad), F32)], axis=2)
    b = jnp.concatenate([b_re, b_rg, jnp.zeros((depth, pad), F32)], axis=1)
    hi = w.astype(BF16)
    lo = (w - hi.astype(F32)).astype(BF16)
    return hi, lo, b


def kernel(x, positions, norm_mix, w_in, w_pool, pool_scale, q_norm, k_norm, sink, branch_gain_pool,
           branch_gain_attn, w_out, norm_ffn, w_router_group, b_router_group, w_router_expert,
           b_router_expert, w_gate, w_up, w_down):
    b, s, d = x.shape
    t = b * s
    depth = norm_mix.shape[0]
    x2 = x.reshape(t, d)
    pos2 = positions.reshape(t, 1)
    freq = _rope_freq_row()
    w_in_bf = w_in.astype(BF16)
    w_out_bf = w_out.astype(BF16)
    w_pool_bf = w_pool.astype(BF16)
    r_hi, r_lo, r_b = _router_operands(w_router_group, b_router_group, w_router_expert,
                                       b_router_expert)
    wg = w_gate.reshape(depth * N_EXPERTS, D_MODEL, D_EXPERT)
    wu = w_up.reshape(depth * N_EXPERTS, D_MODEL, D_EXPERT)
    wd = w_down.reshape(depth * N_EXPERTS, D_EXPERT, D_MODEL)
    n_tiles = (2 * t) // TM_MOE + N_EXPERTS
    rows = lambda v: v.reshape(depth, 1, v.shape[-1])
    for l in range(depth):
        u, qkv = _in_proj(l, x2, pos2, rows(norm_mix), w_in_bf, rows(q_norm), rows(k_norm), freq)
        mixed = _mixer(l, u.reshape(b, s, POOL_WIDTH), qkv.reshape(b, s, QKV_WIDTH), sink, w_pool_bf,
                       rows(pool_scale), rows(branch_gain_pool), rows(branch_gain_attn))
        x1, h, ri, rw, counts_row = _out_proj(l, mixed.reshape(t, D_MODEL), x2, w_out_bf,
                                              rows(norm_ffn), r_hi, r_lo, rows(r_b))
        pad_lo, pad_len, tiles, slots3 = _slot_tables(l, ri, counts_row, n_tiles)
        hs = _dispatch(h, pad_lo, pad_len, tiles[3], slots3, n_tiles * TM_MOE)
        y = _moe_ffn(hs, n_tiles, wg, wu, wd, *tiles)
        x2 = _combine(x1, rw, y, slots3)
    return x2.reshape(b, s, d)
```

```python
import functools

import jax
import jax.numpy as jnp
from jax import lax
from jax.experimental import pallas as pl
from jax.experimental.pallas import tpu as pltpu

F32 = jnp.float32
BF16 = jnp.bfloat16
I32 = jnp.int32

D_MODEL = 2048
POOL_WIDTH = 1024
POOL_WINDOWS = (2, 4, 8, 16)
POOL_GROUP = 256
HEAD_DIM = 128
N_Q_HEADS = 8
N_KV_HEADS = 2
Q_PER_KV = 4
ATTN_WIDTH = 1024
KV_WIDTH = 256
QKV_WIDTH = ATTN_WIDTH + 2 * KV_WIDTH
IN_WIDTH = POOL_WIDTH + QKV_WIDTH
WINDOW = 128
BLOCK = 128
ROPE_THETA = 500000.0
ROT_DIM = 32
ROT_HALF = ROT_DIM // 2
N_GROUPS = 4
EXPERTS_PER_GROUP = 8
N_EXPERTS = 32
D_EXPERT = 512
EPS = 1e-6
LOG2_E = 1.4426950408889634

LANES = 128
POOL_HALO = 8
ROUTER_LANES = LANES
TM_PROJ = 512
TM_IN = 512
SUB_ROWS = 256
MIX_BLOCKS = 4
TM_MOE = 256
TM_COMBINE = 512
ISSUE_UNROLL = 16
PAD_ALIGN = 8
PAD_PIECES = (128, 64, 32, 16, 8)
WEIGHT_DMA_PRIORITY = 1
VMEM_LIMIT = 56 * 1024 * 1024


def _rms(x, gain):
    return x * lax.rsqrt(jnp.mean(x * x, axis=-1, keepdims=True) + EPS) * gain


def _in_proj_kernel(x_ref, pos_ref, g_ref, w_ref, qn_ref, kn_ref, freq_ref, u_ref, qkv_ref):
    for sub in range(TM_IN // SUB_ROWS):
        rows = slice(sub * SUB_ROWS, (sub + 1) * SUB_ROWS)
        h = _rms(x_ref[rows], g_ref[0]).astype(BF16)
        z = jnp.dot(h, w_ref[0], preferred_element_type=F32)
        u_ref[rows] = z[:, :POOL_WIDTH]

        ang = pos_ref[rows].astype(F32) * freq_ref[...]
        cos = jnp.cos(ang)
        sin = jnp.sin(ang)
        lane = lax.broadcasted_iota(I32, ang.shape, 1)
        sin_hi = jnp.where(lane >= ROT_HALF, sin, 0.0)
        sin_lo = jnp.where(lane < ROT_HALF, -sin, 0.0)

        def norm_rope(t, gain):
            y = _rms(t, gain)
            return (y * cos + pltpu.roll(y, ROT_HALF, 1) * sin_hi
                    + pltpu.roll(y, HEAD_DIM - ROT_HALF, 1) * sin_lo)

        for hd in range(N_Q_HEADS + N_KV_HEADS):
            gain = qn_ref[0] if hd < N_Q_HEADS else kn_ref[0]
            src = POOL_WIDTH + hd * HEAD_DIM
            dst = hd * HEAD_DIM
            qkv_ref[rows, dst:dst + HEAD_DIM] = norm_rope(z[:, src:src + HEAD_DIM], gain).astype(BF16)
        v0 = POOL_WIDTH + ATTN_WIDTH + KV_WIDTH
        qkv_ref[rows, ATTN_WIDTH + KV_WIDTH:] = z[:, v0:].astype(BF16)


def _in_proj(l, x2, pos2, gain, w_bf, qn, kn, freq):
    t = x2.shape[0]
    row = lambda i: (i, 0)
    layer = lambda i: (l, 0, 0)
    return pl.pallas_call(
        _in_proj_kernel,
        grid=(t // TM_IN,),
        in_specs=[
            pl.BlockSpec((TM_IN, D_MODEL), row),
            pl.BlockSpec((TM_IN, 1), row),
            pl.BlockSpec((1, 1, D_MODEL), layer),
            pl.BlockSpec((1, D_MODEL, IN_WIDTH), layer),
            pl.BlockSpec((1, 1, HEAD_DIM), layer),
            pl.BlockSpec((1, 1, HEAD_DIM), layer),
            pl.BlockSpec((1, HEAD_DIM), lambda i: (0, 0)),
        ],
        out_specs=[
            pl.BlockSpec((TM_IN, POOL_WIDTH), row),
            pl.BlockSpec((TM_IN, QKV_WIDTH), row),
        ],
        out_shape=[
            jax.ShapeDtypeStruct((t, POOL_WIDTH), F32),
            jax.ShapeDtypeStruct((t, QKV_WIDTH), BF16),
        ],
        compiler_params=pltpu.CompilerParams(
            dimension_semantics=("arbitrary",), vmem_limit_bytes=VMEM_LIMIT),
        name="in_proj",
    )(x2, pos2, gain, w_bf, qn, kn, freq)


def _mixer_kernel(sink_ref, u_ref, up_ref, un_ref, q_ref, kp_ref, kc_ref, kn_ref,
                  vp_ref, vc_ref, vn_ref, wp_ref, ps_ref, gp_ref, ga_ref, o_ref, *, seq, layer):
    n = pl.program_id(1)
    n_steps = pl.num_programs(1)
    step_rows = MIX_BLOCKS * BLOCK
    t0 = n * step_rows

    prev = jnp.where(n > 0, up_ref[0], 0.0)
    nxt = jnp.where(n < n_steps - 1, un_ref[0], 0.0)
    cur = u_ref[0]
    ext = jnp.concatenate([prev, cur, nxt], axis=0)
    rows = ext.shape[0]
    tpos = t0 + lax.broadcasted_iota(I32, (step_rows, LANES), 0)
    pooled = []
    for gi, w in enumerate(POOL_WINDOWS):
        sl = slice(gi * POOL_GROUP, (gi + 1) * POOL_GROUP)
        acc = ext[:, sl]
        span = 1
        while span < w:
            acc = acc + pltpu.roll(acc, span, 0)
            span *= 2
        lead = w // 2 - 1
        if lead:
            acc = pltpu.roll(acc, rows - lead, 0)
        wsum = acc[POOL_HALO:POOL_HALO + step_rows]
        lo = jnp.clip(tpos - w // 2, 0, seq)
        hi = jnp.clip(tpos + w // 2, 0, seq)
        inv = 1.0 / (hi - lo).astype(F32)
        y = wsum * jnp.concatenate([inv] * (POOL_GROUP // LANES), axis=1) - cur[:, sl]
        y = jnp.dot(y.astype(BF16), wp_ref[0, gi], preferred_element_type=F32)
        pooled.append(y)
    pool = jnp.concatenate(pooled, axis=1) * ps_ref[0]
    o_ref[0, :, :POOL_WIDTH] = _rms(pool, gp_ref[0]).astype(BF16)

    r = lax.broadcasted_iota(I32, (Q_PER_KV * BLOCK, BLOCK), 0) % BLOCK
    c = lax.broadcasted_iota(I32, (Q_PER_KV * BLOCK, BLOCK), 1)
    in_prev = c >= r
    in_next = c <= r
    to_log2 = (HEAD_DIM ** -0.5) * LOG2_E
    kcat, vcat = [], []
    for kh in range(N_KV_HEADS):
        ks = slice(kh * HEAD_DIM, (kh + 1) * HEAD_DIM)
        kcat.append(jnp.concatenate([kp_ref[0, :, ks], kc_ref[0, :, ks], kn_ref[0, :, ks]], axis=0))
        vcat.append(jnp.concatenate([vp_ref[0, :, ks], vc_ref[0, :, ks], vn_ref[0, :, ks]], axis=0))
    for j in range(MIX_BLOCKS):
        blk = n * MIX_BLOCKS + j
        keep_prev = jnp.logical_and(in_prev, blk > 0)
        keep_next = jnp.logical_and(in_next, blk < seq // BLOCK - 1)
        qrows = slice(j * BLOCK, (j + 1) * BLOCK)
        krows = slice(j * BLOCK, (j + 3) * BLOCK)
        heads = []
        for kh in range(N_KV_HEADS):
            q = jnp.concatenate(
                [q_ref[0, qrows, (kh * Q_PER_KV + g) * HEAD_DIM:(kh * Q_PER_KV + g + 1) * HEAD_DIM]
                 for g in range(Q_PER_KV)], axis=0)
            s = lax.dot_general(q, kcat[kh][krows], (((1,), (1,)), ((), ())),
                                preferred_element_type=F32)
            s = jnp.concatenate([
                jnp.where(keep_prev, s[:, :BLOCK] * to_log2, -jnp.inf),
                s[:, BLOCK:2 * BLOCK] * to_log2,
                jnp.where(keep_next, s[:, 2 * BLOCK:] * to_log2, -jnp.inf)], axis=1)
            ps, inv_dens = [], []
            for g in range(Q_PER_KV):
                sg = s[g * BLOCK:(g + 1) * BLOCK]
                sink = sink_ref[layer, kh * Q_PER_KV + g] * LOG2_E
                m = jnp.maximum(jnp.max(sg, axis=-1, keepdims=True), sink)
                pg = jnp.exp2(sg - m)
                inv_dens.append(1.0 / (jnp.sum(pg, axis=-1, keepdims=True) + jnp.exp2(sink - m)))
                ps.append(pg.astype(BF16))
            o = jnp.dot(jnp.concatenate(ps, axis=0), vcat[kh][krows], preferred_element_type=F32)
            heads.extend(o[g * BLOCK:(g + 1) * BLOCK] * inv_dens[g] for g in range(Q_PER_KV))
        attn = jnp.concatenate(heads, axis=1)
        o_ref[0, qrows, POOL_WIDTH:] = _rms(attn, ga_ref[0]).astype(BF16)


def _mixer(l, u3, qkv3, sink, wp_bf, pscale, gpool, gattn):
    b, s, _ = u3.shape
    nb = s // BLOCK
    step_rows = MIX_BLOCKS * BLOCK
    n_steps = s // step_rows
    halo_per_step = step_rows // POOL_HALO
    n_halo = s // POOL_HALO
    kcol = ATTN_WIDTH // KV_WIDTH
    vcol = kcol + 1
    layer = lambda bi, n: (l, 0, 0)
    prev_blk = lambda n: jnp.maximum(n * MIX_BLOCKS - 1, 0)
    next_blk = lambda n: jnp.minimum((n + 1) * MIX_BLOCKS, nb - 1)
    edge_spec = lambda blk, col: pl.BlockSpec((1, BLOCK, KV_WIDTH), lambda bi, n: (bi, blk(n), col))
    body_spec = lambda col: pl.BlockSpec((1, step_rows, KV_WIDTH), lambda bi, n: (bi, n, col))
    return pl.pallas_call(
        functools.partial(_mixer_kernel, seq=s, layer=l),
        grid=(b, n_steps),
        in_specs=[
            pl.BlockSpec(memory_space=pltpu.SMEM),
            pl.BlockSpec((1, step_rows, POOL_WIDTH), lambda bi, n: (bi, n, 0)),
            pl.BlockSpec((1, POOL_HALO, POOL_WIDTH),
                         lambda bi, n: (bi, jnp.maximum(n * halo_per_step - 1, 0), 0)),
            pl.BlockSpec((1, POOL_HALO, POOL_WIDTH),
                         lambda bi, n: (bi, jnp.minimum((n + 1) * halo_per_step, n_halo - 1), 0)),
            pl.BlockSpec((1, step_rows, ATTN_WIDTH), lambda bi, n: (bi, n, 0)),
            edge_spec(prev_blk, kcol), body_spec(kcol), edge_spec(next_blk, kcol),
            edge_spec(prev_blk, vcol), body_spec(vcol), edge_spec(next_blk, vcol),
            pl.BlockSpec((1, len(POOL_WINDOWS), POOL_GROUP, POOL_GROUP), lambda bi, n: (l, 0, 0, 0)),
            pl.BlockSpec((1, 1, POOL_WIDTH), layer),
            pl.BlockSpec((1, 1, POOL_WIDTH), layer),
            pl.BlockSpec((1, 1, ATTN_WIDTH), layer),
        ],
        out_specs=pl.BlockSpec((1, step_rows, POOL_WIDTH + ATTN_WIDTH), lambda bi, n: (bi, n, 0)),
        out_shape=jax.ShapeDtypeStruct((b, s, POOL_WIDTH + ATTN_WIDTH), BF16),
        compiler_params=pltpu.CompilerParams(
            dimension_semantics=("arbitrary", "arbitrary"), vmem_limit_bytes=VMEM_LIMIT),
        name="mixer",
    )(sink, u3, u3, u3, qkv3, qkv3, qkv3, qkv3, qkv3, qkv3, qkv3, wp_bf, pscale, gpool, gattn)


def _split_dot(a, b_hi, b_lo):
    a_hi = a.astype(BF16)
    a_lo = (a - a_hi.astype(F32)).astype(BF16)
    return (jnp.dot(a_hi, b_hi, preferred_element_type=F32)
            + jnp.dot(a_lo, b_hi, preferred_element_type=F32)
            + jnp.dot(a_hi, b_lo, preferred_element_type=F32))


def _out_proj_kernel(m_ref, x_ref, w_ref, g_ref, rh_ref, rl_ref, rb_ref,
                     x1_ref, h_ref, ri_ref, rw_ref, cnt_ref, run_ref):
    @pl.when(pl.program_id(0) == 0)
    def _():
        run_ref[...] = jnp.zeros_like(run_ref)

    lane = lax.broadcasted_iota(I32, (SUB_ROWS, ROUTER_LANES), 1)
    far = jnp.int32(ROUTER_LANES)
    neg = -jnp.inf
    lower = (lax.broadcasted_iota(I32, (SUB_ROWS, SUB_ROWS), 1)
             < lax.broadcasted_iota(I32, (SUB_ROWS, SUB_ROWS), 0)).astype(BF16)

    def top(vals):
        best = jnp.max(vals, axis=-1, keepdims=True)
        idx = jnp.min(jnp.where(vals == best, lane, far), axis=-1, keepdims=True)
        return best, idx

    run = run_ref[...]
    for sub in range(TM_PROJ // SUB_ROWS):
        rows = slice(sub * SUB_ROWS, (sub + 1) * SUB_ROWS)
        x1 = x_ref[rows] + jnp.dot(m_ref[rows], w_ref[0], preferred_element_type=F32)
        x1_ref[rows] = x1
        h = _rms(x1, g_ref[0])
        h_ref[rows] = h
        logits = _split_dot(h, rh_ref[0], rl_ref[0]) + rb_ref[0]

        gl = jnp.where((lane >= N_EXPERTS) & (lane < N_EXPERTS + N_GROUPS), logits, neg)
        gmax, gidx = top(gl)
        g_w = 1.0 / jnp.sum(jnp.exp(gl - gmax), axis=-1, keepdims=True)
        in_group = (lane < N_EXPERTS) & ((lane >> 3) == gidx - N_EXPERTS)
        el = jnp.where(in_group, logits, neg)
        m1, i1 = top(el)
        m2, i2 = top(jnp.where(lane == i1, neg, el))
        esum = jnp.sum(jnp.exp(el - m1), axis=-1, keepdims=True)
        p1 = 1.0 / esum
        p2 = jnp.exp(m2 - m1) / esum
        w1 = g_w * (p1 / (p1 + p2))
        w2 = g_w * (p2 / (p1 + p2))

        pick1 = lane == i1
        pick2 = lane == i2
        picks = (pick1 | pick2).astype(BF16)
        before = jnp.dot(lower, picks, preferred_element_type=F32) + run
        r1 = jnp.sum(jnp.where(pick1, before, 0.0), axis=-1, keepdims=True).astype(I32)
        r2 = jnp.sum(jnp.where(pick2, before, 0.0), axis=-1, keepdims=True).astype(I32)
        run = run + jnp.sum(picks.astype(F32), axis=0, keepdims=True)

        ri_ref[rows] = jnp.where(lane == 0, i1, jnp.where(lane == 1, i2,
                                 jnp.where(lane == 2, r1, jnp.where(lane == 3, r2, 0))))
        rw_ref[rows] = jnp.where(lane == 0, w1, jnp.where(lane == 1, w2, 0.0))
    run_ref[...] = run
    cnt_ref[...] = run.astype(I32)


def _out_proj(l, mixed2, x2, w_bf, gain, r_hi, r_lo, r_b):
    t = x2.shape[0]
    row = lambda i: (i, 0)
    layer = lambda i: (l, 0, 0)
    return pl.pallas_call(
        _out_proj_kernel,
        grid=(t // TM_PROJ,),
        in_specs=[
            pl.BlockSpec((TM_PROJ, D_MODEL), row),
            pl.BlockSpec((TM_PROJ, D_MODEL), row),
            pl.BlockSpec((1, D_MODEL, D_MODEL), layer),
            pl.BlockSpec((1, 1, D_MODEL), layer),
            pl.BlockSpec((1, D_MODEL, ROUTER_LANES), layer),
            pl.BlockSpec((1, D_MODEL, ROUTER_LANES), layer),
            pl.BlockSpec((1, 1, ROUTER_LANES), layer),
        ],
        out_specs=[
            pl.BlockSpec((TM_PROJ, D_MODEL), row),
            pl.BlockSpec((TM_PROJ, D_MODEL), row),
            pl.BlockSpec((TM_PROJ, ROUTER_LANES), row),
            pl.BlockSpec((TM_PROJ, ROUTER_LANES), row),
            pl.BlockSpec((1, ROUTER_LANES), lambda i: (0, 0)),
        ],
        out_shape=[
            jax.ShapeDtypeStruct((t, D_MODEL), F32),
            jax.ShapeDtypeStruct((t, D_MODEL), F32),
            jax.ShapeDtypeStruct((t, ROUTER_LANES), I32),
            jax.ShapeDtypeStruct((t, ROUTER_LANES), F32),
            jax.ShapeDtypeStruct((1, ROUTER_LANES), I32),
        ],
        scratch_shapes=[pltpu.VMEM((1, ROUTER_LANES), F32)],
        compiler_params=pltpu.CompilerParams(
            dimension_semantics=("arbitrary",), vmem_limit_bytes=VMEM_LIMIT),
        name="out_proj",
    )(mixed2, x2, w_bf, gain, r_hi, r_lo, r_b)


def _row_copy(src_hbm, idx_ref, dst, sem, r):
    return pltpu.make_async_copy(src_hbm.at[pl.ds(idx_ref[0, 0, r], 1)], dst.at[pl.ds(r, 1)], sem)


def _start_rows(src_hbm, idx_ref, dst, sem, n_rows):
    def body(j, carry):
        for k in range(ISSUE_UNROLL):
            _row_copy(src_hbm, idx_ref, dst, sem, j * ISSUE_UNROLL + k).start()
        return carry
    lax.fori_loop(0, n_rows // ISSUE_UNROLL, body, 0)


def _wait_rows(src_hbm, dst, sem, n_rows):
    pltpu.make_async_copy(src_hbm.at[pl.ds(0, n_rows)], dst, sem).wait()


def _dispatch_kernel(pad_lo_ref, pad_len_ref, nu_ref, slots_ref, h_ref, hs_hbm, stage, zeros, sem, pad_sem):
    i = pl.program_id(0)
    n = pl.num_programs(0)
    s = i % 2
    n_tiles = hs_hbm.shape[0] // TM_MOE
    min_used = (2 * n * TM_COMBINE) // TM_MOE

    def wait_stage(which):
        for _ in range(2):
            pltpu.make_async_copy(stage.at[which], hs_hbm.at[pl.ds(0, TM_COMBINE)], sem.at[which]).wait()

    def zero_copy(first_row, n_rows):
        return pltpu.make_async_copy(zeros.at[pl.ds(0, n_rows)], hs_hbm.at[pl.ds(first_row, n_rows)], pad_sem)

    def pad_copies(e):
        lo = pad_lo_ref[e]
        length = pad_len_ref[e]
        head = (-lo) & (PAD_ALIGN - 1)
        pairs = [(j < head, zero_copy(lo + j, 1)) for j in range(PAD_ALIGN - 1)]
        first = lo + head
        for p in PAD_PIECES:
            use = ((length - head) & p) != 0
            pairs.append((use, zero_copy(pl.multiple_of(first, PAD_ALIGN), p)))
            first = first + jnp.where(use, p, 0)
        return pairs

    def unused_tile_copies():
        half = TM_MOE // 2
        return [(j >= nu_ref[0], zero_copy(j * TM_MOE + k * half, half))
                for j in range(min_used, n_tiles) for k in range(2)]

    def for_each_zero_copy(action):
        def per_expert(e, carry):
            for cond, copy in pad_copies(e):
                pl.when(cond)(functools.partial(action, copy))
            return carry
        lax.fori_loop(0, N_EXPERTS, per_expert, 0)
        for cond, copy in unused_tile_copies():
            pl.when(cond)(functools.partial(action, copy))

    @pl.when(i == 0)
    def _():
        zeros[...] = jnp.zeros_like(zeros)
        for_each_zero_copy(lambda copy: copy.start())

    @pl.when(i >= 2)
    def _():
        wait_stage(s)

    stage[s] = h_ref[...]
    for r in range(2 * TM_COMBINE):
        pltpu.make_async_copy(stage.at[s, pl.ds(r % TM_COMBINE, 1)],
                              hs_hbm.at[pl.ds(slots_ref[0, 0, r], 1)], sem.at[s]).start(priority=r % 2)

    @pl.when(i == n - 1)
    def _():
        wait_stage(s)
        wait_stage(1 - s)
        for_each_zero_copy(lambda copy: copy.wait())


def _dispatch(h2, pad_lo, pad_len, n_used, slots3, n_slots):
    t = h2.shape[0]
    n = t // TM_COMBINE
    return pl.pallas_call(
        _dispatch_kernel,
        grid_spec=pltpu.PrefetchScalarGridSpec(
            num_scalar_prefetch=3,
            grid=(n,),
            in_specs=[
                pl.BlockSpec((1, 1, 2 * TM_COMBINE), lambda i, lo, ln, nu: (i, 0, 0),
                             memory_space=pltpu.SMEM),
                pl.BlockSpec((TM_COMBINE, D_MODEL), lambda i, lo, ln, nu: (i, 0)),
            ],
            out_specs=pl.BlockSpec(memory_space=pl.ANY),
            scratch_shapes=[
                pltpu.VMEM((2, TM_COMBINE, D_MODEL), F32),
                pltpu.VMEM((PAD_PIECES[0], D_MODEL), F32),
                pltpu.SemaphoreType.DMA((2,)),
                pltpu.SemaphoreType.DMA(()),
            ],
        ),
        out_shape=jax.ShapeDtypeStruct((n_slots, D_MODEL), F32),
        compiler_params=pltpu.CompilerParams(
            dimension_semantics=("arbitrary",), vmem_limit_bytes=VMEM_LIMIT),
        name="dispatch",
    )(pad_lo, pad_len, n_used, slots3, h2)


def _moe_kernel(te_ref, grp_ref, nxt_ref, nu_ref, x_ref, wg_hbm, wu_hbm, wd_hbm,
                y_ref, wg_st, wu_st, wd_st, wsem, wg_bf, wu_bf, wd_bf):
    i = pl.program_id(0)
    n_used = nu_ref[0]

    def weight_copies(e, par):
        return (pltpu.make_async_copy(wg_hbm.at[e], wg_st.at[par], wsem.at[par]),
                pltpu.make_async_copy(wu_hbm.at[e], wu_st.at[par], wsem.at[par]),
                pltpu.make_async_copy(wd_hbm.at[e], wd_st.at[par], wsem.at[par]))

    @pl.when(i == 0)
    def _():
        for c in weight_copies(te_ref[0], 0):
            c.start(priority=WEIGHT_DMA_PRIORITY)

    @pl.when(i < n_used)
    def _():
        @pl.when((i == 0) | (te_ref[i] != te_ref[jnp.maximum(i - 1, 0)]))
        def _():
            par = grp_ref[i] % 2
            for c in weight_copies(te_ref[i], par):
                c.wait()

            @pl.when(nxt_ref[i] >= 0)
            def _():
                for c in weight_copies(nxt_ref[i], 1 - par):
                    c.start(priority=WEIGHT_DMA_PRIORITY)

            wg_bf[...] = wg_st[par].astype(BF16)
            wu_bf[...] = wu_st[par].astype(BF16)
            wd_bf[...] = wd_st[par].astype(BF16)

        xb = x_ref[...].astype(BF16)
        g = jnp.dot(xb, wg_bf[...], preferred_element_type=F32)
        u = jnp.dot(xb, wu_bf[...], preferred_element_type=F32)
        a = (g * jax.nn.sigmoid(g) * u).astype(BF16)
        y_ref[...] = jnp.dot(a, wd_bf[...], preferred_element_type=F32)

    @pl.when(i >= n_used)
    def _():
        y_ref[...] = jnp.zeros_like(y_ref)


def _moe_ffn(hs, n_tiles, w_gate, w_up, w_down, tile_expert, tile_group, tile_next, n_used):
    hbm = pl.BlockSpec(memory_space=pl.ANY)
    tile = lambda i, te, gr, nx, nu: (i, 0)
    used_tile = lambda i, te, gr, nx, nu: (jnp.minimum(i, nu[0] - 1), 0)
    return pl.pallas_call(
        _moe_kernel,
        grid_spec=pltpu.PrefetchScalarGridSpec(
            num_scalar_prefetch=4,
            grid=(n_tiles,),
            in_specs=[pl.BlockSpec((TM_MOE, D_MODEL), used_tile), hbm, hbm, hbm],
            out_specs=pl.BlockSpec((TM_MOE, D_MODEL), tile),
            scratch_shapes=[
                pltpu.VMEM((2, D_MODEL, D_EXPERT), F32),
                pltpu.VMEM((2, D_MODEL, D_EXPERT), F32),
                pltpu.VMEM((2, D_EXPERT, D_MODEL), F32),
                pltpu.SemaphoreType.DMA((2,)),
                pltpu.VMEM((D_MODEL, D_EXPERT), BF16),
                pltpu.VMEM((D_MODEL, D_EXPERT), BF16),
                pltpu.VMEM((D_EXPERT, D_MODEL), BF16),
            ],
        ),
        out_shape=jax.ShapeDtypeStruct((n_tiles * TM_MOE, D_MODEL), F32),
        compiler_params=pltpu.CompilerParams(
            dimension_semantics=("arbitrary",), vmem_limit_bytes=VMEM_LIMIT),
        name="moe_ffn",
    )(tile_expert, tile_group, tile_next, n_used, hs, w_gate, w_up, w_down)


def _combine_kernel(first_ref, ahead_ref, x_ref, w_ref, y_hbm, o_ref, buf, sem):
    i = pl.program_id(0)
    n = pl.num_programs(0)
    rows = 2 * TM_COMBINE

    @pl.when(i == 0)
    def _():
        _start_rows(y_hbm, first_ref, buf.at[0], sem.at[0], rows)

    @pl.when(i + 1 < n)
    def _():
        nxt = (i + 1) % 2
        for r in range(rows):
            _row_copy(y_hbm, ahead_ref, buf.at[nxt], sem.at[nxt], r).start(priority=r % 2)

    slot = i % 2
    _wait_rows(y_hbm, buf.at[slot], sem.at[slot], rows)
    w = w_ref[...]
    o_ref[...] = (x_ref[...] + w[:, 0:1] * buf[slot, :TM_COMBINE]
                  + w[:, 1:2] * buf[slot, TM_COMBINE:])


def _combine(x1, rw, y, slots3):
    t = x1.shape[0]
    n = t // TM_COMBINE
    row = lambda i: (i, 0)
    smem_tile = lambda index_map: pl.BlockSpec((1, 1, 2 * TM_COMBINE), index_map,
                                               memory_space=pltpu.SMEM)
    return pl.pallas_call(
        _combine_kernel,
        grid=(n,),
        in_specs=[
            smem_tile(lambda i: (0, 0, 0)),
            smem_tile(lambda i: (jnp.minimum(i + 1, n - 1), 0, 0)),
            pl.BlockSpec((TM_COMBINE, D_MODEL), row),
            pl.BlockSpec((TM_COMBINE, ROUTER_LANES), row),
            pl.BlockSpec(memory_space=pl.ANY),
        ],
        out_specs=pl.BlockSpec((TM_COMBINE, D_MODEL), row),
        out_shape=jax.ShapeDtypeStruct((t, D_MODEL), F32),
        scratch_shapes=[
            pltpu.VMEM((2, 2 * TM_COMBINE, D_MODEL), F32),
            pltpu.SemaphoreType.DMA((2,)),
        ],
        compiler_params=pltpu.CompilerParams(
            dimension_semantics=("arbitrary",), vmem_limit_bytes=VMEM_LIMIT),
        name="combine",
    )(slots3, slots3, x1, rw, y)


def _slot_tables(l, ri, counts_row, n_tiles):
    t = ri.shape[0]
    experts = jnp.arange(N_EXPERTS, dtype=I32)
    counts = counts_row[0, :N_EXPERTS]
    tiles_per = (counts + TM_MOE - 1) // TM_MOE
    tile_end = jnp.cumsum(tiles_per)
    start = (tile_end - tiles_per) * TM_MOE
    n_used = tile_end[-1:].astype(I32)
    nonempty = counts > 0
    group_of = jnp.cumsum(nonempty.astype(I32)) - 1
    later = (experts[None, :] > experts[:, None]) & nonempty[None, :]
    next_of = jnp.min(jnp.where(later, experts[None, :], N_EXPERTS), axis=1)
    next_of = jnp.where(next_of < N_EXPERTS, next_of + l * N_EXPERTS, -1)
    tile_idx = jnp.arange(n_tiles, dtype=I32)
    tile_e = jnp.minimum(jnp.sum((tile_end[None, :] <= tile_idx[:, None]).astype(I32), axis=1),
                         N_EXPERTS - 1)
    pick = tile_e[:, None] == experts[None, :]
    tile_group = jnp.sum(jnp.where(pick, group_of[None, :], 0), axis=1)
    tile_next = jnp.sum(jnp.where(pick, next_of[None, :], 0), axis=1)
    tile_expert = tile_e + l * N_EXPERTS
    eid = ri[:, 0:2]
    slot = ri[:, 2:4] + jnp.sum(jnp.where(eid[..., None] == experts, start, 0), axis=-1)
    slot2 = slot.reshape(t // TM_COMBINE, TM_COMBINE, 2)
    slots3 = jnp.swapaxes(slot2, 1, 2).reshape(t // TM_COMBINE, 1, 2 * TM_COMBINE)
    tiles = (tile_expert.astype(I32), tile_group.astype(I32), tile_next.astype(I32), n_used)
    pad_lo = (start + counts).astype(I32)
    pad_len = (tiles_per * TM_MOE - counts).astype(I32)
    return pad_lo, pad_len, tiles, slots3


def _rope_freq_row():
    inv_freq = ROPE_THETA ** (-(jnp.arange(0, ROT_DIM, 2, dtype=F32) / ROT_DIM))
    return jnp.concatenate(
        [inv_freq, inv_freq, jnp.zeros((HEAD_DIM - ROT_DIM,), F32)]).reshape(1, HEAD_DIM)


def _router_operands(w_rg, b_rg, w_re, b_re):
    depth = w_rg.shape[0]
    pad = ROUTER_LANES - N_EXPERTS - N_GROUPS
    w = jnp.concatenate([w_re, w_rg, jnp.zeros((depth, D_MODEL, pad), F32)], axis=2)
    b = jnp.concatenate([b_re, b_rg, jnp.zeros((depth, pad), F32)], axis=1)
    hi = w.astype(BF16)
    lo = (w - hi.astype(F32)).astype(BF16)
    return hi, lo, b


def kernel(x, positions, norm_mix, w_in, w_pool, pool_scale, q_norm, k_norm, sink, branch_gain_pool,
           branch_gain_attn, w_out, norm_ffn, w_router_group, b_router_group, w_router_expert,
           b_router_expert, w_gate, w_up, w_down):
    b, s, d = x.shape
    t = b * s
    depth = norm_mix.shape[0]
    x2 = x.reshape(t, d)
    pos2 = positions.reshape(t, 1)
    freq = _rope_freq_row()
    w_in_bf = w_in.astype(BF16)
    w_out_bf = w_out.astype(BF16)
    w_pool_bf = w_pool.astype(BF16)
    r_hi, r_lo, r_b = _router_operands(w_router_group, b_router_group, w_router_expert,
                                       b_router_expert)
    wg = w_gate.reshape(depth * N_EXPERTS, D_MODEL, D_EXPERT)
    wu = w_up.reshape(depth * N_EXPERTS, D_MODEL, D_EXPERT)
    wd = w_down.reshape(depth * N_EXPERTS, D_EXPERT, D_MODEL)
    n_tiles = (2 * t) // TM_MOE + N_EXPERTS
    rows = lambda v: v.reshape(depth, 1, v.shape[-1])
    for l in range(depth):
        u, qkv = _in_proj(l, x2, pos2, rows(norm_mix), w_in_bf, rows(q_norm), rows(k_norm), freq)
        mixed = _mixer(l, u.reshape(b, s, POOL_WIDTH), qkv.reshape(b, s, QKV_WIDTH), sink, w_pool_bf,
                       rows(pool_scale), rows(branch_gain_pool), rows(branch_gain_attn))
        x1, h, ri, rw, counts_row = _out_proj(l, mixed.reshape(t, D_MODEL), x2, w_out_bf,
                                              rows(norm_ffn), r_hi, r_lo, rows(r_b))
        pad_lo, pad_len, tiles, slots3 = _slot_tables(l, ri, counts_row, n_tiles)
        hs = _dispatch(h, pad_lo, pad_len, tiles[3], slots3, n_tiles * TM_MOE)
        y = _moe_ffn(hs, n_tiles, wg, wu, wd, *tiles)
        x2 = _combine(x1, rw, y, slots3)
    return x2.reshape(b, s, d)
```

```python
import functools

import jax
import jax.numpy as jnp
from jax import lax
from jax.experimental import pallas as pl
from jax.experimental.pallas import tpu as pltpu

F32 = jnp.float32
BF16 = jnp.bfloat16
I32 = jnp.int32

D_MODEL = 2048
POOL_WIDTH = 1024
POOL_WINDOWS = (2, 4, 8, 16)
POOL_GROUP = 256
HEAD_DIM = 128
N_Q_HEADS = 8
N_KV_HEADS = 2
Q_PER_KV = 4
ATTN_WIDTH = 1024
KV_WIDTH = 256
QKV_WIDTH = ATTN_WIDTH + 2 * KV_WIDTH
IN_WIDTH = POOL_WIDTH + QKV_WIDTH
WINDOW = 128
BLOCK = 128
ROPE_THETA = 500000.0
ROT_DIM = 32
ROT_HALF = ROT_DIM // 2
N_GROUPS = 4
EXPERTS_PER_GROUP = 8
N_EXPERTS = 32
D_EXPERT = 512
EPS = 1e-6
LOG2_E = 1.4426950408889634

LANES = 128
POOL_HALO = 8
ROUTER_LANES = LANES
TM_PROJ = 512
TM_IN = 512
SUB_ROWS = 256
MIX_BLOCKS = 4
TM_MOE = 256
TM_DISPATCH = 512
TM_COMBINE = 256
ISSUE_UNROLL = 16
PAD_ALIGN = 8
PAD_PIECES = (128, 64, 32, 16, 8)
WEIGHT_DMA_PRIORITY = 1
VMEM_LIMIT = 56 * 1024 * 1024


def _rms(x, gain):
    return x * lax.rsqrt(jnp.mean(x * x, axis=-1, keepdims=True) + EPS) * gain


def _in_proj_kernel(x_ref, pos_ref, g_ref, w_ref, qn_ref, kn_ref, freq_ref, u_ref, qkv_ref):
    for sub in range(TM_IN // SUB_ROWS):
        rows = slice(sub * SUB_ROWS, (sub + 1) * SUB_ROWS)
        h = _rms(x_ref[rows], g_ref[0]).astype(BF16)
        z = jnp.dot(h, w_ref[0], preferred_element_type=F32)
        u_ref[rows] = z[:, :POOL_WIDTH]

        ang = pos_ref[rows].astype(F32) * freq_ref[...]
        cos = jnp.cos(ang)
        sin = jnp.sin(ang)
        lane = lax.broadcasted_iota(I32, ang.shape, 1)
        sin_hi = jnp.where(lane >= ROT_HALF, sin, 0.0)
        sin_lo = jnp.where(lane < ROT_HALF, -sin, 0.0)

        def norm_rope(t, gain):
            y = _rms(t, gain)
            return (y * cos + pltpu.roll(y, ROT_HALF, 1) * sin_hi
                    + pltpu.roll(y, HEAD_DIM - ROT_HALF, 1) * sin_lo)

        for hd in range(N_Q_HEADS + N_KV_HEADS):
            gain = qn_ref[0] if hd < N_Q_HEADS else kn_ref[0]
            src = POOL_WIDTH + hd * HEAD_DIM
            dst = hd * HEAD_DIM
            qkv_ref[rows, dst:dst + HEAD_DIM] = norm_rope(z[:, src:src + HEAD_DIM], gain).astype(BF16)
        v0 = POOL_WIDTH + ATTN_WIDTH + KV_WIDTH
        qkv_ref[rows, ATTN_WIDTH + KV_WIDTH:] = z[:, v0:].astype(BF16)


def _in_proj(l, x2, pos2, gain, w_bf, qn, kn, freq):
    t = x2.shape[0]
    row = lambda i: (i, 0)
    layer = lambda i: (l, 0, 0)
    return pl.pallas_call(
        _in_proj_kernel,
        grid=(t // TM_IN,),
        in_specs=[
            pl.BlockSpec((TM_IN, D_MODEL), row),
            pl.BlockSpec((TM_IN, 1), row),
            pl.BlockSpec((1, 1, D_MODEL), layer),
            pl.BlockSpec((1, D_MODEL, IN_WIDTH), layer),
            pl.BlockSpec((1, 1, HEAD_DIM), layer),
            pl.BlockSpec((1, 1, HEAD_DIM), layer),
            pl.BlockSpec((1, HEAD_DIM), lambda i: (0, 0)),
        ],
        out_specs=[
            pl.BlockSpec((TM_IN, POOL_WIDTH), row),
            pl.BlockSpec((TM_IN, QKV_WIDTH), row),
        ],
        out_shape=[
            jax.ShapeDtypeStruct((t, POOL_WIDTH), F32),
            jax.ShapeDtypeStruct((t, QKV_WIDTH), BF16),
        ],
        compiler_params=pltpu.CompilerParams(
            dimension_semantics=("arbitrary",), vmem_limit_bytes=VMEM_LIMIT),
        name="in_proj",
    )(x2, pos2, gain, w_bf, qn, kn, freq)


def _mixer_kernel(sink_ref, u_ref, up_ref, un_ref, q_ref, kp_ref, kc_ref, kn_ref,
                  vp_ref, vc_ref, vn_ref, wp_ref, ps_ref, gp_ref, ga_ref, o_ref, *, seq, layer):
    n = pl.program_id(1)
    n_steps = pl.num_programs(1)
    step_rows = MIX_BLOCKS * BLOCK
    t0 = n * step_rows

    prev = jnp.where(n > 0, up_ref[0], 0.0)
    nxt = jnp.where(n < n_steps - 1, un_ref[0], 0.0)
    cur = u_ref[0]
    ext = jnp.concatenate([prev, cur, nxt], axis=0)
    rows = ext.shape[0]
    tpos = t0 + lax.broadcasted_iota(I32, (step_rows, LANES), 0)
    pooled = []
    for gi, w in enumerate(POOL_WINDOWS):
        sl = slice(gi * POOL_GROUP, (gi + 1) * POOL_GROUP)
        acc = ext[:, sl]
        span = 1
        while span < w:
            acc = acc + pltpu.roll(acc, span, 0)
            span *= 2
        lead = w // 2 - 1
        if lead:
            acc = pltpu.roll(acc, rows - lead, 0)
        wsum = acc[POOL_HALO:POOL_HALO + step_rows]
        lo = jnp.clip(tpos - w // 2, 0, seq)
        hi = jnp.clip(tpos + w // 2, 0, seq)
        inv = 1.0 / (hi - lo).astype(F32)
        y = wsum * jnp.concatenate([inv] * (POOL_GROUP // LANES), axis=1) - cur[:, sl]
        y = jnp.dot(y.astype(BF16), wp_ref[0, gi], preferred_element_type=F32)
        pooled.append(y)
    pool = jnp.concatenate(pooled, axis=1) * ps_ref[0]
    o_ref[0, :, :POOL_WIDTH] = _rms(pool, gp_ref[0]).astype(BF16)

    r = lax.broadcasted_iota(I32, (Q_PER_KV * BLOCK, BLOCK), 0) % BLOCK
    c = lax.broadcasted_iota(I32, (Q_PER_KV * BLOCK, BLOCK), 1)
    in_prev = c >= r
    in_next = c <= r
    to_log2 = (HEAD_DIM ** -0.5) * LOG2_E
    kcat, vcat = [], []
    for kh in range(N_KV_HEADS):
        ks = slice(kh * HEAD_DIM, (kh + 1) * HEAD_DIM)
        kcat.append(jnp.concatenate([kp_ref[0, :, ks], kc_ref[0, :, ks], kn_ref[0, :, ks]], axis=0))
        vcat.append(jnp.concatenate([vp_ref[0, :, ks], vc_ref[0, :, ks], vn_ref[0, :, ks]], axis=0))
    for j in range(MIX_BLOCKS):
        blk = n * MIX_BLOCKS + j
        keep_prev = jnp.logical_and(in_prev, blk > 0)
        keep_next = jnp.logical_and(in_next, blk < seq // BLOCK - 1)
        qrows = slice(j * BLOCK, (j + 1) * BLOCK)
        krows = slice(j * BLOCK, (j + 3) * BLOCK)
        heads = []
        for kh in range(N_KV_HEADS):
            q = jnp.concatenate(
                [q_ref[0, qrows, (kh * Q_PER_KV + g) * HEAD_DIM:(kh * Q_PER_KV + g + 1) * HEAD_DIM]
                 for g in range(Q_PER_KV)], axis=0)
            s = lax.dot_general(q, kcat[kh][krows], (((1,), (1,)), ((), ())),
                                preferred_element_type=F32)
            s = jnp.concatenate([
                jnp.where(keep_prev, s[:, :BLOCK] * to_log2, -jnp.inf),
                s[:, BLOCK:2 * BLOCK] * to_log2,
                jnp.where(keep_next, s[:, 2 * BLOCK:] * to_log2, -jnp.inf)], axis=1)
            ps, inv_dens = [], []
            for g in range(Q_PER_KV):
                sg = s[g * BLOCK:(g + 1) * BLOCK]
                sink = sink_ref[layer, kh * Q_PER_KV + g] * LOG2_E
                m = jnp.maximum(jnp.max(sg, axis=-1, keepdims=True), sink)
                pg = jnp.exp2(sg - m)
                inv_dens.append(1.0 / (jnp.sum(pg, axis=-1, keepdims=True) + jnp.exp2(sink - m)))
                ps.append(pg.astype(BF16))
            o = jnp.dot(jnp.concatenate(ps, axis=0), vcat[kh][krows], preferred_element_type=F32)
            heads.extend(o[g * BLOCK:(g + 1) * BLOCK] * inv_dens[g] for g in range(Q_PER_KV))
        attn = jnp.concatenate(heads, axis=1)
        o_ref[0, qrows, POOL_WIDTH:] = _rms(attn, ga_ref[0]).astype(BF16)


def _mixer(l, u3, qkv3, sink, wp_bf, pscale, gpool, gattn):
    b, s, _ = u3.shape
    nb = s // BLOCK
    step_rows = MIX_BLOCKS * BLOCK
    n_steps = s // step_rows
    halo_per_step = step_rows // POOL_HALO
    n_halo = s // POOL_HALO
    kcol = ATTN_WIDTH // KV_WIDTH
    vcol = kcol + 1
    layer = lambda bi, n: (l, 0, 0)
    prev_blk = lambda n: jnp.maximum(n * MIX_BLOCKS - 1, 0)
    next_blk = lambda n: jnp.minimum((n + 1) * MIX_BLOCKS, nb - 1)
    edge_spec = lambda blk, col: pl.BlockSpec((1, BLOCK, KV_WIDTH), lambda bi, n: (bi, blk(n), col))
    body_spec = lambda col: pl.BlockSpec((1, step_rows, KV_WIDTH), lambda bi, n: (bi, n, col))
    return pl.pallas_call(
        functools.partial(_mixer_kernel, seq=s, layer=l),
        grid=(b, n_steps),
        in_specs=[
            pl.BlockSpec(memory_space=pltpu.SMEM),
            pl.BlockSpec((1, step_rows, POOL_WIDTH), lambda bi, n: (bi, n, 0)),
            pl.BlockSpec((1, POOL_HALO, POOL_WIDTH),
                         lambda bi, n: (bi, jnp.maximum(n * halo_per_step - 1, 0), 0)),
            pl.BlockSpec((1, POOL_HALO, POOL_WIDTH),
                         lambda bi, n: (bi, jnp.minimum((n + 1) * halo_per_step, n_halo - 1), 0)),
            pl.BlockSpec((1, step_rows, ATTN_WIDTH), lambda bi, n: (bi, n, 0)),
            edge_spec(prev_blk, kcol), body_spec(kcol), edge_spec(next_blk, kcol),
            edge_spec(prev_blk, vcol), body_spec(vcol), edge_spec(next_blk, vcol),
            pl.BlockSpec((1, len(POOL_WINDOWS), POOL_GROUP, POOL_GROUP), lambda bi, n: (l, 0, 0, 0)),
            pl.BlockSpec((1, 1, POOL_WIDTH), layer),
            pl.BlockSpec((1, 1, POOL_WIDTH), layer),
            pl.BlockSpec((1, 1, ATTN_WIDTH), layer),
        ],
        out_specs=pl.BlockSpec((1, step_rows, POOL_WIDTH + ATTN_WIDTH), lambda bi, n: (bi, n, 0)),
        out_shape=jax.ShapeDtypeStruct((b, s, POOL_WIDTH + ATTN_WIDTH), BF16),
        compiler_params=pltpu.CompilerParams(
            dimension_semantics=("arbitrary", "arbitrary"), vmem_limit_bytes=VMEM_LIMIT),
        name="mixer",
    )(sink, u3, u3, u3, qkv3, qkv3, qkv3, qkv3, qkv3, qkv3, qkv3, wp_bf, pscale, gpool, gattn)


def _split_dot(a, b_hi, b_lo):
    a_hi = a.astype(BF16)
    a_lo = (a - a_hi.astype(F32)).astype(BF16)
    return (jnp.dot(a_hi, b_hi, preferred_element_type=F32)
            + jnp.dot(a_lo, b_hi, preferred_element_type=F32)
            + jnp.dot(a_hi, b_lo, preferred_element_type=F32))


def _out_proj_kernel(m_ref, x_ref, w_ref, g_ref, rh_ref, rl_ref, rb_ref,
                     x1_ref, h_ref, ri_ref, rw_ref, cnt_ref, run_ref):
    @pl.when(pl.program_id(0) == 0)
    def _():
        run_ref[...] = jnp.zeros_like(run_ref)

    lane = lax.broadcasted_iota(I32, (SUB_ROWS, ROUTER_LANES), 1)
    far = jnp.int32(ROUTER_LANES)
    neg = -jnp.inf
    lower = (lax.broadcasted_iota(I32, (SUB_ROWS, SUB_ROWS), 1)
             < lax.broadcasted_iota(I32, (SUB_ROWS, SUB_ROWS), 0)).astype(BF16)

    def top(vals):
        best = jnp.max(vals, axis=-1, keepdims=True)
        idx = jnp.min(jnp.where(vals == best, lane, far), axis=-1, keepdims=True)
        return best, idx

    run = run_ref[...]
    for sub in range(TM_PROJ // SUB_ROWS):
        rows = slice(sub * SUB_ROWS, (sub + 1) * SUB_ROWS)
        x1 = x_ref[rows] + jnp.dot(m_ref[rows], w_ref[0], preferred_element_type=F32)
        x1_ref[rows] = x1
        h = _rms(x1, g_ref[0])
        h_ref[rows] = h
        logits = _split_dot(h, rh_ref[0], rl_ref[0]) + rb_ref[0]

        gl = jnp.where((lane >= N_EXPERTS) & (lane < N_EXPERTS + N_GROUPS), logits, neg)
        gmax, gidx = top(gl)
        g_w = 1.0 / jnp.sum(jnp.exp(gl - gmax), axis=-1, keepdims=True)
        in_group = (lane < N_EXPERTS) & ((lane >> 3) == gidx - N_EXPERTS)
        el = jnp.where(in_group, logits, neg)
        m1, i1 = top(el)
        m2, i2 = top(jnp.where(lane == i1, neg, el))
        esum = jnp.sum(jnp.exp(el - m1), axis=-1, keepdims=True)
        p1 = 1.0 / esum
        p2 = jnp.exp(m2 - m1) / esum
        w1 = g_w * (p1 / (p1 + p2))
        w2 = g_w * (p2 / (p1 + p2))

        pick1 = lane == i1
        pick2 = lane == i2
        picks = (pick1 | pick2).astype(BF16)
        before = jnp.dot(lower, picks, preferred_element_type=F32) + run
        r1 = jnp.sum(jnp.where(pick1, before, 0.0), axis=-1, keepdims=True).astype(I32)
        r2 = jnp.sum(jnp.where(pick2, before, 0.0), axis=-1, keepdims=True).astype(I32)
        run = run + jnp.sum(picks.astype(F32), axis=0, keepdims=True)

        ri_ref[rows] = jnp.where(lane == 0, i1, jnp.where(lane == 1, i2,
                                 jnp.where(lane == 2, r1, jnp.where(lane == 3, r2, 0))))
        rw_ref[rows] = jnp.where(lane == 0, w1, jnp.where(lane == 1, w2, 0.0))
    run_ref[...] = run
    cnt_ref[...] = run.astype(I32)


def _out_proj(l, mixed2, x2, w_bf, gain, r_hi, r_lo, r_b):
    t = x2.shape[0]
    row = lambda i: (i, 0)
    layer = lambda i: (l, 0, 0)
    return pl.pallas_call(
        _out_proj_kernel,
        grid=(t // TM_PROJ,),
        in_specs=[
            pl.BlockSpec((TM_PROJ, D_MODEL), row),
            pl.BlockSpec((TM_PROJ, D_MODEL), row),
            pl.BlockSpec((1, D_MODEL, D_MODEL), layer),
            pl.BlockSpec((1, 1, D_MODEL), layer),
            pl.BlockSpec((1, D_MODEL, ROUTER_LANES), layer),
            pl.BlockSpec((1, D_MODEL, ROUTER_LANES), layer),
            pl.BlockSpec((1, 1, ROUTER_LANES), layer),
        ],
        out_specs=[
            pl.BlockSpec((TM_PROJ, D_MODEL), row),
            pl.BlockSpec((TM_PROJ, D_MODEL), row),
            pl.BlockSpec((TM_PROJ, ROUTER_LANES), row),
            pl.BlockSpec((TM_PROJ, ROUTER_LANES), row),
            pl.BlockSpec((1, ROUTER_LANES), lambda i: (0, 0)),
        ],
        out_shape=[
            jax.ShapeDtypeStruct((t, D_MODEL), F32),
            jax.ShapeDtypeStruct((t, D_MODEL), F32),
            jax.ShapeDtypeStruct((t, ROUTER_LANES), I32),
            jax.ShapeDtypeStruct((t, ROUTER_LANES), F32),
            jax.ShapeDtypeStruct((1, ROUTER_LANES), I32),
        ],
        scratch_shapes=[pltpu.VMEM((1, ROUTER_LANES), F32)],
        compiler_params=pltpu.CompilerParams(
            dimension_semantics=("arbitrary",), vmem_limit_bytes=VMEM_LIMIT),
        name="out_proj",
    )(mixed2, x2, w_bf, gain, r_hi, r_lo, r_b)


def _row_copy(src_hbm, idx_ref, dst, sem, r):
    return pltpu.make_async_copy(src_hbm.at[pl.ds(idx_ref[0, 0, r], 1)], dst.at[pl.ds(r, 1)], sem)


def _start_rows(src_hbm, idx_ref, dst, sem, n_rows):
    def body(j, carry):
        for k in range(ISSUE_UNROLL):
            _row_copy(src_hbm, idx_ref, dst, sem, j * ISSUE_UNROLL + k).start()
        return carry
    lax.fori_loop(0, n_rows // ISSUE_UNROLL, body, 0)


def _wait_rows(src_hbm, dst, sem, n_rows):
    pltpu.make_async_copy(src_hbm.at[pl.ds(0, n_rows)], dst, sem).wait()


def _dispatch_kernel(pad_lo_ref, pad_len_ref, nu_ref, slots_ref, h_ref, hs_hbm, stage, zeros, sem, pad_sem):
    i = pl.program_id(0)
    n = pl.num_programs(0)
    s = i % 2
    n_tiles = hs_hbm.shape[0] // TM_MOE
    min_used = (2 * n * TM_DISPATCH) // TM_MOE

    def wait_stage(which):
        for _ in range(2):
            pltpu.make_async_copy(stage.at[which], hs_hbm.at[pl.ds(0, TM_DISPATCH)], sem.at[which]).wait()

    def zero_copy(first_row, n_rows):
        return pltpu.make_async_copy(zeros.at[pl.ds(0, n_rows)], hs_hbm.at[pl.ds(first_row, n_rows)], pad_sem)

    def pad_copies(e):
        lo = pad_lo_ref[e]
        length = pad_len_ref[e]
        head = (-lo) & (PAD_ALIGN - 1)
        pairs = [(j < head, zero_copy(lo + j, 1)) for j in range(PAD_ALIGN - 1)]
        first = lo + head
        for p in PAD_PIECES:
            use = ((length - head) & p) != 0
            pairs.append((use, zero_copy(pl.multiple_of(first, PAD_ALIGN), p)))
            first = first + jnp.where(use, p, 0)
        return pairs

    def unused_tile_copies():
        half = TM_MOE // 2
        return [(j >= nu_ref[0], zero_copy(j * TM_MOE + k * half, half))
                for j in range(min_used, n_tiles) for k in range(2)]

    def for_each_zero_copy(action):
        def per_expert(e, carry):
            for cond, copy in pad_copies(e):
                pl.when(cond)(functools.partial(action, copy))
            return carry
        lax.fori_loop(0, N_EXPERTS, per_expert, 0)
        for cond, copy in unused_tile_copies():
            pl.when(cond)(functools.partial(action, copy))

    @pl.when(i == 0)
    def _():
        zeros[...] = jnp.zeros_like(zeros)
        for_each_zero_copy(lambda copy: copy.start())

    @pl.when(i >= 2)
    def _():
        wait_stage(s)

    stage[s] = h_ref[...]
    for r in range(2 * TM_DISPATCH):
        pltpu.make_async_copy(stage.at[s, pl.ds(r % TM_DISPATCH, 1)],
                              hs_hbm.at[pl.ds(slots_ref[0, 0, r], 1)], sem.at[s]).start(priority=r % 2)

    @pl.when(i == n - 1)
    def _():
        wait_stage(s)
        wait_stage(1 - s)
        for_each_zero_copy(lambda copy: copy.wait())


def _dispatch(h2, pad_lo, pad_len, n_used, slots3, n_slots):
    t = h2.shape[0]
    n = t // TM_DISPATCH
    return pl.pallas_call(
        _dispatch_kernel,
        grid_spec=pltpu.PrefetchScalarGridSpec(
            num_scalar_prefetch=3,
            grid=(n,),
            in_specs=[
                pl.BlockSpec((1, 1, 2 * TM_DISPATCH), lambda i, lo, ln, nu: (i, 0, 0),
                             memory_space=pltpu.SMEM),
                pl.BlockSpec((TM_DISPATCH, D_MODEL), lambda i, lo, ln, nu: (i, 0)),
            ],
            out_specs=pl.BlockSpec(memory_space=pl.ANY),
            scratch_shapes=[
                pltpu.VMEM((2, TM_DISPATCH, D_MODEL), F32),
                pltpu.VMEM((PAD_PIECES[0], D_MODEL), F32),
                pltpu.SemaphoreType.DMA((2,)),
                pltpu.SemaphoreType.DMA(()),
            ],
        ),
        out_shape=jax.ShapeDtypeStruct((n_slots, D_MODEL), F32),
        compiler_params=pltpu.CompilerParams(
            dimension_semantics=("arbitrary",), vmem_limit_bytes=VMEM_LIMIT),
        name="dispatch",
    )(pad_lo, pad_len, n_used, slots3, h2)


def _moe_kernel(te_ref, grp_ref, nxt_ref, nu_ref, x_ref, wg_hbm, wu_hbm, wd_hbm,
                y_ref, wg_st, wu_st, wd_st, wsem, wg_bf, wu_bf, wd_bf):
    i = pl.program_id(0)
    n_used = nu_ref[0]

    def weight_copies(e, par):
        return (pltpu.make_async_copy(wg_hbm.at[e], wg_st.at[par], wsem.at[par]),
                pltpu.make_async_copy(wu_hbm.at[e], wu_st.at[par], wsem.at[par]),
                pltpu.make_async_copy(wd_hbm.at[e], wd_st.at[par], wsem.at[par]))

    @pl.when(i == 0)
    def _():
        for c in weight_copies(te_ref[0], 0):
            c.start(priority=WEIGHT_DMA_PRIORITY)

    @pl.when(i < n_used)
    def _():
        @pl.when((i == 0) | (te_ref[i] != te_ref[jnp.maximum(i - 1, 0)]))
        def _():
            par = grp_ref[i] % 2
            for c in weight_copies(te_ref[i], par):
                c.wait()

            @pl.when(nxt_ref[i] >= 0)
            def _():
                for c in weight_copies(nxt_ref[i], 1 - par):
                    c.start(priority=WEIGHT_DMA_PRIORITY)

            wg_bf[...] = wg_st[par].astype(BF16)
            wu_bf[...] = wu_st[par].astype(BF16)
            wd_bf[...] = wd_st[par].astype(BF16)

        xb = x_ref[...].astype(BF16)
        g = jnp.dot(xb, wg_bf[...], preferred_element_type=F32)
        u = jnp.dot(xb, wu_bf[...], preferred_element_type=F32)
        a = (g * jax.nn.sigmoid(g) * u).astype(BF16)
        y_ref[...] = jnp.dot(a, wd_bf[...], preferred_element_type=F32)

    @pl.when(i >= n_used)
    def _():
        y_ref[...] = jnp.zeros_like(y_ref)


def _moe_ffn(hs, n_tiles, w_gate, w_up, w_down, tile_expert, tile_group, tile_next, n_used):
    hbm = pl.BlockSpec(memory_space=pl.ANY)
    tile = lambda i, te, gr, nx, nu: (i, 0)
    used_tile = lambda i, te, gr, nx, nu: (jnp.minimum(i, nu[0] - 1), 0)
    return pl.pallas_call(
        _moe_kernel,
        grid_spec=pltpu.PrefetchScalarGridSpec(
            num_scalar_prefetch=4,
            grid=(n_tiles,),
            in_specs=[pl.BlockSpec((TM_MOE, D_MODEL), used_tile), hbm, hbm, hbm],
            out_specs=pl.BlockSpec((TM_MOE, D_MODEL), tile),
            scratch_shapes=[
                pltpu.VMEM((2, D_MODEL, D_EXPERT), F32),
                pltpu.VMEM((2, D_MODEL, D_EXPERT), F32),
                pltpu.VMEM((2, D_EXPERT, D_MODEL), F32),
                pltpu.SemaphoreType.DMA((2,)),
                pltpu.VMEM((D_MODEL, D_EXPERT), BF16),
                pltpu.VMEM((D_MODEL, D_EXPERT), BF16),
                pltpu.VMEM((D_EXPERT, D_MODEL), BF16),
            ],
        ),
        out_shape=jax.ShapeDtypeStruct((n_tiles * TM_MOE, D_MODEL), F32),
        compiler_params=pltpu.CompilerParams(
            dimension_semantics=("arbitrary",), vmem_limit_bytes=VMEM_LIMIT),
        name="moe_ffn",
    )(tile_expert, tile_group, tile_next, n_used, hs, w_gate, w_up, w_down)


def _combine_kernel(first_ref, ahead_ref, x_ref, w_ref, y_hbm, o_ref, buf, sem):
    i = pl.program_id(0)
    n = pl.num_programs(0)
    rows = 2 * TM_COMBINE

    @pl.when(i == 0)
    def _():
        _start_rows(y_hbm, first_ref, buf.at[0], sem.at[0], rows)

    @pl.when(i + 1 < n)
    def _():
        nxt = (i + 1) % 2
        for r in range(rows):
            _row_copy(y_hbm, ahead_ref, buf.at[nxt], sem.at[nxt], r).start(priority=r % 2)

    slot = i % 2
    _wait_rows(y_hbm, buf.at[slot], sem.at[slot], rows)
    w = w_ref[...]
    o_ref[...] = (x_ref[...] + w[:, 0:1] * buf[slot, :TM_COMBINE]
                  + w[:, 1:2] * buf[slot, TM_COMBINE:])


def _combine(x1, rw, y, slots3):
    t = x1.shape[0]
    n = t // TM_COMBINE
    row = lambda i: (i, 0)
    smem_tile = lambda index_map: pl.BlockSpec((1, 1, 2 * TM_COMBINE), index_map,
                                               memory_space=pltpu.SMEM)
    return pl.pallas_call(
        _combine_kernel,
        grid=(n,),
        in_specs=[
            smem_tile(lambda i: (0, 0, 0)),
            smem_tile(lambda i: (jnp.minimum(i + 1, n - 1), 0, 0)),
            pl.BlockSpec((TM_COMBINE, D_MODEL), row),
            pl.BlockSpec((TM_COMBINE, ROUTER_LANES), row),
            pl.BlockSpec(memory_space=pl.ANY),
        ],
        out_specs=pl.BlockSpec((TM_COMBINE, D_MODEL), row),
        out_shape=jax.ShapeDtypeStruct((t, D_MODEL), F32),
        scratch_shapes=[
            pltpu.VMEM((2, 2 * TM_COMBINE, D_MODEL), F32),
            pltpu.SemaphoreType.DMA((2,)),
        ],
        compiler_params=pltpu.CompilerParams(
            dimension_semantics=("arbitrary",), vmem_limit_bytes=VMEM_LIMIT),
        name="combine",
    )(slots3, slots3, x1, rw, y)


def _slot_tables(l, ri, counts_row, n_tiles):
    t = ri.shape[0]
    experts = jnp.arange(N_EXPERTS, dtype=I32)
    counts = counts_row[0, :N_EXPERTS]
    tiles_per = (counts + TM_MOE - 1) // TM_MOE
    tile_end = jnp.cumsum(tiles_per)
    start = (tile_end - tiles_per) * TM_MOE
    n_used = tile_end[-1:].astype(I32)
    nonempty = counts > 0
    group_of = jnp.cumsum(nonempty.astype(I32)) - 1
    later = (experts[None, :] > experts[:, None]) & nonempty[None, :]
    next_of = jnp.min(jnp.where(later, experts[None, :], N_EXPERTS), axis=1)
    next_of = jnp.where(next_of < N_EXPERTS, next_of + l * N_EXPERTS, -1)
    tile_idx = jnp.arange(n_tiles, dtype=I32)
    tile_e = jnp.minimum(jnp.sum((tile_end[None, :] <= tile_idx[:, None]).astype(I32), axis=1),
                         N_EXPERTS - 1)
    pick = tile_e[:, None] == experts[None, :]
    tile_group = jnp.sum(jnp.where(pick, group_of[None, :], 0), axis=1)
    tile_next = jnp.sum(jnp.where(pick, next_of[None, :], 0), axis=1)
    tile_expert = tile_e + l * N_EXPERTS
    eid = ri[:, 0:2]
    slot = ri[:, 2:4] + jnp.sum(jnp.where(eid[..., None] == experts, start, 0), axis=-1)
    def by_tile(tm):
        return jnp.swapaxes(slot.reshape(t // tm, tm, 2), 1, 2).reshape(t // tm, 1, 2 * tm)
    slots3 = (by_tile(TM_DISPATCH), by_tile(TM_COMBINE))
    tiles = (tile_expert.astype(I32), tile_group.astype(I32), tile_next.astype(I32), n_used)
    pad_lo = (start + counts).astype(I32)
    pad_len = (tiles_per * TM_MOE - counts).astype(I32)
    return pad_lo, pad_len, tiles, slots3


def _rope_freq_row():
    inv_freq = ROPE_THETA ** (-(jnp.arange(0, ROT_DIM, 2, dtype=F32) / ROT_DIM))
    return jnp.concatenate(
        [inv_freq, inv_freq, jnp.zeros((HEAD_DIM - ROT_DIM,), F32)]).reshape(1, HEAD_DIM)


def _router_operands(w_rg, b_rg, w_re, b_re):
    depth = w_rg.shape[0]
    pad = ROUTER_LANES - N_EXPERTS - N_GROUPS
    w = jnp.concatenate([w_re, w_rg, jnp.zeros((depth, D_MODEL, pad), F32)], axis=2)
    b = jnp.concatenate([b_re, b_rg, jnp.zeros((depth, pad), F32)], axis=1)
    hi = w.astype(BF16)
    lo = (w - hi.astype(F32)).astype(BF16)
    return hi, lo, b


def kernel(x, positions, norm_mix, w_in, w_pool, pool_scale, q_norm, k_norm, sink, branch_gain_pool,
           branch_gain_attn, w_out, norm_ffn, w_router_group, b_router_group, w_router_expert,
           b_router_expert, w_gate, w_up, w_down):
    b, s, d = x.shape
    t = b * s
    depth = norm_mix.shape[0]
    x2 = x.reshape(t, d)
    pos2 = positions.reshape(t, 1)
    freq = _rope_freq_row()
    w_in_bf = w_in.astype(BF16)
    w_out_bf = w_out.astype(BF16)
    w_pool_bf = w_pool.astype(BF16)
    r_hi, r_lo, r_b = _router_operands(w_router_group, b_router_group, w_router_expert,
                                       b_router_expert)
    wg = w_gate.reshape(depth * N_EXPERTS, D_MODEL, D_EXPERT)
    wu = w_up.reshape(depth * N_EXPERTS, D_MODEL, D_EXPERT)
    wd = w_down.reshape(depth * N_EXPERTS, D_EXPERT, D_MODEL)
    n_tiles = (2 * t) // TM_MOE + N_EXPERTS
    rows = lambda v: v.reshape(depth, 1, v.shape[-1])
    for l in range(depth):
        u, qkv = _in_proj(l, x2, pos2, rows(norm_mix), w_in_bf, rows(q_norm), rows(k_norm), freq)
        mixed = _mixer(l, u.reshape(b, s, POOL_WIDTH), qkv.reshape(b, s, QKV_WIDTH), sink, w_pool_bf,
                       rows(pool_scale), rows(branch_gain_pool), rows(branch_gain_attn))
        x1, h, ri, rw, counts_row = _out_proj(l, mixed.reshape(t, D_MODEL), x2, w_out_bf,
                                              rows(norm_ffn), r_hi, r_lo, rows(r_b))
        pad_lo, pad_len, tiles, slots3 = _slot_tables(l, ri, counts_row, n_tiles)
        hs = _dispatch(h, pad_lo, pad_len, tiles[3], slots3[0], n_tiles * TM_MOE)
        y = _moe_ffn(hs, n_tiles, wg, wu, wd, *tiles)
        x2 = _combine(x1, rw, y, slots3[1])
    return x2.reshape(b, s, d)
```

```python
import functools

import jax
import jax.numpy as jnp
from jax import lax
from jax.experimental import pallas as pl
from jax.experimental.pallas import tpu as pltpu

F32 = jnp.float32
BF16 = jnp.bfloat16
I32 = jnp.int32

D_MODEL = 2048
POOL_WIDTH = 1024
POOL_WINDOWS = (2, 4, 8, 16)
POOL_GROUP = 256
HEAD_DIM = 128
N_Q_HEADS = 8
N_KV_HEADS = 2
Q_PER_KV = 4
ATTN_WIDTH = 1024
KV_WIDTH = 256
QKV_WIDTH = ATTN_WIDTH + 2 * KV_WIDTH
IN_WIDTH = POOL_WIDTH + QKV_WIDTH
WINDOW = 128
BLOCK = 128
ROPE_THETA = 500000.0
ROT_DIM = 32
ROT_HALF = ROT_DIM // 2
N_GROUPS = 4
EXPERTS_PER_GROUP = 8
N_EXPERTS = 32
D_EXPERT = 512
EPS = 1e-6
LOG2_E = 1.4426950408889634

LANES = 128
POOL_HALO = 8
ROUTER_LANES = LANES
TM_PROJ = 512
TM_IN = 512
SUB_ROWS = 256
MIX_BLOCKS = 4
TM_MOE = 256
TM_DISPATCH = 1024
TM_COMBINE = 256
ISSUE_UNROLL = 16
PAD_ALIGN = 8
PAD_PIECES = (128, 64, 32, 16, 8)
WEIGHT_DMA_PRIORITY = 1
VMEM_LIMIT = 56 * 1024 * 1024


def _rms(x, gain):
    return x * lax.rsqrt(jnp.mean(x * x, axis=-1, keepdims=True) + EPS) * gain


def _in_proj_kernel(x_ref, pos_ref, g_ref, w_ref, qn_ref, kn_ref, freq_ref, u_ref, qkv_ref):
    for sub in range(TM_IN // SUB_ROWS):
        rows = slice(sub * SUB_ROWS, (sub + 1) * SUB_ROWS)
        h = _rms(x_ref[rows], g_ref[0]).astype(BF16)
        z = jnp.dot(h, w_ref[0], preferred_element_type=F32)
        u_ref[rows] = z[:, :POOL_WIDTH]

        ang = pos_ref[rows].astype(F32) * freq_ref[...]
        cos = jnp.cos(ang)
        sin = jnp.sin(ang)
        lane = lax.broadcasted_iota(I32, ang.shape, 1)
        sin_hi = jnp.where(lane >= ROT_HALF, sin, 0.0)
        sin_lo = jnp.where(lane < ROT_HALF, -sin, 0.0)

        def norm_rope(t, gain):
            y = _rms(t, gain)
            return (y * cos + pltpu.roll(y, ROT_HALF, 1) * sin_hi
                    + pltpu.roll(y, HEAD_DIM - ROT_HALF, 1) * sin_lo)

        for hd in range(N_Q_HEADS + N_KV_HEADS):
            gain = qn_ref[0] if hd < N_Q_HEADS else kn_ref[0]
            src = POOL_WIDTH + hd * HEAD_DIM
            dst = hd * HEAD_DIM
            qkv_ref[rows, dst:dst + HEAD_DIM] = norm_rope(z[:, src:src + HEAD_DIM], gain).astype(BF16)
        v0 = POOL_WIDTH + ATTN_WIDTH + KV_WIDTH
        qkv_ref[rows, ATTN_WIDTH + KV_WIDTH:] = z[:, v0:].astype(BF16)


def _in_proj(l, x2, pos2, gain, w_bf, qn, kn, freq):
    t = x2.shape[0]
    row = lambda i: (i, 0)
    layer = lambda i: (l, 0, 0)
    return pl.pallas_call(
        _in_proj_kernel,
        grid=(t // TM_IN,),
        in_specs=[
            pl.BlockSpec((TM_IN, D_MODEL), row),
            pl.BlockSpec((TM_IN, 1), row),
            pl.BlockSpec((1, 1, D_MODEL), layer),
            pl.BlockSpec((1, D_MODEL, IN_WIDTH), layer),
            pl.BlockSpec((1, 1, HEAD_DIM), layer),
            pl.BlockSpec((1, 1, HEAD_DIM), layer),
            pl.BlockSpec((1, HEAD_DIM), lambda i: (0, 0)),
        ],
        out_specs=[
            pl.BlockSpec((TM_IN, POOL_WIDTH), row),
            pl.BlockSpec((TM_IN, QKV_WIDTH), row),
        ],
        out_shape=[
            jax.ShapeDtypeStruct((t, POOL_WIDTH), F32),
            jax.ShapeDtypeStruct((t, QKV_WIDTH), BF16),
        ],
        compiler_params=pltpu.CompilerParams(
            dimension_semantics=("arbitrary",), vmem_limit_bytes=VMEM_LIMIT),
        name="in_proj",
    )(x2, pos2, gain, w_bf, qn, kn, freq)


def _mixer_kernel(sink_ref, u_ref, up_ref, un_ref, q_ref, kp_ref, kc_ref, kn_ref,
                  vp_ref, vc_ref, vn_ref, wp_ref, ps_ref, gp_ref, ga_ref, o_ref, *, seq, layer):
    n = pl.program_id(1)
    n_steps = pl.num_programs(1)
    step_rows = MIX_BLOCKS * BLOCK
    t0 = n * step_rows

    prev = jnp.where(n > 0, up_ref[0], 0.0)
    nxt = jnp.where(n < n_steps - 1, un_ref[0], 0.0)
    cur = u_ref[0]
    ext = jnp.concatenate([prev, cur, nxt], axis=0)
    rows = ext.shape[0]
    tpos = t0 + lax.broadcasted_iota(I32, (step_rows, LANES), 0)
    pooled = []
    for gi, w in enumerate(POOL_WINDOWS):
        sl = slice(gi * POOL_GROUP, (gi + 1) * POOL_GROUP)
        acc = ext[:, sl]
        span = 1
        while span < w:
            acc = acc + pltpu.roll(acc, span, 0)
            span *= 2
        lead = w // 2 - 1
        if lead:
            acc = pltpu.roll(acc, rows - lead, 0)
        wsum = acc[POOL_HALO:POOL_HALO + step_rows]
        lo = jnp.clip(tpos - w // 2, 0, seq)
        hi = jnp.clip(tpos + w // 2, 0, seq)
        inv = 1.0 / (hi - lo).astype(F32)
        y = wsum * jnp.concatenate([inv] * (POOL_GROUP // LANES), axis=1) - cur[:, sl]
        y = jnp.dot(y.astype(BF16), wp_ref[0, gi], preferred_element_type=F32)
        pooled.append(y)
    pool = jnp.concatenate(pooled, axis=1) * ps_ref[0]
    o_ref[0, :, :POOL_WIDTH] = _rms(pool, gp_ref[0]).astype(BF16)

    r = lax.broadcasted_iota(I32, (Q_PER_KV * BLOCK, BLOCK), 0) % BLOCK
    c = lax.broadcasted_iota(I32, (Q_PER_KV * BLOCK, BLOCK), 1)
    in_prev = c >= r
    in_next = c <= r
    to_log2 = (HEAD_DIM ** -0.5) * LOG2_E
    kcat, vcat = [], []
    for kh in range(N_KV_HEADS):
        ks = slice(kh * HEAD_DIM, (kh + 1) * HEAD_DIM)
        kcat.append(jnp.concatenate([kp_ref[0, :, ks], kc_ref[0, :, ks], kn_ref[0, :, ks]], axis=0))
        vcat.append(jnp.concatenate([vp_ref[0, :, ks], vc_ref[0, :, ks], vn_ref[0, :, ks]], axis=0))
    for j in range(MIX_BLOCKS):
        blk = n * MIX_BLOCKS + j
        keep_prev = jnp.logical_and(in_prev, blk > 0)
        keep_next = jnp.logical_and(in_next, blk < seq // BLOCK - 1)
        qrows = slice(j * BLOCK, (j + 1) * BLOCK)
        krows = slice(j * BLOCK, (j + 3) * BLOCK)
        heads = []
        for kh in range(N_KV_HEADS):
            q = jnp.concatenate(
                [q_ref[0, qrows, (kh * Q_PER_KV + g) * HEAD_DIM:(kh * Q_PER_KV + g + 1) * HEAD_DIM]
                 for g in range(Q_PER_KV)], axis=0)
            s = lax.dot_general(q, kcat[kh][krows], (((1,), (1,)), ((), ())),
                                preferred_element_type=F32)
            s = jnp.concatenate([
                jnp.where(keep_prev, s[:, :BLOCK] * to_log2, -jnp.inf),
                s[:, BLOCK:2 * BLOCK] * to_log2,
                jnp.where(keep_next, s[:, 2 * BLOCK:] * to_log2, -jnp.inf)], axis=1)
            ps, inv_dens = [], []
            for g in range(Q_PER_KV):
                sg = s[g * BLOCK:(g + 1) * BLOCK]
                sink = sink_ref[layer, kh * Q_PER_KV + g] * LOG2_E
                m = jnp.maximum(jnp.max(sg, axis=-1, keepdims=True), sink)
                pg = jnp.exp2(sg - m)
                inv_dens.append(1.0 / (jnp.sum(pg, axis=-1, keepdims=True) + jnp.exp2(sink - m)))
                ps.append(pg.astype(BF16))
            o = jnp.dot(jnp.concatenate(ps, axis=0), vcat[kh][krows], preferred_element_type=F32)
            heads.extend(o[g * BLOCK:(g + 1) * BLOCK] * inv_dens[g] for g in range(Q_PER_KV))
        attn = jnp.concatenate(heads, axis=1)
        o_ref[0, qrows, POOL_WIDTH:] = _rms(attn, ga_ref[0]).astype(BF16)


def _mixer(l, u3, qkv3, sink, wp_bf, pscale, gpool, gattn):
    b, s, _ = u3.shape
    nb = s // BLOCK
    step_rows = MIX_BLOCKS * BLOCK
    n_steps = s // step_rows
    halo_per_step = step_rows // POOL_HALO
    n_halo = s // POOL_HALO
    kcol = ATTN_WIDTH // KV_WIDTH
    vcol = kcol + 1
    layer = lambda bi, n: (l, 0, 0)
    prev_blk = lambda n: jnp.maximum(n * MIX_BLOCKS - 1, 0)
    next_blk = lambda n: jnp.minimum((n + 1) * MIX_BLOCKS, nb - 1)
    edge_spec = lambda blk, col: pl.BlockSpec((1, BLOCK, KV_WIDTH), lambda bi, n: (bi, blk(n), col))
    body_spec = lambda col: pl.BlockSpec((1, step_rows, KV_WIDTH), lambda bi, n: (bi, n, col))
    return pl.pallas_call(
        functools.partial(_mixer_kernel, seq=s, layer=l),
        grid=(b, n_steps),
        in_specs=[
            pl.BlockSpec(memory_space=pltpu.SMEM),
            pl.BlockSpec((1, step_rows, POOL_WIDTH), lambda bi, n: (bi, n, 0)),
            pl.BlockSpec((1, POOL_HALO, POOL_WIDTH),
                         lambda bi, n: (bi, jnp.maximum(n * halo_per_step - 1, 0), 0)),
            pl.BlockSpec((1, POOL_HALO, POOL_WIDTH),
                         lambda bi, n: (bi, jnp.minimum((n + 1) * halo_per_step, n_halo - 1), 0)),
            pl.BlockSpec((1, step_rows, ATTN_WIDTH), lambda bi, n: (bi, n, 0)),
            edge_spec(prev_blk, kcol), body_spec(kcol), edge_spec(next_blk, kcol),
            edge_spec(prev_blk, vcol), body_spec(vcol), edge_spec(next_blk, vcol),
            pl.BlockSpec((1, len(POOL_WINDOWS), POOL_GROUP, POOL_GROUP), lambda bi, n: (l, 0, 0, 0)),
            pl.BlockSpec((1, 1, POOL_WIDTH), layer),
            pl.BlockSpec((1, 1, POOL_WIDTH), layer),
            pl.BlockSpec((1, 1, ATTN_WIDTH), layer),
        ],
        out_specs=pl.BlockSpec((1, step_rows, POOL_WIDTH + ATTN_WIDTH), lambda bi, n: (bi, n, 0)),
        out_shape=jax.ShapeDtypeStruct((b, s, POOL_WIDTH + ATTN_WIDTH), BF16),
        compiler_params=pltpu.CompilerParams(
            dimension_semantics=("arbitrary", "arbitrary"), vmem_limit_bytes=VMEM_LIMIT),
        name="mixer",
    )(sink, u3, u3, u3, qkv3, qkv3, qkv3, qkv3, qkv3, qkv3, qkv3, wp_bf, pscale, gpool, gattn)


def _split_dot(a, b_hi, b_lo):
    a_hi = a.astype(BF16)
    a_lo = (a - a_hi.astype(F32)).astype(BF16)
    return (jnp.dot(a_hi, b_hi, preferred_element_type=F32)
            + jnp.dot(a_lo, b_hi, preferred_element_type=F32)
            + jnp.dot(a_hi, b_lo, preferred_element_type=F32))


def _out_proj_kernel(m_ref, x_ref, w_ref, g_ref, rh_ref, rl_ref, rb_ref,
                     x1_ref, h_ref, ri_ref, rw_ref, cnt_ref, run_ref):
    @pl.when(pl.program_id(0) == 0)
    def _():
        run_ref[...] = jnp.zeros_like(run_ref)

    lane = lax.broadcasted_iota(I32, (SUB_ROWS, ROUTER_LANES), 1)
    far = jnp.int32(ROUTER_LANES)
    neg = -jnp.inf
    lower = (lax.broadcasted_iota(I32, (SUB_ROWS, SUB_ROWS), 1)
             < lax.broadcasted_iota(I32, (SUB_ROWS, SUB_ROWS), 0)).astype(BF16)

    def top(vals):
        best = jnp.max(vals, axis=-1, keepdims=True)
        idx = jnp.min(jnp.where(vals == best, lane, far), axis=-1, keepdims=True)
        return best, idx

    run = run_ref[...]
    for sub in range(TM_PROJ // SUB_ROWS):
        rows = slice(sub * SUB_ROWS, (sub + 1) * SUB_ROWS)
        x1 = x_ref[rows] + jnp.dot(m_ref[rows], w_ref[0], preferred_element_type=F32)
        x1_ref[rows] = x1
        h = _rms(x1, g_ref[0])
        h_ref[rows] = h
        logits = _split_dot(h, rh_ref[0], rl_ref[0]) + rb_ref[0]

        gl = jnp.where((lane >= N_EXPERTS) & (lane < N_EXPERTS + N_GROUPS), logits, neg)
        gmax, gidx = top(gl)
        g_w = 1.0 / jnp.sum(jnp.exp(gl - gmax), axis=-1, keepdims=True)
        in_group = (lane < N_EXPERTS) & ((lane >> 3) == gidx - N_EXPERTS)
        el = jnp.where(in_group, logits, neg)
        m1, i1 = top(el)
        m2, i2 = top(jnp.where(lane == i1, neg, el))
        esum = jnp.sum(jnp.exp(el - m1), axis=-1, keepdims=True)
        p1 = 1.0 / esum
        p2 = jnp.exp(m2 - m1) / esum
        w1 = g_w * (p1 / (p1 + p2))
        w2 = g_w * (p2 / (p1 + p2))

        pick1 = lane == i1
        pick2 = lane == i2
        picks = (pick1 | pick2).astype(BF16)
        before = jnp.dot(lower, picks, preferred_element_type=F32) + run
        r1 = jnp.sum(jnp.where(pick1, before, 0.0), axis=-1, keepdims=True).astype(I32)
        r2 = jnp.sum(jnp.where(pick2, before, 0.0), axis=-1, keepdims=True).astype(I32)
        run = run + jnp.sum(picks.astype(F32), axis=0, keepdims=True)

        ri_ref[rows] = jnp.where(lane == 0, i1, jnp.where(lane == 1, i2,
                                 jnp.where(lane == 2, r1, jnp.where(lane == 3, r2, 0))))
        rw_ref[rows] = jnp.where(lane == 0, w1, jnp.where(lane == 1, w2, 0.0))
    run_ref[...] = run
    cnt_ref[...] = run.astype(I32)


def _out_proj(l, mixed2, x2, w_bf, gain, r_hi, r_lo, r_b):
    t = x2.shape[0]
    row = lambda i: (i, 0)
    layer = lambda i: (l, 0, 0)
    return pl.pallas_call(
        _out_proj_kernel,
        grid=(t // TM_PROJ,),
        in_specs=[
            pl.BlockSpec((TM_PROJ, D_MODEL), row),
            pl.BlockSpec((TM_PROJ, D_MODEL), row),
            pl.BlockSpec((1, D_MODEL, D_MODEL), layer),
            pl.BlockSpec((1, 1, D_MODEL), layer),
            pl.BlockSpec((1, D_MODEL, ROUTER_LANES), layer),
            pl.BlockSpec((1, D_MODEL, ROUTER_LANES), layer),
            pl.BlockSpec((1, 1, ROUTER_LANES), layer),
        ],
        out_specs=[
            pl.BlockSpec((TM_PROJ, D_MODEL), row),
            pl.BlockSpec((TM_PROJ, D_MODEL), row),
            pl.BlockSpec((TM_PROJ, ROUTER_LANES), row),
            pl.BlockSpec((TM_PROJ, ROUTER_LANES), row),
            pl.BlockSpec((1, ROUTER_LANES), lambda i: (0, 0)),
        ],
        out_shape=[
            jax.ShapeDtypeStruct((t, D_MODEL), F32),
            jax.ShapeDtypeStruct((t, D_MODEL), F32),
            jax.ShapeDtypeStruct((t, ROUTER_LANES), I32),
            jax.ShapeDtypeStruct((t, ROUTER_LANES), F32),
            jax.ShapeDtypeStruct((1, ROUTER_LANES), I32),
        ],
        scratch_shapes=[pltpu.VMEM((1, ROUTER_LANES), F32)],
        compiler_params=pltpu.CompilerParams(
            dimension_semantics=("arbitrary",), vmem_limit_bytes=VMEM_LIMIT),
        name="out_proj",
    )(mixed2, x2, w_bf, gain, r_hi, r_lo, r_b)


def _row_copy(src_hbm, idx_ref, dst, sem, r):
    return pltpu.make_async_copy(src_hbm.at[pl.ds(idx_ref[0, 0, r], 1)], dst.at[pl.ds(r, 1)], sem)


def _start_rows(src_hbm, idx_ref, dst, sem, n_rows):
    def body(j, carry):
        for k in range(ISSUE_UNROLL):
            _row_copy(src_hbm, idx_ref, dst, sem, j * ISSUE_UNROLL + k).start()
        return carry
    lax.fori_loop(0, n_rows // ISSUE_UNROLL, body, 0)


def _wait_rows(src_hbm, dst, sem, n_rows):
    pltpu.make_async_copy(src_hbm.at[pl.ds(0, n_rows)], dst, sem).wait()


def _dispatch_kernel(pad_lo_ref, pad_len_ref, nu_ref, slots_ref, h_ref, hs_hbm, stage, zeros, sem, pad_sem):
    i = pl.program_id(0)
    n = pl.num_programs(0)
    s = i % 2
    n_tiles = hs_hbm.shape[0] // TM_MOE
    min_used = (2 * n * TM_DISPATCH) // TM_MOE

    def wait_stage(which):
        for _ in range(2):
            pltpu.make_async_copy(stage.at[which], hs_hbm.at[pl.ds(0, TM_DISPATCH)], sem.at[which]).wait()

    def zero_copy(first_row, n_rows):
        return pltpu.make_async_copy(zeros.at[pl.ds(0, n_rows)], hs_hbm.at[pl.ds(first_row, n_rows)], pad_sem)

    def pad_copies(e):
        lo = pad_lo_ref[e]
        length = pad_len_ref[e]
        head = (-lo) & (PAD_ALIGN - 1)
        pairs = [(j < head, zero_copy(lo + j, 1)) for j in range(PAD_ALIGN - 1)]
        first = lo + head
        for p in PAD_PIECES:
            use = ((length - head) & p) != 0
            pairs.append((use, zero_copy(pl.multiple_of(first, PAD_ALIGN), p)))
            first = first + jnp.where(use, p, 0)
        return pairs

    def unused_tile_copies():
        half = TM_MOE // 2
        return [(j >= nu_ref[0], zero_copy(j * TM_MOE + k * half, half))
                for j in range(min_used, n_tiles) for k in range(2)]

    def for_each_zero_copy(action):
        def per_expert(e, carry):
            for cond, copy in pad_copies(e):
                pl.when(cond)(functools.partial(action, copy))
            return carry
        lax.fori_loop(0, N_EXPERTS, per_expert, 0)
        for cond, copy in unused_tile_copies():
            pl.when(cond)(functools.partial(action, copy))

    @pl.when(i == 0)
    def _():
        zeros[...] = jnp.zeros_like(zeros)
        for_each_zero_copy(lambda copy: copy.start())

    @pl.when(i >= 2)
    def _():
        wait_stage(s)

    stage[s] = h_ref[...]
    for r in range(2 * TM_DISPATCH):
        pltpu.make_async_copy(stage.at[s, pl.ds(r % TM_DISPATCH, 1)],
                              hs_hbm.at[pl.ds(slots_ref[0, 0, r], 1)], sem.at[s]).start(priority=r % 2)

    @pl.when(i == n - 1)
    def _():
        wait_stage(s)
        wait_stage(1 - s)
        for_each_zero_copy(lambda copy: copy.wait())


def _dispatch(h2, pad_lo, pad_len, n_used, slots3, n_slots):
    t = h2.shape[0]
    n = t // TM_DISPATCH
    return pl.pallas_call(
        _dispatch_kernel,
        grid_spec=pltpu.PrefetchScalarGridSpec(
            num_scalar_prefetch=3,
            grid=(n,),
            in_specs=[
                pl.BlockSpec((1, 1, 2 * TM_DISPATCH), lambda i, lo, ln, nu: (i, 0, 0),
                             memory_space=pltpu.SMEM),
                pl.BlockSpec((TM_DISPATCH, D_MODEL), lambda i, lo, ln, nu: (i, 0)),
            ],
            out_specs=pl.BlockSpec(memory_space=pl.ANY),
            scratch_shapes=[
                pltpu.VMEM((2, TM_DISPATCH, D_MODEL), F32),
                pltpu.VMEM((PAD_PIECES[0], D_MODEL), F32),
                pltpu.SemaphoreType.DMA((2,)),
                pltpu.SemaphoreType.DMA(()),
            ],
        ),
        out_shape=jax.ShapeDtypeStruct((n_slots, D_MODEL), F32),
        compiler_params=pltpu.CompilerParams(
            dimension_semantics=("arbitrary",), vmem_limit_bytes=VMEM_LIMIT),
        name="dispatch",
    )(pad_lo, pad_len, n_used, slots3, h2)


def _moe_kernel(te_ref, grp_ref, nxt_ref, nu_ref, x_ref, wg_hbm, wu_hbm, wd_hbm,
                y_ref, wg_st, wu_st, wd_st, wsem, wg_bf, wu_bf, wd_bf):
    i = pl.program_id(0)
    n_used = nu_ref[0]

    def weight_copies(e, par):
        return (pltpu.make_async_copy(wg_hbm.at[e], wg_st.at[par], wsem.at[par]),
                pltpu.make_async_copy(wu_hbm.at[e], wu_st.at[par], wsem.at[par]),
                pltpu.make_async_copy(wd_hbm.at[e], wd_st.at[par], wsem.at[par]))

    @pl.when(i == 0)
    def _():
        for c in weight_copies(te_ref[0], 0):
            c.start(priority=WEIGHT_DMA_PRIORITY)

    @pl.when(i < n_used)
    def _():
        @pl.when((i == 0) | (te_ref[i] != te_ref[jnp.maximum(i - 1, 0)]))
        def _():
            par = grp_ref[i] % 2
            for c in weight_copies(te_ref[i], par):
                c.wait()

            @pl.when(nxt_ref[i] >= 0)
            def _():
                for c in weight_copies(nxt_ref[i], 1 - par):
                    c.start(priority=WEIGHT_DMA_PRIORITY)

            wg_bf[...] = wg_st[par].astype(BF16)
            wu_bf[...] = wu_st[par].astype(BF16)
            wd_bf[...] = wd_st[par].astype(BF16)

        xb = x_ref[...].astype(BF16)
        g = jnp.dot(xb, wg_bf[...], preferred_element_type=F32)
        u = jnp.dot(xb, wu_bf[...], preferred_element_type=F32)
        a = (g * jax.nn.sigmoid(g) * u).astype(BF16)
        y_ref[...] = jnp.dot(a, wd_bf[...], preferred_element_type=F32)

    @pl.when(i >= n_used)
    def _():
        y_ref[...] = jnp.zeros_like(y_ref)


def _moe_ffn(hs, n_tiles, w_gate, w_up, w_down, tile_expert, tile_group, tile_next, n_used):
    hbm = pl.BlockSpec(memory_space=pl.ANY)
    tile = lambda i, te, gr, nx, nu: (i, 0)
    used_tile = lambda i, te, gr, nx, nu: (jnp.minimum(i, nu[0] - 1), 0)
    return pl.pallas_call(
        _moe_kernel,
        grid_spec=pltpu.PrefetchScalarGridSpec(
            num_scalar_prefetch=4,
            grid=(n_tiles,),
            in_specs=[pl.BlockSpec((TM_MOE, D_MODEL), used_tile), hbm, hbm, hbm],
            out_specs=pl.BlockSpec((TM_MOE, D_MODEL), tile),
            scratch_shapes=[
                pltpu.VMEM((2, D_MODEL, D_EXPERT), F32),
                pltpu.VMEM((2, D_MODEL, D_EXPERT), F32),
                pltpu.VMEM((2, D_EXPERT, D_MODEL), F32),
                pltpu.SemaphoreType.DMA((2,)),
                pltpu.VMEM((D_MODEL, D_EXPERT), BF16),
                pltpu.VMEM((D_MODEL, D_EXPERT), BF16),
                pltpu.VMEM((D_EXPERT, D_MODEL), BF16),
            ],
        ),
        out_shape=jax.ShapeDtypeStruct((n_tiles * TM_MOE, D_MODEL), F32),
        compiler_params=pltpu.CompilerParams(
            dimension_semantics=("arbitrary",), vmem_limit_bytes=VMEM_LIMIT),
        name="moe_ffn",
    )(tile_expert, tile_group, tile_next, n_used, hs, w_gate, w_up, w_down)


def _combine_kernel(first_ref, ahead_ref, x_ref, w_ref, y_hbm, o_ref, buf, sem):
    i = pl.program_id(0)
    n = pl.num_programs(0)
    rows = 2 * TM_COMBINE

    @pl.when(i == 0)
    def _():
        _start_rows(y_hbm, first_ref, buf.at[0], sem.at[0], rows)

    @pl.when(i + 1 < n)
    def _():
        nxt = (i + 1) % 2
        for r in range(rows):
            _row_copy(y_hbm, ahead_ref, buf.at[nxt], sem.at[nxt], r).start(priority=r % 2)

    slot = i % 2
    _wait_rows(y_hbm, buf.at[slot], sem.at[slot], rows)
    w = w_ref[...]
    o_ref[...] = (x_ref[...] + w[:, 0:1] * buf[slot, :TM_COMBINE]
                  + w[:, 1:2] * buf[slot, TM_COMBINE:])


def _combine(x1, rw, y, slots3):
    t = x1.shape[0]
    n = t // TM_COMBINE
    row = lambda i: (i, 0)
    smem_tile = lambda index_map: pl.BlockSpec((1, 1, 2 * TM_COMBINE), index_map,
                                               memory_space=pltpu.SMEM)
    return pl.pallas_call(
        _combine_kernel,
        grid=(n,),
        in_specs=[
            smem_tile(lambda i: (0, 0, 0)),
            smem_tile(lambda i: (jnp.minimum(i + 1, n - 1), 0, 0)),
            pl.BlockSpec((TM_COMBINE, D_MODEL), row),
            pl.BlockSpec((TM_COMBINE, ROUTER_LANES), row),
            pl.BlockSpec(memory_space=pl.ANY),
        ],
        out_specs=pl.BlockSpec((TM_COMBINE, D_MODEL), row),
        out_shape=jax.ShapeDtypeStruct((t, D_MODEL), F32),
        scratch_shapes=[
            pltpu.VMEM((2, 2 * TM_COMBINE, D_MODEL), F32),
            pltpu.SemaphoreType.DMA((2,)),
        ],
        compiler_params=pltpu.CompilerParams(
            dimension_semantics=("arbitrary",), vmem_limit_bytes=VMEM_LIMIT),
        name="combine",
    )(slots3, slots3, x1, rw, y)


def _slot_tables(l, ri, counts_row, n_tiles):
    t = ri.shape[0]
    experts = jnp.arange(N_EXPERTS, dtype=I32)
    counts = counts_row[0, :N_EXPERTS]
    tiles_per = (counts + TM_MOE - 1) // TM_MOE
    tile_end = jnp.cumsum(tiles_per)
    start = (tile_end - tiles_per) * TM_MOE
    n_used = tile_end[-1:].astype(I32)
    nonempty = counts > 0
    group_of = jnp.cumsum(nonempty.astype(I32)) - 1
    later = (experts[None, :] > experts[:, None]) & nonempty[None, :]
    next_of = jnp.min(jnp.where(later, experts[None, :], N_EXPERTS), axis=1)
    next_of = jnp.where(next_of < N_EXPERTS, next_of + l * N_EXPERTS, -1)
    tile_idx = jnp.arange(n_tiles, dtype=I32)
    tile_e = jnp.minimum(jnp.sum((tile_end[None, :] <= tile_idx[:, None]).astype(I32), axis=1),
                         N_EXPERTS - 1)
    pick = tile_e[:, None] == experts[None, :]
    tile_group = jnp.sum(jnp.where(pick, group_of[None, :], 0), axis=1)
    tile_next = jnp.sum(jnp.where(pick, next_of[None, :], 0), axis=1)
    tile_expert = tile_e + l * N_EXPERTS
    eid = ri[:, 0:2]
    slot = ri[:, 2:4] + jnp.sum(jnp.where(eid[..., None] == experts, start, 0), axis=-1)
    def by_tile(tm):
        return jnp.swapaxes(slot.reshape(t // tm, tm, 2), 1, 2).reshape(t // tm, 1, 2 * tm)
    slots3 = (by_tile(TM_DISPATCH), by_tile(TM_COMBINE))
    tiles = (tile_expert.astype(I32), tile_group.astype(I32), tile_next.astype(I32), n_used)
    pad_lo = (start + counts).astype(I32)
    pad_len = (tiles_per * TM_MOE - counts).astype(I32)
    return pad_lo, pad_len, tiles, slots3


def _rope_freq_row():
    inv_freq = ROPE_THETA ** (-(jnp.arange(0, ROT_DIM, 2, dtype=F32) / ROT_DIM))
    return jnp.concatenate(
        [inv_freq, inv_freq, jnp.zeros((HEAD_DIM - ROT_DIM,), F32)]).reshape(1, HEAD_DIM)


def _router_operands(w_rg, b_rg, w_re, b_re):
    depth = w_rg.shape[0]
    pad = ROUTER_LANES - N_EXPERTS - N_GROUPS
    w = jnp.concatenate([w_re, w_rg, jnp.zeros((depth, D_MODEL, pad), F32)], axis=2)
    b = jnp.concatenate([b_re, b_rg, jnp.zeros((depth, pad), F32)], axis=1)
    hi = w.astype(BF16)
    lo = (w - hi.astype(F32)).astype(BF16)
    return hi, lo, b


def kernel(x, positions, norm_mix, w_in, w_pool, pool_scale, q_norm, k_norm, sink, branch_gain_pool,
           branch_gain_attn, w_out, norm_ffn, w_router_group, b_router_group, w_router_expert,
           b_router_expert, w_gate, w_up, w_down):
    b, s, d = x.shape
    t = b * s
    depth = norm_mix.shape[0]
    x2 = x.reshape(t, d)
    pos2 = positions.reshape(t, 1)
    freq = _rope_freq_row()
    w_in_bf = w_in.astype(BF16)
    w_out_bf = w_out.astype(BF16)
    w_pool_bf = w_pool.astype(BF16)
    r_hi, r_lo, r_b = _router_operands(w_router_group, b_router_group, w_router_expert,
                                       b_router_expert)
    wg = w_gate.reshape(depth * N_EXPERTS, D_MODEL, D_EXPERT)
    wu = w_up.reshape(depth * N_EXPERTS, D_MODEL, D_EXPERT)
    wd = w_down.reshape(depth * N_EXPERTS, D_EXPERT, D_MODEL)
    n_tiles = (2 * t) // TM_MOE + N_EXPERTS
    rows = lambda v: v.reshape(depth, 1, v.shape[-1])
    for l in range(depth):
        u, qkv = _in_proj(l, x2, pos2, rows(norm_mix), w_in_bf, rows(q_norm), rows(k_norm), freq)
        mixed = _mixer(l, u.reshape(b, s, POOL_WIDTH), qkv.reshape(b, s, QKV_WIDTH), sink, w_pool_bf,
                       rows(pool_scale), rows(branch_gain_pool), rows(branch_gain_attn))
        x1, h, ri, rw, counts_row = _out_proj(l, mixed.reshape(t, D_MODEL), x2, w_out_bf,
                                              rows(norm_ffn), r_hi, r_lo, rows(r_b))
        pad_lo, pad_len, tiles, slots3 = _slot_tables(l, ri, counts_row, n_tiles)
        hs = _dispatch(h, pad_lo, pad_len, tiles[3], slots3[0], n_tiles * TM_MOE)
        y = _moe_ffn(hs, n_tiles, wg, wu, wd, *tiles)
        x2 = _combine(x1, rw, y, slots3[1])
    return x2.reshape(b, s, d)
```
